```python
import jax
import jax.numpy as jnp
from jax import lax
import numpy as np

D_MODEL = 1024
BATCH = 4
SEQ = 4096
DEPTH = 1
DEC_BATCH = 8
DEC_SEQ = 64
PAST_LEN = 4096

CHUNK = 64
HEAD_DIM = 64
A_HEADS = 8
B_HEADS = 8
IDX_HEADS = 4
IDX_DIM = 64
TOPK_MAX = 256
ROPE_THETA = 500000.0
ROT_DIM = HEAD_DIM // 4
Q_BLOCK = 128
N_GROUPS = 4
EXPERTS_PER_GROUP = 8
N_EXPERTS = N_GROUPS * EXPERTS_PER_GROUP
TOP_K_EXPERTS = 2
D_EXPERT = 256
N_BRANCHES = 2
RMS_EPS = 1e-6
A_WIDTH = A_HEADS * HEAD_DIM
B_WIDTH = B_HEADS * HEAD_DIM
IN_WIDTH = 3 * A_WIDTH + IDX_HEADS * IDX_DIM + IDX_DIM + IDX_HEADS + 3 * B_WIDTH + N_BRANCHES * D_MODEL

kernel_name = 'hybrid_dsa_stickbreak_hmoe_stream_step'


def _rms(x, g):
    x32 = x.astype(jnp.float32)
    y = x32 * lax.rsqrt(jnp.mean(x32 * x32, axis=-1, keepdims=True) + RMS_EPS)
    return (y * g.astype(jnp.float32)).astype(x.dtype)


def _partial_rope(x, pos):
    half = ROT_DIM // 2
    freqs = ROPE_THETA ** (-jnp.arange(0, ROT_DIM, 2, dtype=jnp.float32) / ROT_DIM)
    ang = pos.astype(jnp.float32)[:, None] * freqs[None, :]
    cos = jnp.cos(ang)[None, :, None, :]
    sin = jnp.sin(ang)[None, :, None, :]
    x32 = x.astype(jnp.float32)
    x1 = x32[..., :half]
    x2 = x32[..., half:ROT_DIM]
    out = jnp.concatenate([x1 * cos - x2 * sin, x2 * cos + x1 * sin, x32[..., ROT_DIM:]], axis=-1)
    return out.astype(x.dtype)


def _to_blocks(x, blk):
    b, t = x.shape[0], x.shape[1]
    return jnp.moveaxis(x.reshape((b, t // blk, blk) + x.shape[2:]), 1, 0)


def _from_blocks(y):
    nb, b, blk = y.shape[0], y.shape[1], y.shape[2]
    return jnp.moveaxis(y, 0, 1).reshape((b, nb * blk) + y.shape[3:])


def _dsa_attention(q, q_idx, w_idx, q_pos, k, v, k_idx, k_pos):
    s_len = k.shape[1]
    topk = min(TOPK_MAX, s_len // 4)
    blk = min(Q_BLOCK, q.shape[1])
    k_chunk = k_pos // CHUNK

    def block(args):
        qb, qib, wb, pb = args
        rel = jax.nn.relu(jnp.einsum('bqhd,bsd->bqhs', qib.astype(jnp.float32), k_idx.astype(jnp.float32)) * IDX_DIM ** -0.5)
        score = jnp.einsum('bqh,bqhs->bqs', wb.astype(jnp.float32), rel)
        admissible = k_chunk[None, :] <= (pb // CHUNK)[:, None]
        score = jnp.where(admissible[None], score, -jnp.inf)
        top_val, top_idx = lax.top_k(score, topk)
        valid = jnp.isfinite(top_val)
        kg = jax.vmap(lambda kk, ii: kk[ii])(k, top_idx)
        vg = jax.vmap(lambda vv, ii: vv[ii])(v, top_idx)
        logits = jnp.einsum('bqhd,bqkhd->bhqk', qb.astype(jnp.float32), kg.astype(jnp.float32)) * HEAD_DIM ** -0.5
        logits = jnp.where(valid[:, None], logits, -jnp.inf)
        p = jax.nn.softmax(logits, axis=-1)
        return jnp.einsum('bhqk,bqkhd->bqhd', p.astype(v.dtype), vg)

    out = lax.map(block, (_to_blocks(q, blk), _to_blocks(q_idx, blk), _to_blocks(w_idx, blk), q_pos.reshape(-1, blk)))
    return _from_blocks(out)


def _stick_breaking(q, k, v, q_pos, k_pos):
    blk = min(Q_BLOCK, q.shape[1])

    def block(args):
        qb, pb = args
        z = jnp.einsum('bqhd,bshd->bhqs', qb.astype(jnp.float32), k.astype(jnp.float32)) * HEAD_DIM ** -0.5
        causal = (k_pos[None, :] < pb[:, None])[None, None]
        log_keep = jnp.where(causal, jax.nn.log_sigmoid(-z), 0.0)
        later = lax.cumsum(log_keep, axis=3, reverse=True) - log_keep
        a = jnp.where(causal, jnp.exp(jax.nn.log_sigmoid(z) + later), 0.0)
        return jnp.einsum('bhqs,bshd->bqhd', a.astype(v.dtype), v)

    out = lax.map(block, (_to_blocks(q, blk), q_pos.reshape(-1, blk)))
    return _from_blocks(out)


def _hier_moe(h, w_rg, b_rg, w_re, b_re, w_e_gate, w_e_up, w_e_down):
    b, t, d = h.shape
    n = h.reshape(b * t, d)
    g_logits = (n @ w_rg + b_rg).astype(jnp.float32)
    g_prob = jax.nn.softmax(g_logits, axis=-1)
    g_sel = jnp.argmax(g_logits, axis=-1)
    p_group = jnp.take_along_axis(g_prob, g_sel[:, None], axis=1)
    e_logits = (n @ w_re + b_re).astype(jnp.float32).reshape(-1, N_GROUPS, EXPERTS_PER_GROUP)
    e_in_group = jnp.take_along_axis(e_logits, g_sel[:, None, None], axis=1)[:, 0]
    top_val, top_i = lax.top_k(e_in_group, TOP_K_EXPERTS)
    w_top = jax.nn.softmax(top_val, axis=-1) * p_group
    expert_id = g_sel[:, None] * EXPERTS_PER_GROUP + top_i
    combine = jnp.sum(jax.nn.one_hot(expert_id, N_EXPERTS, dtype=jnp.float32) * w_top[..., None], axis=1)
    out = jnp.zeros((b * t, d), jnp.float32)
    for e in range(N_EXPERTS):
        act = jax.nn.silu(n @ w_e_gate[e]) * (n @ w_e_up[e])
        out = out + combine[:, e:e + 1] * (act @ w_e_down[e]).astype(jnp.float32)
    return out.astype(h.dtype).reshape(b, t, d)


def _layer(x, c, past_ak, past_av, past_akidx, past_bk, past_bv, p):
    b, t, _ = x.shape
    past = past_ak.shape[1]
    pos = past + jnp.arange(t, dtype=jnp.int32)
    k_pos = jnp.arange(past + t, dtype=jnp.int32)
    mod = jax.nn.silu(c) @ p['w_ada'] + p['b_ada']
    sh1, sc1, g1, sh2, sc2, g2 = [m[:, None, :] for m in jnp.split(mod, 6, axis=-1)]
    h = _rms(x, p['norm1_g']) * (1 + sc1) + sh1
    proj = h @ p['w_in']
    widths = [A_WIDTH, A_WIDTH, A_WIDTH, IDX_HEADS * IDX_DIM, IDX_DIM, IDX_HEADS, B_WIDTH, B_WIDTH, B_WIDTH, N_BRANCHES * D_MODEL]
    offsets = np.cumsum(widths)[:-1].tolist()
    qa, ka, va, qi, ki, wi, qb, kb, vb, gates = jnp.split(proj, offsets, axis=-1)
    qa = _partial_rope(_rms(qa.reshape(b, t, A_HEADS, HEAD_DIM), p['qnorm_g']), pos)
    ka = _partial_rope(_rms(ka.reshape(b, t, A_HEADS, HEAD_DIM), p['knorm_g']), pos)
    va = va.reshape(b, t, A_HEADS, HEAD_DIM)
    qi = _partial_rope(qi.reshape(b, t, IDX_HEADS, IDX_DIM), pos)
    ki = _partial_rope(ki[:, :, None, :], pos)[:, :, 0, :]
    qb = qb.reshape(b, t, B_HEADS, HEAD_DIM)
    kb = kb.reshape(b, t, B_HEADS, HEAD_DIM)
    vb = vb.reshape(b, t, B_HEADS, HEAD_DIM)
    ka_all = jnp.concatenate([past_ak, ka], axis=1)
    va_all = jnp.concatenate([past_av, va], axis=1)
    ki_all = jnp.concatenate([past_akidx, ki], axis=1)
    kb_all = jnp.concatenate([past_bk, kb], axis=1)
    vb_all = jnp.concatenate([past_bv, vb], axis=1)
    o_a = _dsa_attention(qa, qi, wi, pos, ka_all, va_all, ki_all, k_pos)
    o_b = _stick_breaking(qb, kb_all, vb_all, pos, k_pos)
    gate_a, gate_b = jnp.split(jax.nn.sigmoid(gates), 2, axis=-1)
    merged = gate_a * (o_a.reshape(b, t, A_WIDTH) @ p['w_up_a']) + gate_b * (o_b.reshape(b, t, B_WIDTH) @ p['w_up_b'])
    y = x + g1 * (merged @ p['w_out'])
    h2 = _rms(y, p['norm2_g']) * (1 + sc2) + sh2
    out = y + g2 * _hier_moe(h2, p['w_rg'], p['b_rg'], p['w_re'], p['b_re'], p['w_e_gate'], p['w_e_up'], p['w_e_down'])
    return out, (ka, va, ki, kb, vb)


def setup_inputs(seed: int = 0) -> dict:
    key = jax.random.key(seed)
    ks = jax.random.split(key, 26)

    def nrm(k, shape, scale=1.0):
        return jax.random.normal(k, shape, jnp.float32) * scale

    return {
        'x_prompt': nrm(ks[0], (BATCH, SEQ, D_MODEL)),
        'x_sample': nrm(ks[1], (DEC_BATCH, DEC_SEQ, D_MODEL)),
        'cache_a_k': nrm(ks[2], (DEPTH, DEC_BATCH, PAST_LEN, A_HEADS, HEAD_DIM)),
        'cache_a_v': nrm(ks[3], (DEPTH, DEC_BATCH, PAST_LEN, A_HEADS, HEAD_DIM)),
        'cache_a_kidx': nrm(ks[4], (DEPTH, DEC_BATCH, PAST_LEN, IDX_DIM)),
        'cache_b_k': nrm(ks[5], (DEPTH, DEC_BATCH, PAST_LEN, B_HEADS, HEAD_DIM)),
        'cache_b_v': nrm(ks[6], (DEPTH, DEC_BATCH, PAST_LEN, B_HEADS, HEAD_DIM)),
        'c_prompt': nrm(ks[7], (BATCH, D_MODEL)),
        'c_sample': nrm(ks[8], (DEC_BATCH, D_MODEL)),
        'w_ada': nrm(ks[9], (DEPTH, D_MODEL, 6 * D_MODEL), 0.5 * D_MODEL ** -0.5),
        'b_ada': nrm(ks[10], (DEPTH, 6 * D_MODEL), 0.02),
        'norm1_g': 1.0 + nrm(ks[11], (DEPTH, D_MODEL), 0.02),
        'w_in': nrm(ks[12], (DEPTH, D_MODEL, IN_WIDTH), D_MODEL ** -0.5),
        'qnorm_g': 1.0 + nrm(ks[13], (DEPTH, HEAD_DIM), 0.02),
        'knorm_g': 1.0 + nrm(ks[14], (DEPTH, HEAD_DIM), 0.02),
        'w_up_a': nrm(ks[15], (DEPTH, A_WIDTH, D_MODEL), A_WIDTH ** -0.5),
        'w_up_b': nrm(ks[16], (DEPTH, B_WIDTH, D_MODEL), B_WIDTH ** -0.5),
        'w_out': nrm(ks[17], (DEPTH, D_MODEL, D_MODEL), D_MODEL ** -0.5),
        'norm2_g': 1.0 + nrm(ks[18], (DEPTH, D_MODEL), 0.02),
        'w_rg': nrm(ks[19], (DEPTH, D_MODEL, N_GROUPS), D_MODEL ** -0.5),
        'b_rg': nrm(ks[20], (DEPTH, N_GROUPS), 0.01),
        'w_re': nrm(ks[21], (DEPTH, D_MODEL, N_EXPERTS), D_MODEL ** -0.5),
        'b_re': nrm(ks[22], (DEPTH, N_EXPERTS), 0.01),
        'w_e_gate': nrm(ks[23], (DEPTH, N_EXPERTS, D_MODEL, D_EXPERT), D_MODEL ** -0.5),
        'w_e_up': nrm(ks[24], (DEPTH, N_EXPERTS, D_MODEL, D_EXPERT), D_MODEL ** -0.5),
        'w_e_down': nrm(ks[25], (DEPTH, N_EXPERTS, D_EXPERT, D_MODEL), D_EXPERT ** -0.5),
    }


def reference(x_prompt, x_sample, cache_a_k, cache_a_v, cache_a_kidx, cache_b_k, cache_b_v, c_prompt, c_sample,
              w_ada, b_ada, norm1_g, w_in, qnorm_g, knorm_g, w_up_a, w_up_b, w_out, norm2_g,
              w_rg, b_rg, w_re, b_re, w_e_gate, w_e_up, w_e_down):
    bp = x_prompt.shape[0]
    dt = x_prompt.dtype
    empty_heads_a = jnp.zeros((bp, 0, A_HEADS, HEAD_DIM), dt)
    empty_idx = jnp.zeros((bp, 0, IDX_DIM), dt)
    empty_heads_b = jnp.zeros((bp, 0, B_HEADS, HEAD_DIM), dt)
    y_prompt = x_prompt
    y_sample = x_sample
    rows_p = []
    rows_s = []
    for l in range(DEPTH):
        p = dict(w_ada=w_ada[l], b_ada=b_ada[l], norm1_g=norm1_g[l], w_in=w_in[l], qnorm_g=qnorm_g[l],
                 knorm_g=knorm_g[l], w_up_a=w_up_a[l], w_up_b=w_up_b[l], w_out=w_out[l], norm2_g=norm2_g[l],
                 w_rg=w_rg[l], b_rg=b_rg[l], w_re=w_re[l], b_re=b_re[l],
                 w_e_gate=w_e_gate[l], w_e_up=w_e_up[l], w_e_down=w_e_down[l])
        y_prompt, new_p = _layer(y_prompt, c_prompt, empty_heads_a, empty_heads_a, empty_idx,
                                 empty_heads_b, empty_heads_b, p)
        y_sample, new_s = _layer(y_sample, c_sample, cache_a_k[l], cache_a_v[l], cache_a_kidx[l],
                                 cache_b_k[l], cache_b_v[l], p)
        rows_p.append(new_p)
        rows_s.append(new_s)
    new_a_k_prompt = jnp.stack([r[0] for r in rows_p])
    new_a_v_prompt = jnp.stack([r[1] for r in rows_p])
    new_a_kidx_prompt = jnp.stack([r[2] for r in rows_p])
    new_b_k_prompt = jnp.stack([r[3] for r in rows_p])
    new_b_v_prompt = jnp.stack([r[4] for r in rows_p])
    new_a_k_sample = jnp.stack([r[0] for r in rows_s])
    new_a_v_sample = jnp.stack([r[1] for r in rows_s])
    new_a_kidx_sample = jnp.stack([r[2] for r in rows_s])
    new_b_k_sample = jnp.stack([r[3] for r in rows_s])
    new_b_v_sample = jnp.stack([r[4] for r in rows_s])
    return (y_prompt, y_sample,
            new_a_k_prompt, new_a_v_prompt, new_a_kidx_prompt, new_b_k_prompt, new_b_v_prompt,
            new_a_k_sample, new_a_v_sample, new_a_kidx_sample, new_b_k_sample, new_b_v_sample)
```

```python
import functools

import jax
import jax.numpy as jnp
from jax import lax
from jax.experimental import pallas as pl
from jax.experimental.pallas import tpu as pltpu

D_MODEL = 1024
CHUNK = 64
HEAD_DIM = 64
A_HEADS = 8
B_HEADS = 8
IDX_HEADS = 4
IDX_DIM = 64
TOPK_MAX = 256
ROPE_THETA = 500000.0
ROT_DIM = HEAD_DIM // 4
N_GROUPS = 4
EXPERTS_PER_GROUP = 8
N_EXPERTS = N_GROUPS * EXPERTS_PER_GROUP
D_EXPERT = 256
RMS_EPS = 1e-6
A_WIDTH = A_HEADS * HEAD_DIM
B_WIDTH = B_HEADS * HEAD_DIM

LANES = 128
IDX_SEG = 384
ROUTER_LANES = 128
EXPERT_LANE0 = N_GROUPS
VMEM_LIMIT = 56 * 1024 * 1024

F32 = jnp.float32
BF16 = jnp.bfloat16
I32 = jnp.int32
NEG_BIG = -1e30
INT_MIN = -(2 ** 31)
KEY_NEG_INF = (0xFF800000 - (1 << 32)) ^ 0x7FFFFFFF
IDX_BIG = 1 << 30


def _dot(a, b):
    return jnp.dot(a, b, preferred_element_type=F32)


def _dot_nt(a, b):
    return lax.dot_general(a, b, (((1,), (1,)), ((), ())), preferred_element_type=F32)


def _split_bf16(a):
    hi = a.astype(BF16)
    lo = (a - hi.astype(F32)).astype(BF16)
    return hi, lo


def _dot3(a, b):
    ah, al = _split_bf16(a)
    bh, bl = _split_bf16(b)
    return _dot(ah, bh) + (_dot(ah, bl) + _dot(al, bh))


def _silu(x):
    return x * (1.0 / (1.0 + jnp.exp(-x)))


def _sigmoid(x):
    return 1.0 / (1.0 + jnp.exp(-x))


def _rows_times(x, vec, nb):
    if nb == 1:
        return x * vec
    tm, d = x.shape
    return (x.reshape(nb, tm // nb, d) * vec[:, None, :]).reshape(tm, d)


def _rows_plus(x, vec, nb):
    if nb == 1:
        return x + vec
    tm, d = x.shape
    return (x.reshape(nb, tm // nb, d) + vec[:, None, :]).reshape(tm, d)


def _rms_rows(x):
    return x * lax.rsqrt(jnp.mean(x * x, axis=-1, keepdims=True) + RMS_EPS)


def _ada_kernel(c_ref, w_ref, b_ref, o_ref):
    o_ref[...] = _dot3(_silu(c_ref[...]), w_ref[...]) + b_ref[...]


def _ada(c_all, w_ada, b_ada):
    rows = c_all.shape[0]
    n = w_ada.shape[1]
    tn = 1536
    return pl.pallas_call(
        _ada_kernel,
        grid=(n // tn,),
        in_specs=[pl.BlockSpec((rows, D_MODEL), lambda j: (0, 0)),
                  pl.BlockSpec((D_MODEL, tn), lambda j: (0, j)),
                  pl.BlockSpec((1, tn), lambda j: (0, j))],
        out_specs=pl.BlockSpec((rows, tn), lambda j: (0, j)),
        out_shape=jax.ShapeDtypeStruct((rows, n), F32),
        compiler_params=pltpu.CompilerParams(dimension_semantics=("arbitrary",), vmem_limit_bytes=VMEM_LIMIT),
        name="ada_mod",
    )(c_all, w_ada, b_ada)


def _rope_lanes(a, tab_ref, reps):
    width = a.shape[-1]
    cos = jnp.tile(tab_ref[0], (1, reps)) if reps > 1 else tab_ref[0]
    sin_lo = jnp.tile(tab_ref[1], (1, reps)) if reps > 1 else tab_ref[1]
    sin_hi = jnp.tile(tab_ref[2], (1, reps)) if reps > 1 else tab_ref[2]
    half = ROT_DIM // 2
    return a * cos + pltpu.roll(a, width - half, 1) * sin_lo + pltpu.roll(a, half, 1) * sin_hi


def _proj_kernel(x_ref, mod_ref, n1g_ref, wa_ref, wv_ref, wi_ref, wg_ref, bd_ref, gqk_ref, tab_ref, tabi_ref,
                 qa_ref, ka_ref, kab_ref, va_ref, vab_ref, qb_ref, kb_ref, kbb_ref, vb_ref, vbb_ref,
                 qi_ref, kiw_ref, kib_ref, gates_ref, *, nb):
    x = x_ref[...]
    sh1 = mod_ref[:, 0, :]
    sc1 = mod_ref[:, 1, :]
    h = _rows_plus(_rows_times(_rms_rows(x) * n1g_ref[...], 1.0 + sc1, nb), sh1, nb)
    hb = h.astype(BF16)

    a = _dot(hb, wa_ref[...])
    ms = _dot((a * a).astype(BF16), bd_ref[...])
    a = a * lax.rsqrt(ms + RMS_EPS) * gqk_ref[...]
    a = _rope_lanes(a, tab_ref, a.shape[-1] // LANES)
    qa_ref[...] = (a[:, :A_WIDTH] * (HEAD_DIM ** -0.5)).astype(BF16)
    ka = a[:, A_WIDTH:]
    ka_ref[...] = ka
    kab_ref[...] = ka.astype(BF16)

    v = _dot(hb, wv_ref[...])
    va = v[:, :A_WIDTH]
    va_ref[...] = va
    vab_ref[...] = va.astype(BF16)
    qb_ref[...] = (v[:, A_WIDTH:A_WIDTH + B_WIDTH] * (HEAD_DIM ** -0.5)).astype(BF16)
    kb = v[:, A_WIDTH + B_WIDTH:A_WIDTH + 2 * B_WIDTH]
    kb_ref[...] = kb
    kbb_ref[...] = kb.astype(BF16)
    vb = v[:, A_WIDTH + 2 * B_WIDTH:]
    vb_ref[...] = vb
    vbb_ref[...] = vb.astype(BF16)

    i = _dot(hb, wi_ref[...])
    nq = IDX_HEADS * IDX_DIM
    qi = _rope_lanes(i[:, :nq], tab_ref, nq // LANES)
    kiw = _rope_lanes(i[:, nq:], tabi_ref, 1)
    qi_ref[...] = (qi * (IDX_DIM ** -0.5)).astype(BF16)
    kiw_ref[...] = kiw
    kib_ref[...] = kiw[:, :IDX_DIM].astype(BF16)

    gates_ref[...] = _sigmoid(_dot(hb, wg_ref[...]))


def _proj(x2, mod, n1g, wa, wv, wi, wg, bd, gqk, tab, tabi, *, t_len, tm):
    n = x2.shape[0]
    nb = max(1, tm // t_len)
    tiles_per_batch = max(1, t_len // tm)
    tab_tiles = tab.shape[1] // tm
    row = lambda i: (i, 0)
    const2 = lambda i: (0, 0)
    tabmap = lambda i: (0, i % tab_tiles, 0)
    in_specs = [
        pl.BlockSpec((tm, D_MODEL), row),
        pl.BlockSpec((nb, 6, D_MODEL), lambda i: (i // tiles_per_batch, 0, 0)),
        pl.BlockSpec((1, D_MODEL), const2),
        pl.BlockSpec(wa.shape, const2),
        pl.BlockSpec(wv.shape, const2),
        pl.BlockSpec(wi.shape, const2),
        pl.BlockSpec(wg.shape, const2),
        pl.BlockSpec(bd.shape, const2),
        pl.BlockSpec((1, 2 * A_WIDTH), const2),
        pl.BlockSpec((3, tm, LANES), tabmap),
        pl.BlockSpec((3, tm, LANES), tabmap),
    ]
    outs = [
        ("qa", A_WIDTH, BF16), ("ka", A_WIDTH, F32), ("kab", A_WIDTH, BF16),
        ("va", A_WIDTH, F32), ("vab", A_WIDTH, BF16),
        ("qb", B_WIDTH, BF16), ("kb", B_WIDTH, F32), ("kbb", B_WIDTH, BF16),
        ("vb", B_WIDTH, F32), ("vbb", B_WIDTH, BF16),
        ("qi", IDX_HEADS * IDX_DIM, BF16), ("kiw", LANES, F32), ("kib", IDX_DIM, BF16),
        ("gates", 2 * D_MODEL, F32),
    ]
    res = pl.pallas_call(
        functools.partial(_proj_kernel, nb=nb),
        grid=(n // tm,),
        in_specs=in_specs,
        out_specs=[pl.BlockSpec((tm, w), row) for _, w, _ in outs],
        out_shape=[jax.ShapeDtypeStruct((n, w), dt) for _, w, dt in outs],
        compiler_params=pltpu.CompilerParams(dimension_semantics=("arbitrary",), vmem_limit_bytes=VMEM_LIMIT),
        name="in_proj",
    )(x2, mod, n1g, wa, wv, wi, wg, bd, gqk, tab, tabi)
    return dict(zip([o[0] for o in outs], res))


def _dsa_kernel(qi_ref, wt_ref, qa_ref, ki_ref, ka_ref, vt_ref, o_ref,
                key_scr, bias_scr, m_scr, l_scr, acc_scr, *, tq, kb, s_real, past, topk, idx_bits):
    qt = pl.program_id(1)
    q0 = past + qt * tq
    kmax = jnp.minimum(((q0 + tq - 1) // CHUNK + 1) * CHUNK, s_real)
    nblk = (kmax + kb - 1) // kb
    qchunk = (q0 + lax.broadcasted_iota(I32, (1, tq), 1)) // CHUNK
    kf = float(topk)

    def score_block(j, carry):
        r0 = pl.multiple_of(j * kb, kb)
        kib = ki_ref[0, pl.ds(r0, kb), :]
        acc = jnp.zeros((kb, tq), F32)
        for h in range(IDX_HEADS):
            rel = jnp.maximum(_dot_nt(kib, qi_ref[0, h]), 0.0)
            acc = acc + wt_ref[0, h:h + 1, :] * rel
        kpos = r0 + lax.broadcasted_iota(I32, (kb, tq), 0)
        adm = ((kpos // CHUNK) <= qchunk) & (kpos < s_real)
        s = jnp.where(adm, acc + 0.0, -jnp.inf)
        bits = pltpu.bitcast(s, I32)
        key_scr[pl.ds(r0, kb), :] = jnp.where(bits < 0, bits ^ 0x7FFFFFFF, bits)
        return carry

    lax.fori_loop(0, nblk, score_block, 0)

    def count(pred):
        def body(j, acc):
            r0 = pl.multiple_of(j * kb, kb)
            hit = jnp.where(pred(key_scr[pl.ds(r0, kb), :], r0), 1.0, 0.0)
            return acc + jnp.sum(hit.reshape(kb // 8, 8, tq), axis=0)
        acc = lax.fori_loop(0, nblk, body, jnp.zeros((8, tq), F32))
        return jnp.sum(acc, axis=0, keepdims=True)

    c0 = count(lambda blk, r0: blk >= 0)
    t0 = jnp.where(c0 >= kf, 0, INT_MIN).astype(I32)

    def value_bit(i, t):
        cand = t | jnp.left_shift(jnp.int32(1), 30 - i)
        return jnp.where(count(lambda blk, r0: blk >= cand) >= kf, cand, t)

    t = lax.fori_loop(0, 31, value_bit, t0)

    def rank_block(j, carry):
        r0 = pl.multiple_of(j * kb, kb)
        blk = key_scr[pl.ds(r0, kb), :]
        kpos = r0 + lax.broadcasted_iota(I32, (kb, tq), 0)
        rk = jnp.where(blk > t, -1, jnp.where(blk == t, kpos, IDX_BIG))
        key_scr[pl.ds(r0, kb), :] = jnp.where(blk == KEY_NEG_INF, IDX_BIG, rk)
        return carry

    lax.fori_loop(0, nblk, rank_block, 0)

    def index_bit(i, jstar):
        cand = jstar | jnp.left_shift(jnp.int32(1), idx_bits - 1 - i)
        return jnp.where(count(lambda blk, r0: blk < cand) < kf, cand, jstar)

    jstar = lax.fori_loop(0, idx_bits, index_bit, jnp.zeros((1, tq), I32))

    def bias_block(j, carry):
        r0 = pl.multiple_of(j * kb, kb)
        bias_scr[pl.ds(r0, kb), :] = jnp.where(key_scr[pl.ds(r0, kb), :] <= jstar, 0.0, NEG_BIG)
        return carry

    lax.fori_loop(0, nblk, bias_block, 0)

    m_scr[...] = jnp.full(m_scr.shape, NEG_BIG, F32)
    l_scr[...] = jnp.zeros(l_scr.shape, F32)
    acc_scr[...] = jnp.zeros(acc_scr.shape, F32)

    def attend_block(j, carry):
        r0 = pl.multiple_of(j * kb, kb)
        bias = bias_scr[pl.ds(r0, kb), :]
        for h in range(A_HEADS):
            s = _dot_nt(ka_ref[0, h, pl.ds(r0, kb), :], qa_ref[0, h]) + bias
            m_old = m_scr[h:h + 1, :]
            m_new = jnp.maximum(m_old, jnp.max(s, axis=0, keepdims=True))
            alpha = jnp.exp(m_old - m_new)
            p = jnp.exp(s - m_new)
            l_scr[h:h + 1, :] = alpha * l_scr[h:h + 1, :] + jnp.sum(p, axis=0, keepdims=True)
            acc_scr[h] = alpha * acc_scr[h] + _dot(vt_ref[0, h, j], p.astype(BF16))
            m_scr[h:h + 1, :] = m_new
        return carry

    lax.fori_loop(0, nblk, attend_block, 0)

    for h in range(A_HEADS):
        o_ref[0, h] = acc_scr[h] / l_scr[h:h + 1, :]


def _dsa(qi_h, wt, qa_h, ki_b, ka_h, vt_h, *, tq, kb, s_real, past, topk):
    bsz, _, t_len, _ = qa_h.shape
    s_pad = ki_b.shape[1]
    nkb = s_pad // kb
    idx_bits = max(1, (s_pad - 1).bit_length())
    kern = functools.partial(_dsa_kernel, tq=tq, kb=kb, s_real=s_real, past=past, topk=topk, idx_bits=idx_bits)
    return pl.pallas_call(
        kern,
        grid=(bsz, t_len // tq),
        in_specs=[
            pl.BlockSpec((1, IDX_HEADS, tq, IDX_DIM), lambda b, q: (b, 0, q, 0)),
            pl.BlockSpec((1, IDX_HEADS, tq), lambda b, q: (b, 0, q)),
            pl.BlockSpec((1, A_HEADS, tq, HEAD_DIM), lambda b, q: (b, 0, q, 0)),
            pl.BlockSpec((1, s_pad, IDX_DIM), lambda b, q: (b, 0, 0)),
            pl.BlockSpec((1, A_HEADS, s_pad, HEAD_DIM), lambda b, q: (b, 0, 0, 0)),
            pl.BlockSpec((1, A_HEADS, nkb, HEAD_DIM, kb), lambda b, q: (b, 0, 0, 0, 0)),
        ],
        out_specs=pl.BlockSpec((1, A_HEADS, HEAD_DIM, tq), lambda b, q: (b, 0, 0, q)),
        out_shape=jax.ShapeDtypeStruct((bsz, A_HEADS, HEAD_DIM, t_len), F32),
        scratch_shapes=[
            pltpu.VMEM((s_pad, tq), I32),
            pltpu.VMEM((s_pad, tq), F32),
            pltpu.VMEM((A_HEADS, tq), F32),
            pltpu.VMEM((A_HEADS, tq), F32),
            pltpu.VMEM((A_HEADS, HEAD_DIM, tq), F32),
        ],
        compiler_params=pltpu.CompilerParams(dimension_semantics=("arbitrary", "arbitrary"),
                                             vmem_limit_bytes=VMEM_LIMIT),
        name="dsa_attention",
    )(qi_h, wt, qa_h, ki_b, ka_h, vt_h)


def _stick_kernel(q_ref, k_ref, v_ref, tri_ref, o_ref, carry_scr, acc_scr, *, tq, kb, past):
    qt = pl.program_id(1)
    q0 = past + qt * tq
    nblk = (q0 + tq - 1 + kb - 1) // kb
    qpos = q0 + lax.broadcasted_iota(I32, (tq, kb), 0)
    carry_scr[...] = jnp.zeros(carry_scr.shape, F32)
    acc_scr[...] = jnp.zeros(acc_scr.shape, F32)
    tri = tri_ref[...]

    def block(jj, c):
        j = nblk - 1 - jj
        r0 = pl.multiple_of(j * kb, kb)
        causal = (r0 + lax.broadcasted_iota(I32, (tq, kb), 1)) < qpos
        for h in range(B_HEADS):
            z = _dot_nt(q_ref[0, h], k_ref[0, h, pl.ds(r0, kb), :])
            sp = jnp.where(causal, jnp.maximum(z, 0.0) + jnp.log1p(jnp.exp(-jnp.abs(z))), 0.0)
            hi, lo = _split_bf16(sp)
            carry = carry_scr[h]
            suffix = _dot(hi, tri) + _dot(lo, tri) + jnp.tile(carry, (1, kb // LANES))
            a = jnp.where(causal, jnp.exp(z - suffix), 0.0)
            acc_scr[h] = acc_scr[h] + _dot(a.astype(BF16), v_ref[0, h, pl.ds(r0, kb), :])
            carry_scr[h] = carry + jnp.sum(sp, axis=-1, keepdims=True)
        return c

    lax.fori_loop(0, nblk, block, 0)
    for h in range(B_HEADS):
        o_ref[0, h] = acc_scr[h]


def _stick(q_h, k_h, v_h, tri, *, tq, kb, past):
    bsz, _, t_len, _ = q_h.shape
    s_pad = k_h.shape[2]
    kern = functools.partial(_stick_kernel, tq=tq, kb=kb, past=past)
    return pl.pallas_call(
        kern,
        grid=(bsz, t_len // tq),
        in_specs=[
            pl.BlockSpec((1, B_HEADS, tq, HEAD_DIM), lambda b, q: (b, 0, q, 0)),
            pl.BlockSpec((1, B_HEADS, s_pad, HEAD_DIM), lambda b, q: (b, 0, 0, 0)),
            pl.BlockSpec((1, B_HEADS, s_pad, HEAD_DIM), lambda b, q: (b, 0, 0, 0)),
            pl.BlockSpec((kb, kb), lambda b, q: (0, 0)),
        ],
        out_specs=pl.BlockSpec((1, B_HEADS, tq, HEAD_DIM), lambda b, q: (b, 0, q, 0)),
        out_shape=jax.ShapeDtypeStruct((bsz, B_HEADS, t_len, HEAD_DIM), F32),
        scratch_shapes=[
            pltpu.VMEM((B_HEADS, tq, LANES), F32),
            pltpu.VMEM((B_HEADS, tq, HEAD_DIM), F32),
        ],
        compiler_params=pltpu.CompilerParams(dimension_semantics=("arbitrary", "arbitrary"),
                                             vmem_limit_bytes=VMEM_LIMIT),
        name="stick_attention",
    )(q_h, k_h, v_h, tri)


def _merge_kernel(x_ref, oa_ref, ob_ref, gates_ref, mod_ref, wua_ref, wub_ref, wout_ref, n2g_ref, wr_ref, br_ref,
                  y_ref, h2_ref, comb_ref, *, nb):
    g1 = mod_ref[:, 2, :]
    sh2 = mod_ref[:, 3, :]
    sc2 = mod_ref[:, 4, :]
    gates = gates_ref[...]
    merged = gates[:, :D_MODEL] * _dot(oa_ref[...], wua_ref[...]) + gates[:, D_MODEL:] * _dot(ob_ref[...], wub_ref[...])
    y = x_ref[...] + _rows_times(_dot(merged.astype(BF16), wout_ref[...]), g1, nb)
    y_ref[...] = y
    h2 = _rows_plus(_rows_times(_rms_rows(y) * n2g_ref[...], 1.0 + sc2, nb), sh2, nb)
    h2_ref[...] = h2.astype(BF16)

    logits = _dot3(h2, wr_ref[...]) + br_ref[...]
    lane = lax.broadcasted_iota(I32, logits.shape, 1).astype(F32)
    far = float(ROUTER_LANES)
    is_g = lane < float(N_GROUPS)
    gl = jnp.where(is_g, logits, -jnp.inf)
    gmax = jnp.max(gl, axis=-1, keepdims=True)
    gsel = jnp.min(jnp.where(is_g & (gl == gmax), lane, far), axis=-1, keepdims=True)
    p_group = 1.0 / jnp.sum(jnp.where(is_g, jnp.exp(gl - gmax), 0.0), axis=-1, keepdims=True)
    e_lo = float(EXPERT_LANE0) + gsel * float(EXPERTS_PER_GROUP)
    in_grp = (lane >= e_lo) & (lane < e_lo + float(EXPERTS_PER_GROUP))
    el = jnp.where(in_grp, logits, -jnp.inf)
    v1 = jnp.max(el, axis=-1, keepdims=True)
    i1 = jnp.min(jnp.where(in_grp & (el == v1), lane, far), axis=-1, keepdims=True)
    el2 = jnp.where(lane == i1, -jnp.inf, el)
    v2 = jnp.max(el2, axis=-1, keepdims=True)
    i2 = jnp.min(jnp.where(in_grp & (lane != i1) & (el2 == v2), lane, far), axis=-1, keepdims=True)
    e2 = jnp.exp(v2 - v1)
    w1 = p_group / (1.0 + e2)
    w2 = p_group * e2 / (1.0 + e2)
    comb_ref[...] = jnp.where(lane == i1, w1, 0.0) + jnp.where(lane == i2, w2, 0.0)


def _merge(x2, oa, ob, gates, mod, wua, wub, wout, n2g, wr, br, *, t_len, tm):
    n = x2.shape[0]
    nb = max(1, tm // t_len)
    tiles_per_batch = max(1, t_len // tm)
    row = lambda i: (i, 0)
    const2 = lambda i: (0, 0)
    return pl.pallas_call(
        functools.partial(_merge_kernel, nb=nb),
        grid=(n // tm,),
        in_specs=[
            pl.BlockSpec((tm, D_MODEL), row),
            pl.BlockSpec((tm, A_WIDTH), row),
            pl.BlockSpec((tm, B_WIDTH), row),
            pl.BlockSpec((tm, 2 * D_MODEL), row),
            pl.BlockSpec((nb, 6, D_MODEL), lambda i: (i // tiles_per_batch, 0, 0)),
            pl.BlockSpec(wua.shape, const2),
            pl.BlockSpec(wub.shape, const2),
            pl.BlockSpec(wout.shape, const2),
            pl.BlockSpec((1, D_MODEL), const2),
            pl.BlockSpec(wr.shape, const2),
            pl.BlockSpec((1, ROUTER_LANES), const2),
        ],
        out_specs=[pl.BlockSpec((tm, D_MODEL), row), pl.BlockSpec((tm, D_MODEL), row),
                   pl.BlockSpec((tm, ROUTER_LANES), row)],
        out_shape=[jax.ShapeDtypeStruct((n, D_MODEL), F32), jax.ShapeDtypeStruct((n, D_MODEL), BF16),
                   jax.ShapeDtypeStruct((n, ROUTER_LANES), F32)],
        compiler_params=pltpu.CompilerParams(dimension_semantics=("arbitrary",), vmem_limit_bytes=VMEM_LIMIT),
        name="merge_router",
    )(x2, oa, ob, gates, mod, wua, wub, wout, n2g, wr, br)


def _moe_kernel(y_ref, h2_ref, comb_ref, mod_ref, wg_ref, wu_ref, wd_ref, o_ref, acc_scr, *, nb):
    e = pl.program_id(1)

    @pl.when(e == 0)
    def _():
        acc_scr[...] = jnp.zeros(acc_scr.shape, F32)

    h2 = h2_ref[...]
    act = _silu(_dot(h2, wg_ref[0].astype(BF16))) * _dot(h2, wu_ref[0].astype(BF16))
    contrib = _dot(act.astype(BF16), wd_ref[0].astype(BF16))
    comb = comb_ref[...]
    lane = lax.broadcasted_iota(I32, comb.shape, 1)
    w_e = jnp.sum(jnp.where(lane == e + EXPERT_LANE0, comb, 0.0), axis=-1, keepdims=True)
    acc_scr[...] += w_e * contrib

    @pl.when(e == pl.num_programs(1) - 1)
    def _():
        o_ref[...] = y_ref[...] + _rows_times(acc_scr[...], mod_ref[:, 5, :], nb)


def _moe(y, h2, comb, mod, w_gate, w_up, w_down, *, t_len, tm):
    n = y.shape[0]
    nb = max(1, tm // t_len)
    tiles_per_batch = max(1, t_len // tm)
    row = lambda i, e: (i, 0)
    return pl.pallas_call(
        functools.partial(_moe_kernel, nb=nb),
        grid=(n // tm, N_EXPERTS),
        in_specs=[
            pl.BlockSpec((tm, D_MODEL), row),
            pl.BlockSpec((tm, D_MODEL), row),
            pl.BlockSpec((tm, ROUTER_LANES), row),
            pl.BlockSpec((nb, 6, D_MODEL), lambda i, e: (i // tiles_per_batch, 0, 0)),
            pl.BlockSpec((1, D_MODEL, D_EXPERT), lambda i, e: (e, 0, 0)),
            pl.BlockSpec((1, D_MODEL, D_EXPERT), lambda i, e: (e, 0, 0)),
            pl.BlockSpec((1, D_EXPERT, D_MODEL), lambda i, e: (e, 0, 0)),
        ],
        out_specs=pl.BlockSpec((tm, D_MODEL), row),
        out_shape=jax.ShapeDtypeStruct((n, D_MODEL), F32),
        scratch_shapes=[pltpu.VMEM((tm, D_MODEL), F32)],
        compiler_params=pltpu.CompilerParams(dimension_semantics=("arbitrary", "arbitrary"),
                                             vmem_limit_bytes=VMEM_LIMIT),
        name="moe_dense",
    )(y, h2, comb, mod, w_gate, w_up, w_down)


def _rope_tables(pos, lanes_on):
    half = ROT_DIM // 2
    freqs = ROPE_THETA ** (-jnp.arange(0, ROT_DIM, 2, dtype=F32) / ROT_DIM)
    ang = pos.astype(F32)[:, None] * freqs[None, :]
    cos, sin = jnp.cos(ang), jnp.sin(ang)
    d = jnp.arange(LANES)
    dh = d % HEAD_DIM
    on = d < lanes_on
    fi = dh % half
    cos_t = jnp.where((dh < ROT_DIM) & on, cos[:, fi], 1.0)
    sin_lo = jnp.where((dh < half) & on, -sin[:, fi], 0.0)
    sin_hi = jnp.where((dh >= half) & (dh < ROT_DIM) & on, sin[:, fi], 0.0)
    return jnp.stack([cos_t, sin_lo, sin_hi]).astype(F32)


def _heads(x2, bsz, t_len, n_heads):
    return x2.reshape(bsz, t_len, n_heads, HEAD_DIM).transpose(0, 2, 1, 3)


def _pad_keys(x, s_pad):
    pad = s_pad - x.shape[1]
    if pad == 0:
        return x
    return jnp.pad(x, [(0, 0), (0, pad)] + [(0, 0)] * (x.ndim - 2))


def _round_up(a, b):
    return (a + b - 1) // b * b


def _layer(x, mod, past_ak, past_av, past_aki, past_bk, past_bv, w, *, cfg):
    bsz, t_len, _ = x.shape
    past = past_ak.shape[1]
    s_real = past + t_len
    n = bsz * t_len
    x2 = x.reshape(n, D_MODEL)
    pos = past + jnp.arange(t_len, dtype=jnp.int32)

    tm = cfg["tm"]
    reps = max(1, tm // t_len)
    tab = jnp.tile(_rope_tables(pos, LANES), (1, reps, 1))
    tabi = jnp.tile(_rope_tables(pos, IDX_DIM), (1, reps, 1))
    pr = _proj(x2, mod, w["n1g"], w["wa"], w["wv"], w["wi"], w["wg"], w["bd"], w["gqk"], tab, tabi,
               t_len=t_len, tm=tm)

    new_ak = pr["ka"].reshape(bsz, t_len, A_HEADS, HEAD_DIM)
    new_av = pr["va"].reshape(bsz, t_len, A_HEADS, HEAD_DIM)
    new_aki = pr["kiw"][:, :IDX_DIM].reshape(bsz, t_len, IDX_DIM)
    new_bk = pr["kb"].reshape(bsz, t_len, B_HEADS, HEAD_DIM)
    new_bv = pr["vb"].reshape(bsz, t_len, B_HEADS, HEAD_DIM)

    def all_keys(past_x, new_bf16, width):
        new3 = new_bf16.reshape(bsz, t_len, width)
        if past == 0:
            return new3
        return jnp.concatenate([past_x.reshape(bsz, past, width).astype(BF16), new3], axis=1)

    tq, kb = cfg["dsa_tq"], cfg["dsa_kb"]
    s_pad = _round_up(s_real, kb)
    topk = min(TOPK_MAX, s_real // 4)
    ka_all = _pad_keys(all_keys(past_ak, pr["kab"], A_WIDTH), s_pad)
    va_all = _pad_keys(all_keys(past_av, pr["vab"], A_WIDTH), s_pad)
    ki_all = _pad_keys(all_keys(past_aki, pr["kib"], IDX_DIM), s_pad)
    ka_h = ka_all.reshape(bsz, s_pad, A_HEADS, HEAD_DIM).transpose(0, 2, 1, 3)
    vt_h = va_all.reshape(bsz, s_pad // kb, kb, A_HEADS, HEAD_DIM).transpose(0, 3, 1, 4, 2)
    qa_h = _heads(pr["qa"], bsz, t_len, A_HEADS)
    qi_h = _heads(pr["qi"], bsz, t_len, IDX_HEADS)
    wt = pr["kiw"][:, IDX_DIM:IDX_DIM + IDX_HEADS].reshape(bsz, t_len, IDX_HEADS).transpose(0, 2, 1)
    oa_t = _dsa(qi_h, wt, qa_h, ki_all, ka_h, vt_h, tq=tq, kb=kb, s_real=s_real, past=past, topk=topk)
    oa = oa_t.transpose(0, 3, 1, 2).reshape(n, A_WIDTH).astype(BF16)

    tqb, kbb = cfg["sb_tq"], cfg["sb_kb"]
    s_pad_b = _round_up(s_real, kbb)
    kb_h = _pad_keys(all_keys(past_bk, pr["kbb"], B_WIDTH), s_pad_b).reshape(
        bsz, s_pad_b, B_HEADS, HEAD_DIM).transpose(0, 2, 1, 3)
    vb_h = _pad_keys(all_keys(past_bv, pr["vbb"], B_WIDTH), s_pad_b).reshape(
        bsz, s_pad_b, B_HEADS, HEAD_DIM).transpose(0, 2, 1, 3)
    qb_h = _heads(pr["qb"], bsz, t_len, B_HEADS)
    idx = jnp.arange(kbb)
    tri = (idx[:, None] >= idx[None, :]).astype(BF16)
    ob_h = _stick(qb_h, kb_h, vb_h, tri, tq=tqb, kb=kbb, past=past)
    ob = ob_h.transpose(0, 2, 1, 3).reshape(n, B_WIDTH).astype(BF16)

    y, h2, comb = _merge(x2, oa, ob, pr["gates"], mod, w["wua"], w["wub"], w["wout"], w["n2g"], w["wr"], w["br"],
                         t_len=t_len, tm=cfg["tm_merge"])
    out = _moe(y, h2, comb, mod, w["w_gate"], w["w_up"], w["w_down"], t_len=t_len, tm=cfg["tm_moe"])
    return out.reshape(bsz, t_len, D_MODEL), (new_ak, new_av, new_aki, new_bk, new_bv)


def _prep_weights(w_in, norm1_g, qnorm_g, knorm_g, w_up_a, w_up_b, w_out, norm2_g, w_rg, b_rg, w_re, b_re,
                  w_e_gate, w_e_up, w_e_down):
    o = 0
    seg = {}
    for name, width in (("qa", A_WIDTH), ("ka", A_WIDTH), ("va", A_WIDTH), ("qi", IDX_HEADS * IDX_DIM),
                        ("ki", IDX_DIM), ("wi", IDX_HEADS), ("qb", B_WIDTH), ("kb", B_WIDTH), ("vb", B_WIDTH),
                        ("gates", 2 * D_MODEL)):
        seg[name] = w_in[:, o:o + width]
        o += width
    pad_i = jnp.zeros((D_MODEL, IDX_SEG - IDX_HEADS * IDX_DIM - IDX_DIM - IDX_HEADS), w_in.dtype)
    hd = jnp.arange(2 * A_WIDTH) // HEAD_DIM
    n_r = N_GROUPS + N_EXPERTS
    return dict(
        n1g=norm1_g.reshape(1, D_MODEL),
        wa=jnp.concatenate([seg["qa"], seg["ka"]], axis=1).astype(BF16),
        wv=jnp.concatenate([seg["va"], seg["qb"], seg["kb"], seg["vb"]], axis=1).astype(BF16),
        wi=jnp.concatenate([seg["qi"], seg["ki"], seg["wi"], pad_i], axis=1).astype(BF16),
        wg=seg["gates"].astype(BF16),
        bd=((hd[:, None] == hd[None, :]).astype(F32) / HEAD_DIM).astype(BF16),
        gqk=jnp.concatenate([jnp.tile(qnorm_g, A_HEADS), jnp.tile(knorm_g, A_HEADS)]).reshape(1, 2 * A_WIDTH),
        wua=w_up_a.astype(BF16), wub=w_up_b.astype(BF16), wout=w_out.astype(BF16),
        n2g=norm2_g.reshape(1, D_MODEL),
        wr=jnp.pad(jnp.concatenate([w_rg, w_re], axis=1), ((0, 0), (0, ROUTER_LANES - n_r))),
        br=jnp.pad(jnp.concatenate([b_rg, b_re]), (0, ROUTER_LANES - n_r)).reshape(1, ROUTER_LANES),
        w_gate=w_e_gate, w_up=w_e_up, w_down=w_e_down,
    )


def _group_cfg(t_len):
    if t_len >= 512:
        return dict(tm=256, tm_merge=256, tm_moe=1024, dsa_tq=128, dsa_kb=512, sb_tq=256, sb_kb=256)
    return dict(tm=512, tm_merge=512, tm_moe=512, dsa_tq=t_len, dsa_kb=384, sb_tq=t_len, sb_kb=256)


def kernel(x_prompt, x_sample, cache_a_k, cache_a_v, cache_a_kidx, cache_b_k, cache_b_v, c_prompt, c_sample,
           w_ada, b_ada, norm1_g, w_in, qnorm_g, knorm_g, w_up_a, w_up_b, w_out, norm2_g,
           w_rg, b_rg, w_re, b_re, w_e_gate, w_e_up, w_e_down):
    depth = w_ada.shape[0]
    bp, bs = x_prompt.shape[0], x_sample.shape[0]
    dt = x_prompt.dtype
    rows = _round_up(bp + bs, 8)
    empty_a = jnp.zeros((bp, 0, A_HEADS, HEAD_DIM), dt)
    empty_i = jnp.zeros((bp, 0, IDX_DIM), dt)
    empty_b = jnp.zeros((bp, 0, B_HEADS, HEAD_DIM), dt)
    y_p, y_s = x_prompt, x_sample
    rows_p, rows_s = [], []
    c_all = jnp.concatenate([c_prompt, c_sample, jnp.zeros((rows - bp - bs, D_MODEL), dt)], axis=0)
    for l in range(depth):
        w = _prep_weights(w_in[l], norm1_g[l], qnorm_g[l], knorm_g[l], w_up_a[l], w_up_b[l], w_out[l], norm2_g[l],
                          w_rg[l], b_rg[l], w_re[l], b_re[l], w_e_gate[l], w_e_up[l], w_e_down[l])
        mod = _ada(c_all, w_ada[l], b_ada[l].reshape(1, -1))
        mod_p = mod[:bp].reshape(bp, 6, D_MODEL)
        mod_s = mod[bp:bp + bs].reshape(bs, 6, D_MODEL)
        y_p, new_p = _layer(y_p, mod_p, empty_a, empty_a, empty_i, empty_b, empty_b, w,
                            cfg=_group_cfg(y_p.shape[1]))
        y_s, new_s = _layer(y_s, mod_s, cache_a_k[l], cache_a_v[l], cache_a_kidx[l], cache_b_k[l], cache_b_v[l], w,
                            cfg=_group_cfg(y_s.shape[1]))
        rows_p.append(new_p)
        rows_s.append(new_s)
    stack = lambda rws, i: jnp.stack([r[i] for r in rws])
    return (y_p, y_s) + tuple(stack(rows_p, i) for i in range(5)) + tuple(stack(rows_s, i) for i in range(5))
```

```python
import functools

import jax
import jax.numpy as jnp
from jax import lax
from jax.experimental import pallas as pl
from jax.experimental.pallas import tpu as pltpu

D_MODEL = 1024
CHUNK = 64
HEAD_DIM = 64
A_HEADS = 8
B_HEADS = 8
IDX_HEADS = 4
IDX_DIM = 64
TOPK_MAX = 256
ROPE_THETA = 500000.0
ROT_DIM = HEAD_DIM // 4
N_GROUPS = 4
EXPERTS_PER_GROUP = 8
N_EXPERTS = N_GROUPS * EXPERTS_PER_GROUP
D_EXPERT = 256
RMS_EPS = 1e-6
A_WIDTH = A_HEADS * HEAD_DIM
B_WIDTH = B_HEADS * HEAD_DIM

LANES = 128
IDX_SEG = 512
PAIRS = 4
COUNT_ROWS = 64
ROUTER_LANES = 128
EXPERT_LANE0 = N_GROUPS
VMEM_LIMIT = 56 * 1024 * 1024

F32 = jnp.float32
BF16 = jnp.bfloat16
I32 = jnp.int32
NEG_BIG = -1e30
INT_MIN = -(2 ** 31)
KEY_NEG_INF = (0xFF800000 - (1 << 32)) ^ 0x7FFFFFFF
IDX_BIG = 1 << 30


def _dot(a, b):
    return jnp.dot(a, b, preferred_element_type=F32)


def _dot_nt(a, b):
    return lax.dot_general(a, b, (((1,), (1,)), ((), ())), preferred_element_type=F32)


def _dot_tn(a, b):
    return lax.dot_general(a, b, (((0,), (0,)), ((), ())), preferred_element_type=F32)


def _split_bf16(a):
    hi = a.astype(BF16)
    lo = (a - hi.astype(F32)).astype(BF16)
    return hi, lo


def _dot3(a, b):
    ah, al = _split_bf16(a)
    bh, bl = _split_bf16(b)
    return _dot(ah, bh) + (_dot(ah, bl) + _dot(al, bh))


def _silu(x):
    return x * (1.0 / (1.0 + jnp.exp(-x)))


def _sigmoid(x):
    return 1.0 / (1.0 + jnp.exp(-x))


def _rows_times(x, vec, nb):
    if nb == 1:
        return x * vec
    tm, d = x.shape
    return (x.reshape(nb, tm // nb, d) * vec[:, None, :]).reshape(tm, d)


def _rows_plus(x, vec, nb):
    if nb == 1:
        return x + vec
    tm, d = x.shape
    return (x.reshape(nb, tm // nb, d) + vec[:, None, :]).reshape(tm, d)


def _rms_rows(x):
    return x * lax.rsqrt(jnp.mean(x * x, axis=-1, keepdims=True) + RMS_EPS)


def _ada_kernel(c_ref, w_ref, b_ref, o_ref):
    o_ref[...] = _dot3(_silu(c_ref[...]), w_ref[...]) + b_ref[...]


def _ada(c_all, w_ada, b_ada):
    rows = c_all.shape[0]
    n = w_ada.shape[1]
    tn = 1536
    return pl.pallas_call(
        _ada_kernel,
        grid=(n // tn,),
        in_specs=[pl.BlockSpec((rows, D_MODEL), lambda j: (0, 0)),
                  pl.BlockSpec((D_MODEL, tn), lambda j: (0, j)),
                  pl.BlockSpec((1, tn), lambda j: (0, j))],
        out_specs=pl.BlockSpec((rows, tn), lambda j: (0, j)),
        out_shape=jax.ShapeDtypeStruct((rows, n), F32),
        compiler_params=pltpu.CompilerParams(dimension_semantics=("arbitrary",), vmem_limit_bytes=VMEM_LIMIT),
        name="ada_mod",
    )(c_all, w_ada, b_ada)


def _rope_lanes(a, tab_ref, reps):
    width = a.shape[-1]
    cos = jnp.tile(tab_ref[0], (1, reps)) if reps > 1 else tab_ref[0]
    sin_lo = jnp.tile(tab_ref[1], (1, reps)) if reps > 1 else tab_ref[1]
    sin_hi = jnp.tile(tab_ref[2], (1, reps)) if reps > 1 else tab_ref[2]
    half = ROT_DIM // 2
    return a * cos + pltpu.roll(a, width - half, 1) * sin_lo + pltpu.roll(a, half, 1) * sin_hi


def _proj_kernel(x_ref, mod_ref, n1g_ref, wa_ref, wv_ref, wi_ref, wg_ref, bd_ref, gqk_ref, tab_ref,
                 qa_ref, ka_ref, kab_ref, va_ref, vab_ref, qb_ref, kb_ref, kbb_ref, vb_ref, vbb_ref,
                 qi_ref, ki2_ref, ki2b_ref, wi_out_ref, gates_ref, *, nb):
    x = x_ref[...]
    sh1 = mod_ref[:, 0, :]
    sc1 = mod_ref[:, 1, :]
    h = _rows_plus(_rows_times(_rms_rows(x) * n1g_ref[...], 1.0 + sc1, nb), sh1, nb)
    hb = h.astype(BF16)

    a = _dot(hb, wa_ref[...])
    ms = _dot((a * a).astype(BF16), bd_ref[...])
    a = a * lax.rsqrt(ms + RMS_EPS) * gqk_ref[...]
    a = _rope_lanes(a, tab_ref, a.shape[-1] // LANES)
    qa_ref[...] = (a[:, :A_WIDTH] * (HEAD_DIM ** -0.5)).astype(BF16)
    ka = a[:, A_WIDTH:]
    ka_ref[...] = ka
    kab_ref[...] = ka.astype(BF16)

    v = _dot(hb, wv_ref[...])
    va = v[:, :A_WIDTH]
    va_ref[...] = va
    vab_ref[...] = va.astype(BF16)
    qb_ref[...] = (v[:, A_WIDTH:A_WIDTH + B_WIDTH] * (HEAD_DIM ** -0.5)).astype(BF16)
    kb = v[:, A_WIDTH + B_WIDTH:A_WIDTH + 2 * B_WIDTH]
    kb_ref[...] = kb
    kbb_ref[...] = kb.astype(BF16)
    vb = v[:, A_WIDTH + 2 * B_WIDTH:]
    vb_ref[...] = vb
    vbb_ref[...] = vb.astype(BF16)

    i = _dot(hb, wi_ref[...])
    nr = IDX_SEG - LANES
    ir = _rope_lanes(i[:, :nr], tab_ref, nr // LANES)
    nq = IDX_HEADS * IDX_DIM
    qi_ref[...] = (ir[:, :nq] * (IDX_DIM ** -0.5)).astype(BF16)
    ki2 = ir[:, nq:]
    ki2_ref[...] = ki2
    ki2b_ref[...] = ki2.astype(BF16)
    wi_out_ref[...] = i[:, nr:]

    gates_ref[...] = _sigmoid(_dot(hb, wg_ref[...]))


def _proj(x2, mod, n1g, wa, wv, wi, wg, bd, gqk, tab, *, t_len, tm):
    n = x2.shape[0]
    nb = max(1, tm // t_len)
    tiles_per_batch = max(1, t_len // tm)
    tab_tiles = tab.shape[1] // tm
    row = lambda i: (i, 0)
    const2 = lambda i: (0, 0)
    tabmap = lambda i: (0, i % tab_tiles, 0)
    in_specs = [
        pl.BlockSpec((tm, D_MODEL), row),
        pl.BlockSpec((nb, 6, D_MODEL), lambda i: (i // tiles_per_batch, 0, 0)),
        pl.BlockSpec((1, D_MODEL), const2),
        pl.BlockSpec(wa.shape, const2),
        pl.BlockSpec(wv.shape, const2),
        pl.BlockSpec(wi.shape, const2),
        pl.BlockSpec(wg.shape, const2),
        pl.BlockSpec(bd.shape, const2),
        pl.BlockSpec((1, 2 * A_WIDTH), const2),
        pl.BlockSpec((3, tm, LANES), tabmap),
    ]
    outs = [
        ("qa", A_WIDTH, BF16), ("ka", A_WIDTH, F32), ("kab", A_WIDTH, BF16),
        ("va", A_WIDTH, F32), ("vab", A_WIDTH, BF16),
        ("qb", B_WIDTH, BF16), ("kb", B_WIDTH, F32), ("kbb", B_WIDTH, BF16),
        ("vb", B_WIDTH, F32), ("vbb", B_WIDTH, BF16),
        ("qi", IDX_HEADS * IDX_DIM, BF16), ("ki2", LANES, F32), ("ki2b", LANES, BF16), ("wi", LANES, F32),
        ("gates", 2 * D_MODEL, F32),
    ]
    res = pl.pallas_call(
        functools.partial(_proj_kernel, nb=nb),
        grid=(n // tm,),
        in_specs=in_specs,
        out_specs=[pl.BlockSpec((tm, w), row) for _, w, _ in outs],
        out_shape=[jax.ShapeDtypeStruct((n, w), dt) for _, w, dt in outs],
        compiler_params=pltpu.CompilerParams(dimension_semantics=("arbitrary",), vmem_limit_bytes=VMEM_LIMIT),
        name="in_proj",
    )(x2, mod, n1g, wa, wv, wi, wg, bd, gqk, tab)
    return dict(zip([o[0] for o in outs], res))


def _head_halves(x):
    lane = lax.broadcasted_iota(I32, x.shape, 1)
    zero = jnp.zeros_like(x)
    return jnp.where(lane < HEAD_DIM, x, zero), jnp.where(lane >= HEAD_DIM, x, zero)


def _dsa_kernel(qi_ref, w_ref, qa_ref, ki_ref, ka_ref, va_ref, o_ref,
                key_scr, bias_scr, m_scr, l_scr, acc_scr, *, tq, kb, s_real, past, topk, idx_bits):
    qt = pl.program_id(1)
    q0 = past + qt * tq
    kmax = jnp.minimum(((q0 + tq - 1) // CHUNK + 1) * CHUNK, s_real)
    nblk = (kmax + kb - 1) // kb
    qchunk = (q0 + lax.broadcasted_iota(I32, (1, tq), 1)) // CHUNK
    kf = float(topk)
    chunks = lambda ref, i: ref[0, :, i * LANES:(i + 1) * LANES]
    qi_heads = [q for i in range(IDX_HEADS // 2) for q in _head_halves(chunks(qi_ref, i))]
    qa_heads = [q for i in range(PAIRS) for q in _head_halves(chunks(qa_ref, i))]
    w_rows = w_ref[0].T

    def score_block(j, carry):
        r0 = pl.multiple_of(j * kb, kb)
        kib = ki_ref[0, pl.ds(r0, kb), :]
        acc = jnp.zeros((kb, tq), F32)
        for h in range(IDX_HEADS):
            rel = jnp.maximum(_dot_nt(kib, qi_heads[h]), 0.0)
            acc = acc + w_rows[h:h + 1, :] * rel
        kpos = r0 + lax.broadcasted_iota(I32, (kb, tq), 0)
        adm = ((kpos // CHUNK) <= qchunk) & (kpos < s_real)
        s = jnp.where(adm, acc + 0.0, -jnp.inf)
        bits = pltpu.bitcast(s, I32)
        key_scr[pl.ds(r0, kb), :] = jnp.where(bits < 0, bits ^ 0x7FFFFFFF, bits)
        return carry

    lax.fori_loop(0, nblk, score_block, 0)

    def count(pred):
        def body(j, acc):
            r0 = pl.multiple_of(j * kb, kb)
            hit = jnp.where(pred(key_scr[pl.ds(r0, kb), :], r0), 1.0, 0.0)
            return acc + jnp.sum(hit.reshape(kb // COUNT_ROWS, COUNT_ROWS, tq), axis=0)
        acc = lax.fori_loop(0, nblk, body, jnp.zeros((COUNT_ROWS, tq), F32))
        return jnp.sum(acc, axis=0, keepdims=True)

    c0 = count(lambda blk, r0: blk >= 0)
    t0 = jnp.where(c0 >= kf, 0, INT_MIN).astype(I32)

    def value_bit(i, t):
        cand = t | jnp.left_shift(jnp.int32(1), 30 - i)
        return jnp.where(count(lambda blk, r0: blk >= cand) >= kf, cand, t)

    t = lax.fori_loop(0, 31, value_bit, t0)

    def rank_block(j, carry):
        r0 = pl.multiple_of(j * kb, kb)
        blk = key_scr[pl.ds(r0, kb), :]
        kpos = r0 + lax.broadcasted_iota(I32, (kb, tq), 0)
        rk = jnp.where(blk > t, -1, jnp.where(blk == t, kpos, IDX_BIG))
        key_scr[pl.ds(r0, kb), :] = jnp.where(blk == KEY_NEG_INF, IDX_BIG, rk)
        return carry

    lax.fori_loop(0, nblk, rank_block, 0)

    def index_bit(i, jstar):
        cand = jstar | jnp.left_shift(jnp.int32(1), idx_bits - 1 - i)
        return jnp.where(count(lambda blk, r0: blk < cand) < kf, cand, jstar)

    jstar = lax.fori_loop(0, idx_bits, index_bit, jnp.zeros((1, tq), I32))

    def bias_block(j, carry):
        r0 = pl.multiple_of(j * kb, kb)
        bias_scr[pl.ds(r0, kb), :] = jnp.where(key_scr[pl.ds(r0, kb), :] <= jstar, 0.0, NEG_BIG)
        return carry

    lax.fori_loop(0, nblk, bias_block, 0)

    m_scr[...] = jnp.full(m_scr.shape, NEG_BIG, F32)
    l_scr[...] = jnp.zeros(l_scr.shape, F32)
    acc_scr[...] = jnp.zeros(acc_scr.shape, F32)

    def fold_rows(x, op):
        part = op(x.reshape(kb // COUNT_ROWS, COUNT_ROWS, tq), axis=0)
        return op(part, axis=0, keepdims=True)

    def attend_block(j, carry):
        r0 = pl.multiple_of(j * kb, kb)
        bias = bias_scr[pl.ds(r0, kb), :]
        heads = range(A_HEADS)
        m_old = m_scr[...]
        l_old = l_scr[...]
        k_pair = [ka_ref[0, pl.ds(r0, kb), i * LANES:(i + 1) * LANES] for i in range(PAIRS)]
        v_pair = [va_ref[0, pl.ds(r0, kb), i * LANES:(i + 1) * LANES] for i in range(PAIRS)]
        ss = [_dot_nt(k_pair[h // 2], qa_heads[h]) + bias for h in heads]
        m_new = [jnp.maximum(m_old[h:h + 1, :], fold_rows(ss[h], jnp.max)) for h in heads]
        ps = [jnp.exp(ss[h] - m_new[h]) for h in heads]
        pvs = [_dot_tn(v_pair[h // 2], ps[h].astype(BF16)) for h in heads]
        alphas = [jnp.exp(m_old[h:h + 1, :] - m_new[h]) for h in heads]
        for i in range(PAIRS):
            e, o = 2 * i, 2 * i + 1
            acc_scr[i] = (jnp.where(even_rows, alphas[e], alphas[o]) * acc_scr[i]
                          + jnp.where(even_rows, pvs[e], pvs[o]))
        for h in heads:
            l_scr[h:h + 1, :] = alphas[h] * l_old[h:h + 1, :] + fold_rows(ps[h], jnp.sum)
            m_scr[h:h + 1, :] = m_new[h]
        return carry

    even_rows = lax.broadcasted_iota(I32, (LANES, tq), 0) < HEAD_DIM
    lax.fori_loop(0, nblk, attend_block, 0)

    l_fin = l_scr[...]
    for i in range(PAIRS):
        denom = jnp.where(even_rows, l_fin[2 * i:2 * i + 1, :], l_fin[2 * i + 1:2 * i + 2, :])
        o_ref[0, :, i * LANES:(i + 1) * LANES] = (acc_scr[i] / denom).T.astype(o_ref.dtype)


def _dsa(qi, wi, qa, ki2, ka, va, *, tq, kb, s_real, past, topk):
    bsz, t_len, _ = qa.shape
    s_pad = ki2.shape[1]
    idx_bits = max(1, (s_pad - 1).bit_length())
    kern = functools.partial(_dsa_kernel, tq=tq, kb=kb, s_real=s_real, past=past, topk=topk, idx_bits=idx_bits)
    qmap = lambda b, q: (b, q, 0)
    kmap = lambda b, q: (b, 0, 0)
    return pl.pallas_call(
        kern,
        grid=(bsz, t_len // tq),
        in_specs=[
            pl.BlockSpec((1, tq, IDX_HEADS * IDX_DIM), qmap),
            pl.BlockSpec((1, tq, LANES), qmap),
            pl.BlockSpec((1, tq, A_WIDTH), qmap),
            pl.BlockSpec((1, s_pad, LANES), kmap),
            pl.BlockSpec((1, s_pad, A_WIDTH), kmap),
            pl.BlockSpec((1, s_pad, A_WIDTH), kmap),
        ],
        out_specs=pl.BlockSpec((1, tq, A_WIDTH), qmap),
        out_shape=jax.ShapeDtypeStruct((bsz, t_len, A_WIDTH), BF16),
        scratch_shapes=[
            pltpu.VMEM((s_pad, tq), I32),
            pltpu.VMEM((s_pad, tq), F32),
            pltpu.VMEM((A_HEADS, tq), F32),
            pltpu.VMEM((A_HEADS, tq), F32),
            pltpu.VMEM((PAIRS, LANES, tq), F32),
        ],
        compiler_params=pltpu.CompilerParams(dimension_semantics=("arbitrary", "arbitrary"),
                                             vmem_limit_bytes=VMEM_LIMIT),
        name="dsa_attention",
    )(qi, wi, qa, ki2, ka, va)


def _stick_kernel(q_ref, k_ref, v_ref, tri_ref, o_ref, carry_scr, acc_scr, *, tq, kb, past):
    qt = pl.program_id(1)
    q0 = past + qt * tq
    nblk = (q0 + tq - 1 + kb - 1) // kb
    n_full = q0 // kb
    qpos = q0 + lax.broadcasted_iota(I32, (tq, kb), 0)
    carry_scr[...] = jnp.zeros(carry_scr.shape, F32)
    acc_scr[...] = jnp.zeros(acc_scr.shape, F32)
    tri = tri_ref[...]
    heads = range(B_HEADS)
    q_heads = [q for i in range(PAIRS) for q in _head_halves(q_ref[0, :, i * LANES:(i + 1) * LANES])]
    even_lanes = lax.broadcasted_iota(I32, (tq, LANES), 1) < HEAD_DIM

    def block(j, masked):
        r0 = pl.multiple_of(j * kb, kb)
        causal = ((r0 + lax.broadcasted_iota(I32, (tq, kb), 1)) < qpos) if masked else None
        carries = [carry_scr[h] for h in heads]
        k_pair = [k_ref[0, pl.ds(r0, kb), i * LANES:(i + 1) * LANES] for i in range(PAIRS)]
        v_pair = [v_ref[0, pl.ds(r0, kb), i * LANES:(i + 1) * LANES] for i in range(PAIRS)]
        zs = [_dot_nt(q_heads[h], k_pair[h // 2]) for h in heads]
        sps = []
        for z in zs:
            sp = jnp.maximum(z, 0.0) + jnp.log(1.0 + jnp.exp(-jnp.abs(z)))
            sps.append(jnp.where(causal, sp, 0.0) if masked else sp)
        sufs = []
        for h in heads:
            hi, lo = _split_bf16(sps[h])
            sufs.append(_dot(hi, tri) + _dot(lo, tri) + jnp.tile(carries[h], (1, kb // LANES)))
        pvs = []
        for h in heads:
            a = jnp.exp(zs[h] - sufs[h])
            if masked:
                a = jnp.where(causal, a, 0.0)
            pvs.append(_dot(a.astype(BF16), v_pair[h // 2]))
        for i in range(PAIRS):
            acc_scr[i] = acc_scr[i] + jnp.where(even_lanes, pvs[2 * i], pvs[2 * i + 1])
        for h in heads:
            carry_scr[h] = carries[h] + jnp.sum(sps[h], axis=-1, keepdims=True)

    def masked_step(jj, c):
        block(nblk - 1 - jj, True)
        return c

    def full_step(jj, c):
        block(n_full - 1 - jj, False)
        return c

    lax.fori_loop(0, nblk - n_full, masked_step, 0)
    lax.fori_loop(0, n_full, full_step, 0)
    for i in range(PAIRS):
        o_ref[0, :, i * LANES:(i + 1) * LANES] = acc_scr[i].astype(o_ref.dtype)


def _stick(q, k, v, tri, *, tq, kb, past):
    bsz, t_len, _ = q.shape
    s_pad = k.shape[1]
    kern = functools.partial(_stick_kernel, tq=tq, kb=kb, past=past)
    kmap = lambda b, q: (b, 0, 0)
    return pl.pallas_call(
        kern,
        grid=(bsz, t_len // tq),
        in_specs=[
            pl.BlockSpec((1, tq, B_WIDTH), lambda b, q: (b, q, 0)),
            pl.BlockSpec((1, s_pad, B_WIDTH), kmap),
            pl.BlockSpec((1, s_pad, B_WIDTH), kmap),
            pl.BlockSpec((kb, kb), lambda b, q: (0, 0)),
        ],
        out_specs=pl.BlockSpec((1, tq, B_WIDTH), lambda b, q: (b, q, 0)),
        out_shape=jax.ShapeDtypeStruct((bsz, t_len, B_WIDTH), BF16),
        scratch_shapes=[
            pltpu.VMEM((B_HEADS, tq, LANES), F32),
            pltpu.VMEM((PAIRS, tq, LANES), F32),
        ],
        compiler_params=pltpu.CompilerParams(dimension_semantics=("arbitrary", "arbitrary"),
                                             vmem_limit_bytes=VMEM_LIMIT),
        name="stick_attention",
    )(q, k, v, tri)


def _merge_kernel(x_ref, oa_ref, ob_ref, gates_ref, mod_ref, wua_ref, wub_ref, wout_ref, n2g_ref, wr_ref, br_ref,
                  y_ref, h2_ref, comb_ref, *, nb):
    g1 = mod_ref[:, 2, :]
    sh2 = mod_ref[:, 3, :]
    sc2 = mod_ref[:, 4, :]
    gates = gates_ref[...]
    merged = gates[:, :D_MODEL] * _dot(oa_ref[...], wua_ref[...]) + gates[:, D_MODEL:] * _dot(ob_ref[...], wub_ref[...])
    y = x_ref[...] + _rows_times(_dot(merged.astype(BF16), wout_ref[...]), g1, nb)
    y_ref[...] = y
    h2 = _rows_plus(_rows_times(_rms_rows(y) * n2g_ref[...], 1.0 + sc2, nb), sh2, nb)
    h2_ref[...] = h2.astype(BF16)

    logits = _dot3(h2, wr_ref[...]) + br_ref[...]
    lane = lax.broadcasted_iota(I32, logits.shape, 1).astype(F32)
    far = float(ROUTER_LANES)
    is_g = lane < float(N_GROUPS)
    gl = jnp.where(is_g, logits, -jnp.inf)
    gmax = jnp.max(gl, axis=-1, keepdims=True)
    gsel = jnp.min(jnp.where(is_g & (gl == gmax), lane, far), axis=-1, keepdims=True)
    p_group = 1.0 / jnp.sum(jnp.where(is_g, jnp.exp(gl - gmax), 0.0), axis=-1, keepdims=True)
    e_lo = float(EXPERT_LANE0) + gsel * float(EXPERTS_PER_GROUP)
    in_grp = (lane >= e_lo) & (lane < e_lo + float(EXPERTS_PER_GROUP))
    el = jnp.where(in_grp, logits, -jnp.inf)
    v1 = jnp.max(el, axis=-1, keepdims=True)
    i1 = jnp.min(jnp.where(in_grp & (el == v1), lane, far), axis=-1, keepdims=True)
    el2 = jnp.where(lane == i1, -jnp.inf, el)
    v2 = jnp.max(el2, axis=-1, keepdims=True)
    i2 = jnp.min(jnp.where(in_grp & (lane != i1) & (el2 == v2), lane, far), axis=-1, keepdims=True)
    e2 = jnp.exp(v2 - v1)
    w1 = p_group / (1.0 + e2)
    w2 = p_group * e2 / (1.0 + e2)
    comb_ref[...] = jnp.where(lane == i1, w1, 0.0) + jnp.where(lane == i2, w2, 0.0)


def _merge(x2, oa, ob, gates, mod, wua, wub, wout, n2g, wr, br, *, t_len, tm):
    n = x2.shape[0]
    nb = max(1, tm // t_len)
    tiles_per_batch = max(1, t_len // tm)
    row = lambda i: (i, 0)
    const2 = lambda i: (0, 0)
    return pl.pallas_call(
        functools.partial(_merge_kernel, nb=nb),
        grid=(n // tm,),
        in_specs=[
            pl.BlockSpec((tm, D_MODEL), row),
            pl.BlockSpec((tm, A_WIDTH), row),
            pl.BlockSpec((tm, B_WIDTH), row),
            pl.BlockSpec((tm, 2 * D_MODEL), row),
            pl.BlockSpec((nb, 6, D_MODEL), lambda i: (i // tiles_per_batch, 0, 0)),
            pl.BlockSpec(wua.shape, const2),
            pl.BlockSpec(wub.shape, const2),
            pl.BlockSpec(wout.shape, const2),
            pl.BlockSpec((1, D_MODEL), const2),
            pl.BlockSpec(wr.shape, const2),
            pl.BlockSpec((1, ROUTER_LANES), const2),
        ],
        out_specs=[pl.BlockSpec((tm, D_MODEL), row), pl.BlockSpec((tm, D_MODEL), row),
                   pl.BlockSpec((tm, ROUTER_LANES), row)],
        out_shape=[jax.ShapeDtypeStruct((n, D_MODEL), F32), jax.ShapeDtypeStruct((n, D_MODEL), BF16),
                   jax.ShapeDtypeStruct((n, ROUTER_LANES), F32)],
        compiler_params=pltpu.CompilerParams(dimension_semantics=("arbitrary",), vmem_limit_bytes=VMEM_LIMIT),
        name="merge_router",
    )(x2, oa, ob, gates, mod, wua, wub, wout, n2g, wr, br)


def _moe_kernel(y_ref, h2_ref, comb_ref, mod_ref, wg_ref, wu_ref, wd_ref, o_ref, acc_scr, *, nb):
    e = pl.program_id(1)

    @pl.when(e == 0)
    def _():
        acc_scr[...] = jnp.zeros(acc_scr.shape, F32)

    h2 = h2_ref[...]
    act = _silu(_dot(h2, wg_ref[0].astype(BF16))) * _dot(h2, wu_ref[0].astype(BF16))
    contrib = _dot(act.astype(BF16), wd_ref[0].astype(BF16))
    comb = comb_ref[...]
    lane = lax.broadcasted_iota(I32, comb.shape, 1)
    w_e = jnp.sum(jnp.where(lane == e + EXPERT_LANE0, comb, 0.0), axis=-1, keepdims=True)
    acc_scr[...] += w_e * contrib

    @pl.when(e == pl.num_programs(1) - 1)
    def _():
        o_ref[...] = y_ref[...] + _rows_times(acc_scr[...], mod_ref[:, 5, :], nb)


def _moe(y, h2, comb, mod, w_gate, w_up, w_down, *, t_len, tm):
    n = y.shape[0]
    nb = max(1, tm // t_len)
    tiles_per_batch = max(1, t_len // tm)
    row = lambda i, e: (i, 0)
    return pl.pallas_call(
        functools.partial(_moe_kernel, nb=nb),
        grid=(n // tm, N_EXPERTS),
        in_specs=[
            pl.BlockSpec((tm, D_MODEL), row),
            pl.BlockSpec((tm, D_MODEL), row),
            pl.BlockSpec((tm, ROUTER_LANES), row),
            pl.BlockSpec((nb, 6, D_MODEL), lambda i, e: (i // tiles_per_batch, 0, 0)),
            pl.BlockSpec((1, D_MODEL, D_EXPERT), lambda i, e: (e, 0, 0)),
            pl.BlockSpec((1, D_MODEL, D_EXPERT), lambda i, e: (e, 0, 0)),
            pl.BlockSpec((1, D_EXPERT, D_MODEL), lambda i, e: (e, 0, 0)),
        ],
        out_specs=pl.BlockSpec((tm, D_MODEL), row),
        out_shape=jax.ShapeDtypeStruct((n, D_MODEL), F32),
        scratch_shapes=[pltpu.VMEM((tm, D_MODEL), F32)],
        compiler_params=pltpu.CompilerParams(dimension_semantics=("arbitrary", "arbitrary"),
                                             vmem_limit_bytes=VMEM_LIMIT),
        name="moe_dense",
    )(y, h2, comb, mod, w_gate, w_up, w_down)


def _rope_tables(pos):
    half = ROT_DIM // 2
    freqs = ROPE_THETA ** (-jnp.arange(0, ROT_DIM, 2, dtype=F32) / ROT_DIM)
    ang = pos.astype(F32)[:, None] * freqs[None, :]
    cos, sin = jnp.cos(ang), jnp.sin(ang)
    dh = jnp.arange(LANES) % HEAD_DIM
    fi = dh % half
    cos_t = jnp.where(dh < ROT_DIM, cos[:, fi], 1.0)
    sin_lo = jnp.where(dh < half, -sin[:, fi], 0.0)
    sin_hi = jnp.where((dh >= half) & (dh < ROT_DIM), sin[:, fi], 0.0)
    return jnp.stack([cos_t, sin_lo, sin_hi]).astype(F32)


def _round_up(a, b):
    return (a + b - 1) // b * b


def _layer(x, mod, past_ak, past_av, past_aki, past_bk, past_bv, w, *, cfg):
    bsz, t_len, _ = x.shape
    past = past_ak.shape[1]
    s_real = past + t_len
    n = bsz * t_len
    x2 = x.reshape(n, D_MODEL)
    pos = past + jnp.arange(t_len, dtype=jnp.int32)

    tm = cfg["tm"]
    tab = jnp.tile(_rope_tables(pos), (1, max(1, tm // t_len), 1))
    pr = _proj(x2, mod, w["n1g"], w["wa"], w["wv"], w["wi"], w["wg"], w["bd"], w["gqk"], tab, t_len=t_len, tm=tm)

    new_ak = pr["ka"].reshape(bsz, t_len, A_HEADS, HEAD_DIM)
    new_av = pr["va"].reshape(bsz, t_len, A_HEADS, HEAD_DIM)
    new_aki = pr["ki2"][:, :IDX_DIM].reshape(bsz, t_len, IDX_DIM)
    new_bk = pr["kb"].reshape(bsz, t_len, B_HEADS, HEAD_DIM)
    new_bv = pr["vb"].reshape(bsz, t_len, B_HEADS, HEAD_DIM)

    def all_keys(past_x, new_bf16, s_pad):
        width = new_bf16.shape[-1]
        parts = [new_bf16.reshape(bsz, t_len, width)]
        if past:
            parts.insert(0, past_x.astype(BF16))
        if s_pad > s_real:
            parts.append(jnp.zeros((bsz, s_pad - s_real, width), BF16))
        return parts[0] if len(parts) == 1 else jnp.concatenate(parts, axis=1)

    tq, kb = cfg["dsa_tq"], cfg["dsa_kb"]
    s_pad = _round_up(s_real, kb)
    topk = min(TOPK_MAX, s_real // 4)
    past_ki2 = jnp.concatenate([past_aki, past_aki], axis=-1) if past else None
    oa = _dsa(pr["qi"].reshape(bsz, t_len, -1), pr["wi"].reshape(bsz, t_len, LANES), pr["qa"].reshape(bsz, t_len, -1),
              all_keys(past_ki2, pr["ki2b"], s_pad),
              all_keys(past_ak.reshape(bsz, past, A_WIDTH), pr["kab"], s_pad),
              all_keys(past_av.reshape(bsz, past, A_WIDTH), pr["vab"], s_pad),
              tq=tq, kb=kb, s_real=s_real, past=past, topk=topk)

    tqb, kbb = cfg["sb_tq"], cfg["sb_kb"]
    s_pad_b = _round_up(s_real, kbb)
    idx = jnp.arange(kbb)
    tri = (idx[:, None] >= idx[None, :]).astype(BF16)
    ob = _stick(pr["qb"].reshape(bsz, t_len, -1),
                all_keys(past_bk.reshape(bsz, past, B_WIDTH), pr["kbb"], s_pad_b),
                all_keys(past_bv.reshape(bsz, past, B_WIDTH), pr["vbb"], s_pad_b),
                tri, tq=tqb, kb=kbb, past=past)

    y, h2, comb = _merge(x2, oa.reshape(n, A_WIDTH), ob.reshape(n, B_WIDTH), pr["gates"], mod,
                         w["wua"], w["wub"], w["wout"], w["n2g"], w["wr"], w["br"], t_len=t_len, tm=cfg["tm_merge"])
    out = _moe(y, h2, comb, mod, w["w_gate"], w["w_up"], w["w_down"], t_len=t_len, tm=cfg["tm_moe"])
    return out.reshape(bsz, t_len, D_MODEL), (new_ak, new_av, new_aki, new_bk, new_bv)


def _prep_weights(w_in, norm1_g, qnorm_g, knorm_g, w_up_a, w_up_b, w_out, norm2_g, w_rg, b_rg, w_re, b_re,
                  w_e_gate, w_e_up, w_e_down):
    o = 0
    seg = {}
    for name, width in (("qa", A_WIDTH), ("ka", A_WIDTH), ("va", A_WIDTH), ("qi", IDX_HEADS * IDX_DIM),
                        ("ki", IDX_DIM), ("wi", IDX_HEADS), ("qb", B_WIDTH), ("kb", B_WIDTH), ("vb", B_WIDTH),
                        ("gates", 2 * D_MODEL)):
        seg[name] = w_in[:, o:o + width]
        o += width
    pad_i = jnp.zeros((D_MODEL, LANES - IDX_HEADS), w_in.dtype)
    hd = jnp.arange(2 * A_WIDTH) // HEAD_DIM
    n_r = N_GROUPS + N_EXPERTS
    return dict(
        n1g=norm1_g.reshape(1, D_MODEL),
        wa=jnp.concatenate([seg["qa"], seg["ka"]], axis=1).astype(BF16),
        wv=jnp.concatenate([seg["va"], seg["qb"], seg["kb"], seg["vb"]], axis=1).astype(BF16),
        wi=jnp.concatenate([seg["qi"], seg["ki"], seg["ki"], seg["wi"], pad_i], axis=1).astype(BF16),
        wg=seg["gates"].astype(BF16),
        bd=((hd[:, None] == hd[None, :]).astype(F32) / HEAD_DIM).astype(BF16),
        gqk=jnp.concatenate([jnp.tile(qnorm_g, A_HEADS), jnp.tile(knorm_g, A_HEADS)]).reshape(1, 2 * A_WIDTH),
        wua=w_up_a.astype(BF16), wub=w_up_b.astype(BF16), wout=w_out.astype(BF16),
        n2g=norm2_g.reshape(1, D_MODEL),
        wr=jnp.pad(jnp.concatenate([w_rg, w_re], axis=1), ((0, 0), (0, ROUTER_LANES - n_r))),
        br=jnp.pad(jnp.concatenate([b_rg, b_re]), (0, ROUTER_LANES - n_r)).reshape(1, ROUTER_LANES),
        w_gate=w_e_gate, w_up=w_e_up, w_down=w_e_down,
    )


def _group_cfg(t_len):
    if t_len >= 512:
        return dict(tm=256, tm_merge=256, tm_moe=1024, dsa_tq=128, dsa_kb=512, sb_tq=256, sb_kb=256)
    return dict(tm=512, tm_merge=512, tm_moe=512, dsa_tq=t_len, dsa_kb=384, sb_tq=t_len, sb_kb=256)


def kernel(x_prompt, x_sample, cache_a_k, cache_a_v, cache_a_kidx, cache_b_k, cache_b_v, c_prompt, c_sample,
           w_ada, b_ada, norm1_g, w_in, qnorm_g, knorm_g, w_up_a, w_up_b, w_out, norm2_g,
           w_rg, b_rg, w_re, b_re, w_e_gate, w_e_up, w_e_down):
    depth = w_ada.shape[0]
    bp, bs = x_prompt.shape[0], x_sample.shape[0]
    dt = x_prompt.dtype
    rows = _round_up(bp + bs, 8)
    empty_a = jnp.zeros((bp, 0, A_HEADS, HEAD_DIM), dt)
    empty_i = jnp.zeros((bp, 0, IDX_DIM), dt)
    empty_b = jnp.zeros((bp, 0, B_HEADS, HEAD_DIM), dt)
    y_p, y_s = x_prompt, x_sample
    rows_p, rows_s = [], []
    c_all = jnp.concatenate([c_prompt, c_sample, jnp.zeros((rows - bp - bs, D_MODEL), dt)], axis=0)
    for l in range(depth):
        w = _prep_weights(w_in[l], norm1_g[l], qnorm_g[l], knorm_g[l], w_up_a[l], w_up_b[l], w_out[l], norm2_g[l],
                          w_rg[l], b_rg[l], w_re[l], b_re[l], w_e_gate[l], w_e_up[l], w_e_down[l])
        mod = _ada(c_all, w_ada[l], b_ada[l].reshape(1, -1))
        mod_p = mod[:bp].reshape(bp, 6, D_MODEL)
        mod_s = mod[bp:bp + bs].reshape(bs, 6, D_MODEL)
        y_p, new_p = _layer(y_p, mod_p, empty_a, empty_a, empty_i, empty_b, empty_b, w,
                            cfg=_group_cfg(y_p.shape[1]))
        y_s, new_s = _layer(y_s, mod_s, cache_a_k[l], cache_a_v[l], cache_a_kidx[l], cache_b_k[l], cache_b_v[l], w,
                            cfg=_group_cfg(y_s.shape[1]))
        rows_p.append(new_p)
        rows_s.append(new_s)
    stack = lambda rws, i: jnp.stack([r[i] for r in rws])
    return (y_p, y_s) + tuple(stack(rows_p, i) for i in range(5)) + tuple(stack(rows_s, i) for i in range(5))
```

```python
import functools

import jax
import jax.numpy as jnp
from jax import lax
from jax.experimental import pallas as pl
from jax.experimental.pallas import tpu as pltpu

D_MODEL = 1024
CHUNK = 64
CHUNK_SHIFT = 6
HEAD_DIM = 64
A_HEADS = 8
B_HEADS = 8
IDX_HEADS = 4
IDX_DIM = 64
TOPK_MAX = 256
ROPE_THETA = 500000.0
ROT_DIM = HEAD_DIM // 4
N_GROUPS = 4
EXPERTS_PER_GROUP = 8
N_EXPERTS = N_GROUPS * EXPERTS_PER_GROUP
D_EXPERT = 256
RMS_EPS = 1e-6
A_WIDTH = A_HEADS * HEAD_DIM
B_WIDTH = B_HEADS * HEAD_DIM

LANES = 128
IDX_SEG = 512
PAIRS = 4
COUNT_ROWS = 64
ROUTER_LANES = 128
EXPERT_LANE0 = N_GROUPS
VMEM_LIMIT = 56 * 1024 * 1024

F32 = jnp.float32
BF16 = jnp.bfloat16
I32 = jnp.int32
NEG_BIG = -1e30
INT_MIN = -(2 ** 31)
KEY_NEG_INF = (0xFF800000 - (1 << 32)) ^ 0x7FFFFFFF


def _dot(a, b):
    return jnp.dot(a, b, preferred_element_type=F32)


def _dot_nt(a, b):
    return lax.dot_general(a, b, (((1,), (1,)), ((), ())), preferred_element_type=F32)


def _dot_tn(a, b):
    return lax.dot_general(a, b, (((0,), (0,)), ((), ())), preferred_element_type=F32)


def _split_bf16(a):
    hi = a.astype(BF16)
    lo = (a - hi.astype(F32)).astype(BF16)
    return hi, lo


def _dot3(a, b):
    ah, al = _split_bf16(a)
    bh, bl = _split_bf16(b)
    return _dot(ah, bh) + (_dot(ah, bl) + _dot(al, bh))


def _silu(x):
    return x * (1.0 / (1.0 + jnp.exp(-x)))


def _sigmoid(x):
    return 1.0 / (1.0 + jnp.exp(-x))


def _rows_times(x, vec, nb):
    if nb == 1:
        return x * vec
    tm, d = x.shape
    return (x.reshape(nb, tm // nb, d) * vec[:, None, :]).reshape(tm, d)


def _rows_plus(x, vec, nb):
    if nb == 1:
        return x + vec
    tm, d = x.shape
    return (x.reshape(nb, tm // nb, d) + vec[:, None, :]).reshape(tm, d)


def _rms_rows(x):
    return x * lax.rsqrt(jnp.mean(x * x, axis=-1, keepdims=True) + RMS_EPS)


def _ada_kernel(c_ref, w_ref, b_ref, o_ref):
    o_ref[...] = _dot3(_silu(c_ref[...]), w_ref[...]) + b_ref[...]


def _ada(c_all, w_ada, b_ada):
    rows = c_all.shape[0]
    n = w_ada.shape[1]
    tn = 1536
    return pl.pallas_call(
        _ada_kernel,
        grid=(n // tn,),
        in_specs=[pl.BlockSpec((rows, D_MODEL), lambda j: (0, 0)),
                  pl.BlockSpec((D_MODEL, tn), lambda j: (0, j)),
                  pl.BlockSpec((1, tn), lambda j: (0, j))],
        out_specs=pl.BlockSpec((rows, tn), lambda j: (0, j)),
        out_shape=jax.ShapeDtypeStruct((rows, n), F32),
        compiler_params=pltpu.CompilerParams(dimension_semantics=("arbitrary",), vmem_limit_bytes=VMEM_LIMIT),
        name="ada_mod",
    )(c_all, w_ada, b_ada)


def _rope_lanes(a, tab_ref, reps):
    width = a.shape[-1]
    cos = jnp.tile(tab_ref[0], (1, reps)) if reps > 1 else tab_ref[0]
    sin_lo = jnp.tile(tab_ref[1], (1, reps)) if reps > 1 else tab_ref[1]
    sin_hi = jnp.tile(tab_ref[2], (1, reps)) if reps > 1 else tab_ref[2]
    half = ROT_DIM // 2
    return a * cos + pltpu.roll(a, width - half, 1) * sin_lo + pltpu.roll(a, half, 1) * sin_hi


def _proj_kernel(x_ref, mod_ref, n1g_ref, wa_ref, wv_ref, wi_ref, wg_ref, bd_ref, gqk_ref, tab_ref,
                 qa_ref, ka_ref, kab_ref, va_ref, vab_ref, qb_ref, kb_ref, kbb_ref, vb_ref, vbb_ref,
                 qi_ref, ki2_ref, ki2b_ref, wi_out_ref, gates_ref, *, nb):
    x = x_ref[...]
    sh1 = mod_ref[:, 0, :]
    sc1 = mod_ref[:, 1, :]
    h = _rows_plus(_rows_times(_rms_rows(x) * n1g_ref[...], 1.0 + sc1, nb), sh1, nb)
    hb = h.astype(BF16)

    a = _dot(hb, wa_ref[...])
    ms = _dot((a * a).astype(BF16), bd_ref[...])
    a = a * lax.rsqrt(ms + RMS_EPS) * gqk_ref[...]
    a = _rope_lanes(a, tab_ref, a.shape[-1] // LANES)
    qa_ref[...] = (a[:, :A_WIDTH] * (HEAD_DIM ** -0.5)).astype(BF16)
    ka = a[:, A_WIDTH:]
    ka_ref[...] = ka
    kab_ref[...] = ka.astype(BF16)

    v = _dot(hb, wv_ref[...])
    va = v[:, :A_WIDTH]
    va_ref[...] = va
    vab_ref[...] = va.astype(BF16)
    qb_ref[...] = (v[:, A_WIDTH:A_WIDTH + B_WIDTH] * (HEAD_DIM ** -0.5)).astype(BF16)
    kb = v[:, A_WIDTH + B_WIDTH:A_WIDTH + 2 * B_WIDTH]
    kb_ref[...] = kb
    kbb_ref[...] = kb.astype(BF16)
    vb = v[:, A_WIDTH + 2 * B_WIDTH:]
    vb_ref[...] = vb
    vbb_ref[...] = vb.astype(BF16)

    i = _dot(hb, wi_ref[...])
    nr = IDX_SEG - LANES
    ir = _rope_lanes(i[:, :nr], tab_ref, nr // LANES)
    nq = IDX_HEADS * IDX_DIM
    qi_ref[...] = (ir[:, :nq] * (IDX_DIM ** -0.5)).astype(BF16)
    ki2 = ir[:, nq:]
    ki2_ref[...] = ki2
    ki2b_ref[...] = ki2.astype(BF16)
    wi_out_ref[...] = i[:, nr:]

    gates_ref[...] = _sigmoid(_dot(hb, wg_ref[...]))


def _proj(x2, mod, n1g, wa, wv, wi, wg, bd, gqk, tab, *, t_len, tm):
    n = x2.shape[0]
    nb = max(1, tm // t_len)
    tiles_per_batch = max(1, t_len // tm)
    tab_tiles = tab.shape[1] // tm
    row = lambda i: (i, 0)
    const2 = lambda i: (0, 0)
    tabmap = lambda i: (0, i % tab_tiles, 0)
    in_specs = [
        pl.BlockSpec((tm, D_MODEL), row),
        pl.BlockSpec((nb, 6, D_MODEL), lambda i: (i // tiles_per_batch, 0, 0)),
        pl.BlockSpec((1, D_MODEL), const2),
        pl.BlockSpec(wa.shape, const2),
        pl.BlockSpec(wv.shape, const2),
        pl.BlockSpec(wi.shape, const2),
        pl.BlockSpec(wg.shape, const2),
        pl.BlockSpec(bd.shape, const2),
        pl.BlockSpec((1, 2 * A_WIDTH), const2),
        pl.BlockSpec((3, tm, LANES), tabmap),
    ]
    outs = [
        ("qa", A_WIDTH, BF16), ("ka", A_WIDTH, F32), ("kab", A_WIDTH, BF16),
        ("va", A_WIDTH, F32), ("vab", A_WIDTH, BF16),
        ("qb", B_WIDTH, BF16), ("kb", B_WIDTH, F32), ("kbb", B_WIDTH, BF16),
        ("vb", B_WIDTH, F32), ("vbb", B_WIDTH, BF16),
        ("qi", IDX_HEADS * IDX_DIM, BF16), ("ki2", LANES, F32), ("ki2b", LANES, BF16), ("wi", LANES, F32),
        ("gates", 2 * D_MODEL, F32),
    ]
    res = pl.pallas_call(
        functools.partial(_proj_kernel, nb=nb),
        grid=(n // tm,),
        in_specs=in_specs,
        out_specs=[pl.BlockSpec((tm, w), row) for _, w, _ in outs],
        out_shape=[jax.ShapeDtypeStruct((n, w), dt) for _, w, dt in outs],
        compiler_params=pltpu.CompilerParams(dimension_semantics=("arbitrary",), vmem_limit_bytes=VMEM_LIMIT),
        name="in_proj",
    )(x2, mod, n1g, wa, wv, wi, wg, bd, gqk, tab)
    return dict(zip([o[0] for o in outs], res))


def _lower_tri(n):
    idx = jnp.arange(n)
    return (idx[:, None] >= idx[None, :]).astype(BF16)


def _head_halves(x):
    lane = lax.broadcasted_iota(I32, x.shape, 1)
    zero = jnp.zeros_like(x)
    return jnp.where(lane < HEAD_DIM, x, zero), jnp.where(lane >= HEAD_DIM, x, zero)


def _dsa_kernel(qi_ref, w_ref, qa_ref, ki_ref, ka_ref, va_ref, tri_ref, o_ref,
                s_scr, bias_scr, m_scr, l_scr, acc_scr, *, tq, kb, s_real, past, topk):
    qt = pl.program_id(1)
    q0 = past + qt * tq
    kmax = jnp.minimum((lax.shift_right_logical(q0 + tq - 1, CHUNK_SHIFT) + 1) * CHUNK, s_real)
    nblk = (kmax + kb - 1) // kb
    qchunk = lax.shift_right_logical(q0 + lax.broadcasted_iota(I32, (1, tq), 1), CHUNK_SHIFT)
    kf = float(topk)
    chunks = lambda ref, i: ref[0, :, i * LANES:(i + 1) * LANES]
    qi_heads = [q for i in range(IDX_HEADS // 2) for q in _head_halves(chunks(qi_ref, i))]
    qa_heads = [q for i in range(PAIRS) for q in _head_halves(chunks(qa_ref, i))]
    w_rows = w_ref[0].T

    def fold_rows(x, op):
        part = op(x.reshape(kb // COUNT_ROWS, COUNT_ROWS, tq), axis=0)
        return op(part, axis=0, keepdims=True)

    def score_block(j, masked):
        r0 = pl.multiple_of(j * kb, kb)
        kib = ki_ref[0, pl.ds(r0, kb), :]
        acc = jnp.zeros((kb, tq), F32)
        for h in range(IDX_HEADS):
            rel = jnp.maximum(_dot_nt(kib, qi_heads[h]), 0.0)
            acc = acc + w_rows[h:h + 1, :] * rel
        s = acc + 0.0
        if masked:
            kpos = r0 + lax.broadcasted_iota(I32, (kb, tq), 0)
            adm = (lax.shift_right_logical(kpos, CHUNK_SHIFT) <= qchunk) & (kpos < s_real)
            s = jnp.where(adm, s, -jnp.inf)
        s_scr[pl.ds(r0, kb), :] = s

    n_open = jnp.minimum((lax.shift_right_logical(q0, CHUNK_SHIFT) + 1) * CHUNK, s_real) // kb

    def open_step(j, c):
        score_block(j, False)
        return c

    def masked_step(j, c):
        score_block(j, True)
        return c

    lax.fori_loop(0, n_open, open_step, 0)
    lax.fori_loop(n_open, nblk, masked_step, 0)

    def count(pred):
        def body(j, acc):
            r0 = pl.multiple_of(j * kb, kb)
            hit = jnp.where(pred(s_scr[pl.ds(r0, kb), :]), 1.0, 0.0)
            return acc + jnp.sum(hit.reshape(kb // COUNT_ROWS, COUNT_ROWS, tq), axis=0)
        acc = lax.fori_loop(0, nblk, body, jnp.zeros((COUNT_ROWS, tq), F32))
        return jnp.sum(acc, axis=0, keepdims=True)

    def key_to_score(key):
        return pltpu.bitcast(jnp.where(key < 0, key ^ 0x7FFFFFFF, key), F32)

    c0 = count(lambda blk: blk >= 0.0)
    t0 = jnp.where(c0 >= kf, 0, INT_MIN).astype(I32)

    def value_bit(i, t):
        cand = t | jnp.left_shift(jnp.int32(1), 30 - i)
        cand_f = key_to_score(cand)
        return jnp.where(count(lambda blk: blk >= cand_f) >= kf, cand, t)

    t_key = lax.fori_loop(0, 31, value_bit, t0)
    t = key_to_score(jnp.maximum(t_key, KEY_NEG_INF))

    need = kf - count(lambda blk: blk > t)
    tri = tri_ref[...]

    def select_block(j, seen):
        r0 = pl.multiple_of(j * kb, kb)
        blk = s_scr[pl.ds(r0, kb), :]
        tie = jnp.where((blk == t) & (blk > -jnp.inf), 1.0, 0.0)
        rank = _dot(tri, tie.astype(BF16)) + seen
        chosen = (blk > t) | ((tie > 0.0) & (rank <= need))
        bias_scr[pl.ds(r0, kb), :] = jnp.where(chosen, 0.0, NEG_BIG)
        return seen + fold_rows(tie, jnp.sum)

    lax.fori_loop(0, nblk, select_block, jnp.zeros((1, tq), F32))

    m_scr[...] = jnp.full(m_scr.shape, NEG_BIG, F32)
    l_scr[...] = jnp.zeros(l_scr.shape, F32)
    acc_scr[...] = jnp.zeros(acc_scr.shape, F32)

    def attend_block(j, carry):
        r0 = pl.multiple_of(j * kb, kb)
        bias = bias_scr[pl.ds(r0, kb), :]
        heads = range(A_HEADS)
        m_old = m_scr[...]
        l_old = l_scr[...]
        k_pair = [ka_ref[0, pl.ds(r0, kb), i * LANES:(i + 1) * LANES] for i in range(PAIRS)]
        v_pair = [va_ref[0, pl.ds(r0, kb), i * LANES:(i + 1) * LANES] for i in range(PAIRS)]
        ss = [_dot_nt(k_pair[h // 2], qa_heads[h]) + bias for h in heads]
        m_new = [jnp.maximum(m_old[h:h + 1, :], fold_rows(ss[h], jnp.max)) for h in heads]
        ps = [jnp.exp(ss[h] - m_new[h]) for h in heads]
        pvs = [_dot_tn(v_pair[h // 2], ps[h].astype(BF16)) for h in heads]
        alphas = [jnp.exp(m_old[h:h + 1, :] - m_new[h]) for h in heads]
        for i in range(PAIRS):
            e, o = 2 * i, 2 * i + 1
            acc_scr[i] = (jnp.where(even_rows, alphas[e], alphas[o]) * acc_scr[i]
                          + jnp.where(even_rows, pvs[e], pvs[o]))
        for h in heads:
            l_scr[h:h + 1, :] = alphas[h] * l_old[h:h + 1, :] + fold_rows(ps[h], jnp.sum)
            m_scr[h:h + 1, :] = m_new[h]
        return carry

    even_rows = lax.broadcasted_iota(I32, (LANES, tq), 0) < HEAD_DIM
    lax.fori_loop(0, nblk, attend_block, 0)

    l_fin = l_scr[...]
    for i in range(PAIRS):
        denom = jnp.where(even_rows, l_fin[2 * i:2 * i + 1, :], l_fin[2 * i + 1:2 * i + 2, :])
        o_ref[0, :, i * LANES:(i + 1) * LANES] = (acc_scr[i] / denom).T.astype(o_ref.dtype)


def _dsa(qi, wi, qa, ki2, ka, va, *, tq, kb, s_real, past, topk):
    bsz, t_len, _ = qa.shape
    s_pad = ki2.shape[1]
    kern = functools.partial(_dsa_kernel, tq=tq, kb=kb, s_real=s_real, past=past, topk=topk)
    qmap = lambda b, q: (b, q, 0)
    kmap = lambda b, q: (b, 0, 0)
    return pl.pallas_call(
        kern,
        grid=(bsz, t_len // tq),
        in_specs=[
            pl.BlockSpec((1, tq, IDX_HEADS * IDX_DIM), qmap),
            pl.BlockSpec((1, tq, LANES), qmap),
            pl.BlockSpec((1, tq, A_WIDTH), qmap),
            pl.BlockSpec((1, s_pad, LANES), kmap),
            pl.BlockSpec((1, s_pad, A_WIDTH), kmap),
            pl.BlockSpec((1, s_pad, A_WIDTH), kmap),
            pl.BlockSpec((kb, kb), lambda b, q: (0, 0)),
        ],
        out_specs=pl.BlockSpec((1, tq, A_WIDTH), qmap),
        out_shape=jax.ShapeDtypeStruct((bsz, t_len, A_WIDTH), BF16),
        scratch_shapes=[
            pltpu.VMEM((s_pad, tq), F32),
            pltpu.VMEM((s_pad, tq), F32),
            pltpu.VMEM((A_HEADS, tq), F32),
            pltpu.VMEM((A_HEADS, tq), F32),
            pltpu.VMEM((PAIRS, LANES, tq), F32),
        ],
        compiler_params=pltpu.CompilerParams(dimension_semantics=("arbitrary", "arbitrary"),
                                             vmem_limit_bytes=VMEM_LIMIT),
        name="dsa_attention",
    )(qi, wi, qa, ki2, ka, va, _lower_tri(kb))


def _stick_kernel(q_ref, k_ref, v_ref, tri_ref, o_ref, carry_scr, acc_scr, *, tq, kb, past):
    qt = pl.program_id(1)
    q0 = past + qt * tq
    nblk = (q0 + tq - 1 + kb - 1) // kb
    n_full = q0 // kb
    qpos = q0 + lax.broadcasted_iota(I32, (tq, kb), 0)
    carry_scr[...] = jnp.zeros(carry_scr.shape, F32)
    acc_scr[...] = jnp.zeros(acc_scr.shape, F32)
    tri2 = tri_ref[...]
    heads = range(B_HEADS)
    q_heads = [q for i in range(PAIRS) for q in _head_halves(q_ref[0, :, i * LANES:(i + 1) * LANES])]
    even_lanes = lax.broadcasted_iota(I32, (tq, LANES), 1) < HEAD_DIM

    def block(j, masked):
        r0 = pl.multiple_of(j * kb, kb)
        causal = ((r0 + lax.broadcasted_iota(I32, (tq, kb), 1)) < qpos) if masked else None
        carries = [carry_scr[h] for h in heads]
        k_pair = [k_ref[0, pl.ds(r0, kb), i * LANES:(i + 1) * LANES] for i in range(PAIRS)]
        v_pair = [v_ref[0, pl.ds(r0, kb), i * LANES:(i + 1) * LANES] for i in range(PAIRS)]
        zs = [_dot_nt(q_heads[h], k_pair[h // 2]) for h in heads]
        sps = []
        for z in zs:
            sp = jnp.maximum(z, 0.0) + jnp.log(1.0 + jnp.exp(-jnp.abs(z)))
            sps.append(jnp.where(causal, sp, 0.0) if masked else sp)
        sufs = []
        for h in heads:
            hi, lo = _split_bf16(sps[h])
            hilo = jnp.concatenate([hi, lo], axis=1)
            sufs.append(_dot(hilo, tri2) + jnp.tile(carries[h], (1, kb // LANES)))
        pvs = []
        for h in heads:
            a = jnp.exp(zs[h] - sufs[h])
            if masked:
                a = jnp.where(causal, a, 0.0)
            pvs.append(_dot(a.astype(BF16), v_pair[h // 2]))
        for i in range(PAIRS):
            acc_scr[i] = acc_scr[i] + jnp.where(even_lanes, pvs[2 * i], pvs[2 * i + 1])
        for h in heads:
            carry_scr[h] = carries[h] + jnp.sum(sps[h], axis=-1, keepdims=True)

    def masked_step(jj, c):
        block(nblk - 1 - jj, True)
        return c

    def full_step(jj, c):
        block(n_full - 1 - jj, False)
        return c

    lax.fori_loop(0, nblk - n_full, masked_step, 0)
    lax.fori_loop(0, n_full, full_step, 0)
    for i in range(PAIRS):
        o_ref[0, :, i * LANES:(i + 1) * LANES] = acc_scr[i].astype(o_ref.dtype)


def _stick(q, k, v, *, tq, kb, past):
    bsz, t_len, _ = q.shape
    s_pad = k.shape[1]
    kern = functools.partial(_stick_kernel, tq=tq, kb=kb, past=past)
    kmap = lambda b, q: (b, 0, 0)
    return pl.pallas_call(
        kern,
        grid=(bsz, t_len // tq),
        in_specs=[
            pl.BlockSpec((1, tq, B_WIDTH), lambda b, q: (b, q, 0)),
            pl.BlockSpec((1, s_pad, B_WIDTH), kmap),
            pl.BlockSpec((1, s_pad, B_WIDTH), kmap),
            pl.BlockSpec((2 * kb, kb), lambda b, q: (0, 0)),
        ],
        out_specs=pl.BlockSpec((1, tq, B_WIDTH), lambda b, q: (b, q, 0)),
        out_shape=jax.ShapeDtypeStruct((bsz, t_len, B_WIDTH), BF16),
        scratch_shapes=[
            pltpu.VMEM((B_HEADS, tq, LANES), F32),
            pltpu.VMEM((PAIRS, tq, LANES), F32),
        ],
        compiler_params=pltpu.CompilerParams(dimension_semantics=("arbitrary", "arbitrary"),
                                             vmem_limit_bytes=VMEM_LIMIT),
        name="stick_attention",
    )(q, k, v, jnp.tile(_lower_tri(kb), (2, 1)))


def _merge_kernel(x_ref, oa_ref, ob_ref, gates_ref, mod_ref, wua_ref, wub_ref, wout_ref, n2g_ref, wr_ref, br_ref,
                  y_ref, h2_ref, comb_ref, *, nb):
    g1 = mod_ref[:, 2, :]
    sh2 = mod_ref[:, 3, :]
    sc2 = mod_ref[:, 4, :]
    gates = gates_ref[...]
    merged = gates[:, :D_MODEL] * _dot(oa_ref[...], wua_ref[...]) + gates[:, D_MODEL:] * _dot(ob_ref[...], wub_ref[...])
    y = x_ref[...] + _rows_times(_dot(merged.astype(BF16), wout_ref[...]), g1, nb)
    y_ref[...] = y
    h2 = _rows_plus(_rows_times(_rms_rows(y) * n2g_ref[...], 1.0 + sc2, nb), sh2, nb)
    h2_ref[...] = h2.astype(BF16)

    logits = _dot3(h2, wr_ref[...]) + br_ref[...]
    lane = lax.broadcasted_iota(I32, logits.shape, 1).astype(F32)
    far = float(ROUTER_LANES)
    is_g = lane < float(N_GROUPS)
    gl = jnp.where(is_g, logits, -jnp.inf)
    gmax = jnp.max(gl, axis=-1, keepdims=True)
    gsel = jnp.min(jnp.where(is_g & (gl == gmax), lane, far), axis=-1, keepdims=True)
    p_group = 1.0 / jnp.sum(jnp.where(is_g, jnp.exp(gl - gmax), 0.0), axis=-1, keepdims=True)
    e_lo = float(EXPERT_LANE0) + gsel * float(EXPERTS_PER_GROUP)
    in_grp = (lane >= e_lo) & (lane < e_lo + float(EXPERTS_PER_GROUP))
    el = jnp.where(in_grp, logits, -jnp.inf)
    v1 = jnp.max(el, axis=-1, keepdims=True)
    i1 = jnp.min(jnp.where(in_grp & (el == v1), lane, far), axis=-1, keepdims=True)
    el2 = jnp.where(lane == i1, -jnp.inf, el)
    v2 = jnp.max(el2, axis=-1, keepdims=True)
    i2 = jnp.min(jnp.where(in_grp & (lane != i1) & (el2 == v2), lane, far), axis=-1, keepdims=True)
    e2 = jnp.exp(v2 - v1)
    w1 = p_group / (1.0 + e2)
    w2 = p_group * e2 / (1.0 + e2)
    comb_ref[...] = jnp.where(lane == i1, w1, 0.0) + jnp.where(lane == i2, w2, 0.0)


def _merge(x2, oa, ob, gates, mod, wua, wub, wout, n2g, wr, br, *, t_len, tm):
    n = x2.shape[0]
    nb = max(1, tm // t_len)
    tiles_per_batch = max(1, t_len // tm)
    row = lambda i: (i, 0)
    const2 = lambda i: (0, 0)
    return pl.pallas_call(
        functools.partial(_merge_kernel, nb=nb),
        grid=(n // tm,),
        in_specs=[
            pl.BlockSpec((tm, D_MODEL), row),
            pl.BlockSpec((tm, A_WIDTH), row),
            pl.BlockSpec((tm, B_WIDTH), row),
            pl.BlockSpec((tm, 2 * D_MODEL), row),
            pl.BlockSpec((nb, 6, D_MODEL), lambda i: (i // tiles_per_batch, 0, 0)),
            pl.BlockSpec(wua.shape, const2),
            pl.BlockSpec(wub.shape, const2),
            pl.BlockSpec(wout.shape, const2),
            pl.BlockSpec((1, D_MODEL), const2),
            pl.BlockSpec(wr.shape, const2),
            pl.BlockSpec((1, ROUTER_LANES), const2),
        ],
        out_specs=[pl.BlockSpec((tm, D_MODEL), row), pl.BlockSpec((tm, D_MODEL), row),
                   pl.BlockSpec((tm, ROUTER_LANES), row)],
        out_shape=[jax.ShapeDtypeStruct((n, D_MODEL), F32), jax.ShapeDtypeStruct((n, D_MODEL), BF16),
                   jax.ShapeDtypeStruct((n, ROUTER_LANES), F32)],
        compiler_params=pltpu.CompilerParams(dimension_semantics=("arbitrary",), vmem_limit_bytes=VMEM_LIMIT),
        name="merge_router",
    )(x2, oa, ob, gates, mod, wua, wub, wout, n2g, wr, br)


def _moe_kernel(y_ref, h2_ref, comb_ref, mod_ref, wg_ref, wu_ref, wd_ref, o_ref, acc_scr, *, nb):
    e = pl.program_id(1)

    @pl.when(e == 0)
    def _():
        acc_scr[...] = jnp.zeros(acc_scr.shape, F32)

    h2 = h2_ref[...]
    act = _silu(_dot(h2, wg_ref[0].astype(BF16))) * _dot(h2, wu_ref[0].astype(BF16))
    contrib = _dot(act.astype(BF16), wd_ref[0].astype(BF16))
    comb = comb_ref[...]
    lane = lax.broadcasted_iota(I32, comb.shape, 1)
    w_e = jnp.sum(jnp.where(lane == e + EXPERT_LANE0, comb, 0.0), axis=-1, keepdims=True)
    acc_scr[...] += w_e * contrib

    @pl.when(e == pl.num_programs(1) - 1)
    def _():
        o_ref[...] = y_ref[...] + _rows_times(acc_scr[...], mod_ref[:, 5, :], nb)


def _moe(y, h2, comb, mod, w_gate, w_up, w_down, *, t_len, tm):
    n = y.shape[0]
    nb = max(1, tm // t_len)
    tiles_per_batch = max(1, t_len // tm)
    row = lambda i, e: (i, 0)
    return pl.pallas_call(
        functools.partial(_moe_kernel, nb=nb),
        grid=(n // tm, N_EXPERTS),
        in_specs=[
            pl.BlockSpec((tm, D_MODEL), row),
            pl.BlockSpec((tm, D_MODEL), row),
            pl.BlockSpec((tm, ROUTER_LANES), row),
            pl.BlockSpec((nb, 6, D_MODEL), lambda i, e: (i // tiles_per_batch, 0, 0)),
            pl.BlockSpec((1, D_MODEL, D_EXPERT), lambda i, e: (e, 0, 0)),
            pl.BlockSpec((1, D_MODEL, D_EXPERT), lambda i, e: (e, 0, 0)),
            pl.BlockSpec((1, D_EXPERT, D_MODEL), lambda i, e: (e, 0, 0)),
        ],
        out_specs=pl.BlockSpec((tm, D_MODEL), row),
        out_shape=jax.ShapeDtypeStruct((n, D_MODEL), F32),
        scratch_shapes=[pltpu.VMEM((tm, D_MODEL), F32)],
        compiler_params=pltpu.CompilerParams(dimension_semantics=("arbitrary", "arbitrary"),
                                             vmem_limit_bytes=VMEM_LIMIT),
        name="moe_dense",
    )(y, h2, comb, mod, w_gate, w_up, w_down)


def _rope_tables(pos):
    half = ROT_DIM // 2
    freqs = ROPE_THETA ** (-jnp.arange(0, ROT_DIM, 2, dtype=F32) / ROT_DIM)
    ang = pos.astype(F32)[:, None] * freqs[None, :]
    cos, sin = jnp.cos(ang), jnp.sin(ang)
    dh = jnp.arange(LANES) % HEAD_DIM
    fi = dh % half
    cos_t = jnp.where(dh < ROT_DIM, cos[:, fi], 1.0)
    sin_lo = jnp.where(dh < half, -sin[:, fi], 0.0)
    sin_hi = jnp.where((dh >= half) & (dh < ROT_DIM), sin[:, fi], 0.0)
    return jnp.stack([cos_t, sin_lo, sin_hi]).astype(F32)


def _round_up(a, b):
    return (a + b - 1) // b * b


def _layer(x, mod, past_ak, past_av, past_aki, past_bk, past_bv, w, *, cfg):
    bsz, t_len, _ = x.shape
    past = past_ak.shape[1]
    s_real = past + t_len
    n = bsz * t_len
    x2 = x.reshape(n, D_MODEL)
    pos = past + jnp.arange(t_len, dtype=jnp.int32)

    tm = cfg["tm"]
    tab = jnp.tile(_rope_tables(pos), (1, max(1, tm // t_len), 1))
    pr = _proj(x2, mod, w["n1g"], w["wa"], w["wv"], w["wi"], w["wg"], w["bd"], w["gqk"], tab, t_len=t_len, tm=tm)

    new_ak = pr["ka"].reshape(bsz, t_len, A_HEADS, HEAD_DIM)
    new_av = pr["va"].reshape(bsz, t_len, A_HEADS, HEAD_DIM)
    new_aki = pr["ki2"][:, :IDX_DIM].reshape(bsz, t_len, IDX_DIM)
    new_bk = pr["kb"].reshape(bsz, t_len, B_HEADS, HEAD_DIM)
    new_bv = pr["vb"].reshape(bsz, t_len, B_HEADS, HEAD_DIM)

    def all_keys(past_x, new_bf16, s_pad):
        width = new_bf16.shape[-1]
        parts = [new_bf16.reshape(bsz, t_len, width)]
        if past:
            parts.insert(0, past_x.astype(BF16))
        if s_pad > s_real:
            parts.append(jnp.zeros((bsz, s_pad - s_real, width), BF16))
        return parts[0] if len(parts) == 1 else jnp.concatenate(parts, axis=1)

    tq, kb = cfg["dsa_tq"], cfg["dsa_kb"]
    s_pad = _round_up(s_real, kb)
    topk = min(TOPK_MAX, s_real // 4)
    past_ki2 = jnp.concatenate([past_aki, past_aki], axis=-1) if past else None
    oa = _dsa(pr["qi"].reshape(bsz, t_len, -1), pr["wi"].reshape(bsz, t_len, LANES), pr["qa"].reshape(bsz, t_len, -1),
              all_keys(past_ki2, pr["ki2b"], s_pad),
              all_keys(past_ak.reshape(bsz, past, A_WIDTH), pr["kab"], s_pad),
              all_keys(past_av.reshape(bsz, past, A_WIDTH), pr["vab"], s_pad),
              tq=tq, kb=kb, s_real=s_real, past=past, topk=topk)

    tqb, kbb = cfg["sb_tq"], cfg["sb_kb"]
    s_pad_b = _round_up(s_real, kbb)
    ob = _stick(pr["qb"].reshape(bsz, t_len, -1),
                all_keys(past_bk.reshape(bsz, past, B_WIDTH), pr["kbb"], s_pad_b),
                all_keys(past_bv.reshape(bsz, past, B_WIDTH), pr["vbb"], s_pad_b),
                tq=tqb, kb=kbb, past=past)

    y, h2, comb = _merge(x2, oa.reshape(n, A_WIDTH), ob.reshape(n, B_WIDTH), pr["gates"], mod,
                         w["wua"], w["wub"], w["wout"], w["n2g"], w["wr"], w["br"], t_len=t_len, tm=cfg["tm_merge"])
    out = _moe(y, h2, comb, mod, w["w_gate"], w["w_up"], w["w_down"], t_len=t_len, tm=cfg["tm_moe"])
    return out.reshape(bsz, t_len, D_MODEL), (new_ak, new_av, new_aki, new_bk, new_bv)


def _prep_weights(w_in, norm1_g, qnorm_g, knorm_g, w_up_a, w_up_b, w_out, norm2_g, w_rg, b_rg, w_re, b_re,
                  w_e_gate, w_e_up, w_e_down):
    o = 0
    seg = {}
    for name, width in (("qa", A_WIDTH), ("ka", A_WIDTH), ("va", A_WIDTH), ("qi", IDX_HEADS * IDX_DIM),
                        ("ki", IDX_DIM), ("wi", IDX_HEADS), ("qb", B_WIDTH), ("kb", B_WIDTH), ("vb", B_WIDTH),
                        ("gates", 2 * D_MODEL)):
        seg[name] = w_in[:, o:o + width]
        o += width
    pad_i = jnp.zeros((D_MODEL, LANES - IDX_HEADS), w_in.dtype)
    hd = jnp.arange(2 * A_WIDTH) // HEAD_DIM
    n_r = N_GROUPS + N_EXPERTS
    return dict(
        n1g=norm1_g.reshape(1, D_MODEL),
        wa=jnp.concatenate([seg["qa"], seg["ka"]], axis=1).astype(BF16),
        wv=jnp.concatenate([seg["va"], seg["qb"], seg["kb"], seg["vb"]], axis=1).astype(BF16),
        wi=jnp.concatenate([seg["qi"], seg["ki"], seg["ki"], seg["wi"], pad_i], axis=1).astype(BF16),
        wg=seg["gates"].astype(BF16),
        bd=((hd[:, None] == hd[None, :]).astype(F32) / HEAD_DIM).astype(BF16),
        gqk=jnp.concatenate([jnp.tile(qnorm_g, A_HEADS), jnp.tile(knorm_g, A_HEADS)]).reshape(1, 2 * A_WIDTH),
        wua=w_up_a.astype(BF16), wub=w_up_b.astype(BF16), wout=w_out.astype(BF16),
        n2g=norm2_g.reshape(1, D_MODEL),
        wr=jnp.pad(jnp.concatenate([w_rg, w_re], axis=1), ((0, 0), (0, ROUTER_LANES - n_r))),
        br=jnp.pad(jnp.concatenate([b_rg, b_re]), (0, ROUTER_LANES - n_r)).reshape(1, ROUTER_LANES),
        w_gate=w_e_gate, w_up=w_e_up, w_down=w_e_down,
    )


def _group_cfg(t_len):
    if t_len >= 512:
        return dict(tm=256, tm_merge=256, tm_moe=1024, dsa_tq=256, dsa_kb=512, sb_tq=256, sb_kb=256)
    return dict(tm=512, tm_merge=512, tm_moe=512, dsa_tq=t_len, dsa_kb=384, sb_tq=t_len, sb_kb=256)


def kernel(x_prompt, x_sample, cache_a_k, cache_a_v, cache_a_kidx, cache_b_k, cache_b_v, c_prompt, c_sample,
           w_ada, b_ada, norm1_g, w_in, qnorm_g, knorm_g, w_up_a, w_up_b, w_out, norm2_g,
           w_rg, b_rg, w_re, b_re, w_e_gate, w_e_up, w_e_down):
    depth = w_ada.shape[0]
    bp, bs = x_prompt.shape[0], x_sample.shape[0]
    dt = x_prompt.dtype
    rows = _round_up(bp + bs, 8)
    empty_a = jnp.zeros((bp, 0, A_HEADS, HEAD_DIM), dt)
    empty_i = jnp.zeros((bp, 0, IDX_DIM), dt)
    empty_b = jnp.zeros((bp, 0, B_HEADS, HEAD_DIM), dt)
    y_p, y_s = x_prompt, x_sample
    rows_p, rows_s = [], []
    c_all = jnp.concatenate([c_prompt, c_sample, jnp.zeros((rows - bp - bs, D_MODEL), dt)], axis=0)
    for l in range(depth):
        w = _prep_weights(w_in[l], norm1_g[l], qnorm_g[l], knorm_g[l], w_up_a[l], w_up_b[l], w_out[l], norm2_g[l],
                          w_rg[l], b_rg[l], w_re[l], b_re[l], w_e_gate[l], w_e_up[l], w_e_down[l])
        mod = _ada(c_all, w_ada[l], b_ada[l].reshape(1, -1))
        mod_p = mod[:bp].reshape(bp, 6, D_MODEL)
        mod_s = mod[bp:bp + bs].reshape(bs, 6, D_MODEL)
        y_p, new_p = _layer(y_p, mod_p, empty_a, empty_a, empty_i, empty_b, empty_b, w,
                            cfg=_group_cfg(y_p.shape[1]))
        y_s, new_s = _layer(y_s, mod_s, cache_a_k[l], cache_a_v[l], cache_a_kidx[l], cache_b_k[l], cache_b_v[l], w,
                            cfg=_group_cfg(y_s.shape[1]))
        rows_p.append(new_p)
        rows_s.append(new_s)
    stack = lambda rws, i: jnp.stack([r[i] for r in rws])
    return (y_p, y_s) + tuple(stack(rows_p, i) for i in range(5)) + tuple(stack(rows_s, i) for i in range(5))
```

```python
import functools

import jax
import jax.numpy as jnp
from jax import lax
from jax.experimental import pallas as pl
from jax.experimental.pallas import tpu as pltpu

D_MODEL = 1024
CHUNK = 64
CHUNK_SHIFT = 6
HEAD_DIM = 64
A_HEADS = 8
B_HEADS = 8
IDX_HEADS = 4
IDX_DIM = 64
TOPK_MAX = 256
ROPE_THETA = 500000.0
ROT_DIM = HEAD_DIM // 4
N_GROUPS = 4
EXPERTS_PER_GROUP = 8
N_EXPERTS = N_GROUPS * EXPERTS_PER_GROUP
D_EXPERT = 256
RMS_EPS = 1e-6
A_WIDTH = A_HEADS * HEAD_DIM
B_WIDTH = B_HEADS * HEAD_DIM

LANES = 128
IDX_SEG = 512
PAIRS = 4
COUNT_ROWS = 64
ROUTER_LANES = 128
EXPERT_LANE0 = N_GROUPS
VMEM_LIMIT = 56 * 1024 * 1024

F32 = jnp.float32
BF16 = jnp.bfloat16
I32 = jnp.int32
NEG_BIG = -1e30
INT_MIN = -(2 ** 31)
KEY_NEG_INF = (0xFF800000 - (1 << 32)) ^ 0x7FFFFFFF


def _dot(a, b):
    return jnp.dot(a, b, preferred_element_type=F32)


def _dot_nt(a, b):
    return lax.dot_general(a, b, (((1,), (1,)), ((), ())), preferred_element_type=F32)


def _dot_tn(a, b):
    return lax.dot_general(a, b, (((0,), (0,)), ((), ())), preferred_element_type=F32)


def _split_bf16(a):
    hi = a.astype(BF16)
    lo = (a - hi.astype(F32)).astype(BF16)
    return hi, lo


def _dot3(a, b):
    ah, al = _split_bf16(a)
    bh, bl = _split_bf16(b)
    return _dot(ah, bh) + (_dot(ah, bl) + _dot(al, bh))


def _silu(x):
    return x * (1.0 / (1.0 + jnp.exp(-x)))


def _sigmoid(x):
    return 1.0 / (1.0 + jnp.exp(-x))


def _rows_times(x, vec, nb):
    if nb == 1:
        return x * vec
    tm, d = x.shape
    return (x.reshape(nb, tm // nb, d) * vec[:, None, :]).reshape(tm, d)


def _rows_plus(x, vec, nb):
    if nb == 1:
        return x + vec
    tm, d = x.shape
    return (x.reshape(nb, tm // nb, d) + vec[:, None, :]).reshape(tm, d)


def _rms_rows(x):
    return x * lax.rsqrt(jnp.mean(x * x, axis=-1, keepdims=True) + RMS_EPS)


def _ada_kernel(c_ref, w_ref, b_ref, o_ref):
    o_ref[...] = _dot3(_silu(c_ref[...]), w_ref[...]) + b_ref[...]


def _ada(c_all, w_ada, b_ada):
    rows = c_all.shape[0]
    n = w_ada.shape[1]
    tn = 1536
    return pl.pallas_call(
        _ada_kernel,
        grid=(n // tn,),
        in_specs=[pl.BlockSpec((rows, D_MODEL), lambda j: (0, 0)),
                  pl.BlockSpec((D_MODEL, tn), lambda j: (0, j)),
                  pl.BlockSpec((1, tn), lambda j: (0, j))],
        out_specs=pl.BlockSpec((rows, tn), lambda j: (0, j)),
        out_shape=jax.ShapeDtypeStruct((rows, n), F32),
        compiler_params=pltpu.CompilerParams(dimension_semantics=("arbitrary",), vmem_limit_bytes=VMEM_LIMIT),
        name="ada_mod",
    )(c_all, w_ada, b_ada)


def _rope_lanes(a, tab_ref, reps):
    width = a.shape[-1]
    cos = jnp.tile(tab_ref[0], (1, reps)) if reps > 1 else tab_ref[0]
    sin_lo = jnp.tile(tab_ref[1], (1, reps)) if reps > 1 else tab_ref[1]
    sin_hi = jnp.tile(tab_ref[2], (1, reps)) if reps > 1 else tab_ref[2]
    half = ROT_DIM // 2
    return a * cos + pltpu.roll(a, width - half, 1) * sin_lo + pltpu.roll(a, half, 1) * sin_hi


def _proj_kernel(x_ref, mod_ref, n1g_ref, wa_ref, wv_ref, wi_ref, wg_ref, bd_ref, gqk_ref, tab_ref,
                 qa_ref, ka_ref, kab_ref, va_ref, vab_ref, qb_ref, kb_ref, kbb_ref, vb_ref, vbb_ref,
                 qi_ref, ki2_ref, ki2b_ref, wi_out_ref, gates_ref, *, nb):
    x = x_ref[...]
    sh1 = mod_ref[:, 0, :]
    sc1 = mod_ref[:, 1, :]
    h = _rows_plus(_rows_times(_rms_rows(x) * n1g_ref[...], 1.0 + sc1, nb), sh1, nb)
    hb = h.astype(BF16)

    a = _dot(hb, wa_ref[...])
    ms = _dot((a * a).astype(BF16), bd_ref[...])
    a = a * lax.rsqrt(ms + RMS_EPS) * gqk_ref[...]
    a = _rope_lanes(a, tab_ref, a.shape[-1] // LANES)
    qa_ref[...] = (a[:, :A_WIDTH] * (HEAD_DIM ** -0.5)).astype(BF16)
    ka = a[:, A_WIDTH:]
    ka_ref[...] = ka
    kab_ref[...] = ka.astype(BF16)

    v = _dot(hb, wv_ref[...])
    va = v[:, :A_WIDTH]
    va_ref[...] = va
    vab_ref[...] = va.astype(BF16)
    qb_ref[...] = (v[:, A_WIDTH:A_WIDTH + B_WIDTH] * (HEAD_DIM ** -0.5)).astype(BF16)
    kb = v[:, A_WIDTH + B_WIDTH:A_WIDTH + 2 * B_WIDTH]
    kb_ref[...] = kb
    kbb_ref[...] = kb.astype(BF16)
    vb = v[:, A_WIDTH + 2 * B_WIDTH:]
    vb_ref[...] = vb
    vbb_ref[...] = vb.astype(BF16)

    i = _dot(hb, wi_ref[...])
    nr = IDX_SEG - LANES
    ir = _rope_lanes(i[:, :nr], tab_ref, nr // LANES)
    nq = IDX_HEADS * IDX_DIM
    qi_ref[...] = (ir[:, :nq] * (IDX_DIM ** -0.5)).astype(BF16)
    ki2 = ir[:, nq:]
    ki2_ref[...] = ki2
    ki2b_ref[...] = ki2.astype(BF16)
    wi_out_ref[...] = i[:, nr:]

    gates_ref[...] = _sigmoid(_dot(hb, wg_ref[...]))


def _proj(x2, mod, n1g, wa, wv, wi, wg, bd, gqk, tab, *, t_len, tm):
    n = x2.shape[0]
    nb = max(1, tm // t_len)
    tiles_per_batch = max(1, t_len // tm)
    tab_tiles = tab.shape[1] // tm
    row = lambda i: (i, 0)
    const2 = lambda i: (0, 0)
    tabmap = lambda i: (0, i % tab_tiles, 0)
    in_specs = [
        pl.BlockSpec((tm, D_MODEL), row),
        pl.BlockSpec((nb, 6, D_MODEL), lambda i: (i // tiles_per_batch, 0, 0)),
        pl.BlockSpec((1, D_MODEL), const2),
        pl.BlockSpec(wa.shape, const2),
        pl.BlockSpec(wv.shape, const2),
        pl.BlockSpec(wi.shape, const2),
        pl.BlockSpec(wg.shape, const2),
        pl.BlockSpec(bd.shape, const2),
        pl.BlockSpec((1, 2 * A_WIDTH), const2),
        pl.BlockSpec((3, tm, LANES), tabmap),
    ]
    outs = [
        ("qa", A_WIDTH, BF16), ("ka", A_WIDTH, F32), ("kab", A_WIDTH, BF16),
        ("va", A_WIDTH, F32), ("vab", A_WIDTH, BF16),
        ("qb", B_WIDTH, BF16), ("kb", B_WIDTH, F32), ("kbb", B_WIDTH, BF16),
        ("vb", B_WIDTH, F32), ("vbb", B_WIDTH, BF16),
        ("qi", IDX_HEADS * IDX_DIM, BF16), ("ki2", LANES, F32), ("ki2b", LANES, BF16), ("wi", LANES, F32),
        ("gates", 2 * D_MODEL, F32),
    ]
    res = pl.pallas_call(
        functools.partial(_proj_kernel, nb=nb),
        grid=(n // tm,),
        in_specs=in_specs,
        out_specs=[pl.BlockSpec((tm, w), row) for _, w, _ in outs],
        out_shape=[jax.ShapeDtypeStruct((n, w), dt) for _, w, dt in outs],
        compiler_params=pltpu.CompilerParams(dimension_semantics=("arbitrary",), vmem_limit_bytes=VMEM_LIMIT),
        name="in_proj",
    )(x2, mod, n1g, wa, wv, wi, wg, bd, gqk, tab)
    return dict(zip([o[0] for o in outs], res))


def _lower_tri(n):
    idx = jnp.arange(n)
    return (idx[:, None] >= idx[None, :]).astype(BF16)


def _head_halves(x):
    lane = lax.broadcasted_iota(I32, x.shape, 1)
    zero = jnp.zeros_like(x)
    return jnp.where(lane < HEAD_DIM, x, zero), jnp.where(lane >= HEAD_DIM, x, zero)


def _dsa_kernel(qi_ref, w_ref, qa_ref, ki_ref, ka_ref, va_ref, tri_ref, o_ref,
                s_scr, bias_scr, m_scr, l_scr, acc_scr, *, tq, kb, s_real, past, topk):
    qt = pl.program_id(1)
    q0 = past + qt * tq
    kmax = jnp.minimum((lax.shift_right_logical(q0 + tq - 1, CHUNK_SHIFT) + 1) * CHUNK, s_real)
    nblk = (kmax + kb - 1) // kb
    qchunk = lax.shift_right_logical(q0 + lax.broadcasted_iota(I32, (1, tq), 1), CHUNK_SHIFT)
    kf = float(topk)
    chunks = lambda ref, i: ref[0, :, i * LANES:(i + 1) * LANES]
    qi_heads = [q for i in range(IDX_HEADS // 2) for q in _head_halves(chunks(qi_ref, i))]
    qa_heads = [q for i in range(PAIRS) for q in _head_halves(chunks(qa_ref, i))]
    w_rows = w_ref[0].T

    def fold_rows(x, op):
        part = op(x.reshape(kb // COUNT_ROWS, COUNT_ROWS, tq), axis=0)
        return op(part, axis=0, keepdims=True)

    def score_block(j, masked):
        r0 = pl.multiple_of(j * kb, kb)
        kib = ki_ref[0, pl.ds(r0, kb), :]
        acc = jnp.zeros((kb, tq), F32)
        for h in range(IDX_HEADS):
            rel = jnp.maximum(_dot_nt(kib, qi_heads[h]), 0.0)
            acc = acc + w_rows[h:h + 1, :] * rel
        s = acc + 0.0
        if masked:
            kpos = r0 + lax.broadcasted_iota(I32, (kb, tq), 0)
            adm = (lax.shift_right_logical(kpos, CHUNK_SHIFT) <= qchunk) & (kpos < s_real)
            s = jnp.where(adm, s, -jnp.inf)
        s_scr[pl.ds(r0, kb), :] = s

    n_open = jnp.minimum((lax.shift_right_logical(q0, CHUNK_SHIFT) + 1) * CHUNK, s_real) // kb

    def open_step(j, c):
        score_block(j, False)
        return c

    def masked_step(j, c):
        score_block(j, True)
        return c

    lax.fori_loop(0, n_open, open_step, 0)
    lax.fori_loop(n_open, nblk, masked_step, 0)

    def count(pred):
        def body(j, acc):
            r0 = pl.multiple_of(j * kb, kb)
            hit = jnp.where(pred(s_scr[pl.ds(r0, kb), :]), 1.0, 0.0)
            return acc + jnp.sum(hit.reshape(kb // COUNT_ROWS, COUNT_ROWS, tq), axis=0)
        acc = lax.fori_loop(0, nblk, body, jnp.zeros((COUNT_ROWS, tq), F32))
        return jnp.sum(acc, axis=0, keepdims=True)

    def key_to_score(key):
        return pltpu.bitcast(jnp.where(key < 0, key ^ 0x7FFFFFFF, key), F32)

    c0 = count(lambda blk: blk >= 0.0)
    t0 = jnp.where(c0 >= kf, 0, INT_MIN).astype(I32)

    def value_bit(i, t):
        cand = t | jnp.left_shift(jnp.int32(1), 30 - i)
        cand_f = key_to_score(cand)
        return jnp.where(count(lambda blk: blk >= cand_f) >= kf, cand, t)

    t_key = lax.fori_loop(0, 31, value_bit, t0)
    t = key_to_score(jnp.maximum(t_key, KEY_NEG_INF))

    need = kf - count(lambda blk: blk > t)
    tri = tri_ref[...]

    def select_block(j, seen):
        r0 = pl.multiple_of(j * kb, kb)
        blk = s_scr[pl.ds(r0, kb), :]
        tie = jnp.where((blk == t) & (blk > -jnp.inf), 1.0, 0.0)
        rank = _dot(tri, tie.astype(BF16)) + seen
        chosen = (blk > t) | ((tie > 0.0) & (rank <= need))
        bias_scr[pl.ds(r0, kb), :] = jnp.where(chosen, 0.0, NEG_BIG)
        return seen + fold_rows(tie, jnp.sum)

    lax.fori_loop(0, nblk, select_block, jnp.zeros((1, tq), F32))

    m_scr[...] = jnp.full(m_scr.shape, NEG_BIG, F32)
    l_scr[...] = jnp.zeros(l_scr.shape, F32)
    acc_scr[...] = jnp.zeros(acc_scr.shape, F32)

    def attend_block(j, carry):
        r0 = pl.multiple_of(j * kb, kb)
        bias = bias_scr[pl.ds(r0, kb), :]
        heads = range(A_HEADS)
        m_old = m_scr[...]
        l_old = l_scr[...]
        k_pair = [ka_ref[0, pl.ds(r0, kb), i * LANES:(i + 1) * LANES] for i in range(PAIRS)]
        v_pair = [va_ref[0, pl.ds(r0, kb), i * LANES:(i + 1) * LANES] for i in range(PAIRS)]
        ss = [_dot_nt(k_pair[h // 2], qa_heads[h]) + bias for h in heads]
        m_new = [jnp.maximum(m_old[h:h + 1, :], fold_rows(ss[h], jnp.max)) for h in heads]
        ps = [jnp.exp(ss[h] - m_new[h]) for h in heads]
        pvs = [_dot_tn(v_pair[h // 2], ps[h].astype(BF16)) for h in heads]
        alphas = [jnp.exp(m_old[h:h + 1, :] - m_new[h]) for h in heads]
        for i in range(PAIRS):
            e, o = 2 * i, 2 * i + 1
            acc_scr[i] = (jnp.where(even_rows, alphas[e], alphas[o]) * acc_scr[i]
                          + jnp.where(even_rows, pvs[e], pvs[o]))
        for h in heads:
            l_scr[h:h + 1, :] = alphas[h] * l_old[h:h + 1, :] + fold_rows(ps[h], jnp.sum)
            m_scr[h:h + 1, :] = m_new[h]
        return carry

    even_rows = lax.broadcasted_iota(I32, (LANES, tq), 0) < HEAD_DIM
    lax.fori_loop(0, nblk, attend_block, 0)

    l_fin = l_scr[...]
    for i in range(PAIRS):
        denom = jnp.where(even_rows, l_fin[2 * i:2 * i + 1, :], l_fin[2 * i + 1:2 * i + 2, :])
        o_ref[0, :, i * LANES:(i + 1) * LANES] = (acc_scr[i] / denom).T.astype(o_ref.dtype)


def _dsa(qi, wi, qa, ki2, ka, va, *, tq, kb, s_real, past, topk):
    bsz, t_len, _ = qa.shape
    s_pad = ki2.shape[1]
    kern = functools.partial(_dsa_kernel, tq=tq, kb=kb, s_real=s_real, past=past, topk=topk)
    qmap = lambda b, q: (b, q, 0)
    kmap = lambda b, q: (b, 0, 0)
    return pl.pallas_call(
        kern,
        grid=(bsz, t_len // tq),
        in_specs=[
            pl.BlockSpec((1, tq, IDX_HEADS * IDX_DIM), qmap),
            pl.BlockSpec((1, tq, LANES), qmap),
            pl.BlockSpec((1, tq, A_WIDTH), qmap),
            pl.BlockSpec((1, s_pad, LANES), kmap),
            pl.BlockSpec((1, s_pad, A_WIDTH), kmap),
            pl.BlockSpec((1, s_pad, A_WIDTH), kmap),
            pl.BlockSpec((kb, kb), lambda b, q: (0, 0)),
        ],
        out_specs=pl.BlockSpec((1, tq, A_WIDTH), qmap),
        out_shape=jax.ShapeDtypeStruct((bsz, t_len, A_WIDTH), BF16),
        scratch_shapes=[
            pltpu.VMEM((s_pad, tq), F32),
            pltpu.VMEM((s_pad, tq), F32),
            pltpu.VMEM((A_HEADS, tq), F32),
            pltpu.VMEM((A_HEADS, tq), F32),
            pltpu.VMEM((PAIRS, LANES, tq), F32),
        ],
        compiler_params=pltpu.CompilerParams(dimension_semantics=("arbitrary", "arbitrary"),
                                             vmem_limit_bytes=VMEM_LIMIT),
        name="dsa_attention",
    )(qi, wi, qa, ki2, ka, va, _lower_tri(kb))


def _stick_kernel(q_ref, k_ref, v_ref, tri_ref, o_ref, carry_scr, acc_scr, *, tq, kb, past):
    qt = pl.program_id(1)
    q0 = past + qt * tq
    nblk = (q0 + tq - 1 + kb - 1) // kb
    n_full = q0 // kb
    qpos = q0 + lax.broadcasted_iota(I32, (tq, kb), 0)
    carry_scr[...] = jnp.zeros(carry_scr.shape, F32)
    acc_scr[...] = jnp.zeros(acc_scr.shape, F32)
    tri2 = tri_ref[...]
    heads = range(B_HEADS)
    q_heads = [q for i in range(PAIRS) for q in _head_halves(q_ref[0, :, i * LANES:(i + 1) * LANES])]
    even_lanes = lax.broadcasted_iota(I32, (tq, LANES), 1) < HEAD_DIM

    def block(j, masked):
        r0 = pl.multiple_of(j * kb, kb)
        causal = ((r0 + lax.broadcasted_iota(I32, (tq, kb), 1)) < qpos) if masked else None
        carries = [carry_scr[h] for h in heads]
        k_pair = [k_ref[0, pl.ds(r0, kb), i * LANES:(i + 1) * LANES] for i in range(PAIRS)]
        v_pair = [v_ref[0, pl.ds(r0, kb), i * LANES:(i + 1) * LANES] for i in range(PAIRS)]
        zs = [_dot_nt(q_heads[h], k_pair[h // 2]) for h in heads]
        sps = []
        for z in zs:
            sp = jnp.maximum(z, 0.0) + jnp.log(1.0 + jnp.exp(-jnp.abs(z)))
            sps.append(jnp.where(causal, sp, 0.0) if masked else sp)
        sufs = []
        for h in heads:
            hi, lo = _split_bf16(sps[h])
            hilo = jnp.concatenate([hi, lo], axis=1)
            sufs.append(_dot(hilo, tri2) + jnp.tile(carries[h], (1, kb // LANES)))
        pvs = []
        for h in heads:
            a = jnp.exp(zs[h] - sufs[h])
            if masked:
                a = jnp.where(causal, a, 0.0)
            pvs.append(_dot(a.astype(BF16), v_pair[h // 2]))
        for i in range(PAIRS):
            acc_scr[i] = acc_scr[i] + jnp.where(even_lanes, pvs[2 * i], pvs[2 * i + 1])
        for h in heads:
            carry_scr[h] = carries[h] + jnp.sum(sps[h], axis=-1, keepdims=True)

    def masked_step(jj, c):
        block(nblk - 1 - jj, True)
        return c

    def full_step(jj, c):
        block(n_full - 1 - jj, False)
        return c

    lax.fori_loop(0, nblk - n_full, masked_step, 0)
    lax.fori_loop(0, n_full, full_step, 0)
    for i in range(PAIRS):
        o_ref[0, :, i * LANES:(i + 1) * LANES] = acc_scr[i].astype(o_ref.dtype)


def _stick(q, k, v, *, tq, kb, past):
    bsz, t_len, _ = q.shape
    s_pad = k.shape[1]
    kern = functools.partial(_stick_kernel, tq=tq, kb=kb, past=past)
    kmap = lambda b, q: (b, 0, 0)
    return pl.pallas_call(
        kern,
        grid=(bsz, t_len // tq),
        in_specs=[
            pl.BlockSpec((1, tq, B_WIDTH), lambda b, q: (b, q, 0)),
            pl.BlockSpec((1, s_pad, B_WIDTH), kmap),
            pl.BlockSpec((1, s_pad, B_WIDTH), kmap),
            pl.BlockSpec((2 * kb, kb), lambda b, q: (0, 0)),
        ],
        out_specs=pl.BlockSpec((1, tq, B_WIDTH), lambda b, q: (b, q, 0)),
        out_shape=jax.ShapeDtypeStruct((bsz, t_len, B_WIDTH), BF16),
        scratch_shapes=[
            pltpu.VMEM((B_HEADS, tq, LANES), F32),
            pltpu.VMEM((PAIRS, tq, LANES), F32),
        ],
        compiler_params=pltpu.CompilerParams(dimension_semantics=("arbitrary", "arbitrary"),
                                             vmem_limit_bytes=VMEM_LIMIT),
        name="stick_attention",
    )(q, k, v, jnp.tile(_lower_tri(kb), (2, 1)))


def _merge_kernel(x_ref, oa_ref, ob_ref, gates_ref, mod_ref, wua_ref, wub_ref, wout_ref, n2g_ref, wr_ref, br_ref,
                  y_ref, h2_ref, comb_ref, *, nb):
    g1 = mod_ref[:, 2, :]
    sh2 = mod_ref[:, 3, :]
    sc2 = mod_ref[:, 4, :]
    gates = gates_ref[...]
    merged = gates[:, :D_MODEL] * _dot(oa_ref[...], wua_ref[...]) + gates[:, D_MODEL:] * _dot(ob_ref[...], wub_ref[...])
    y = x_ref[...] + _rows_times(_dot(merged.astype(BF16), wout_ref[...]), g1, nb)
    y_ref[...] = y
    h2 = _rows_plus(_rows_times(_rms_rows(y) * n2g_ref[...], 1.0 + sc2, nb), sh2, nb)
    h2_ref[...] = h2.astype(BF16)

    logits = _dot3(h2, wr_ref[...]) + br_ref[...]
    lane = lax.broadcasted_iota(I32, logits.shape, 1).astype(F32)
    far = float(ROUTER_LANES)
    is_g = lane < float(N_GROUPS)
    gl = jnp.where(is_g, logits, -jnp.inf)
    gmax = jnp.max(gl, axis=-1, keepdims=True)
    gsel = jnp.min(jnp.where(is_g & (gl == gmax), lane, far), axis=-1, keepdims=True)
    p_group = 1.0 / jnp.sum(jnp.where(is_g, jnp.exp(gl - gmax), 0.0), axis=-1, keepdims=True)
    e_lo = float(EXPERT_LANE0) + gsel * float(EXPERTS_PER_GROUP)
    in_grp = (lane >= e_lo) & (lane < e_lo + float(EXPERTS_PER_GROUP))
    el = jnp.where(in_grp, logits, -jnp.inf)
    v1 = jnp.max(el, axis=-1, keepdims=True)
    i1 = jnp.min(jnp.where(in_grp & (el == v1), lane, far), axis=-1, keepdims=True)
    el2 = jnp.where(lane == i1, -jnp.inf, el)
    v2 = jnp.max(el2, axis=-1, keepdims=True)
    i2 = jnp.min(jnp.where(in_grp & (lane != i1) & (el2 == v2), lane, far), axis=-1, keepdims=True)
    e2 = jnp.exp(v2 - v1)
    w1 = p_group / (1.0 + e2)
    w2 = p_group * e2 / (1.0 + e2)
    comb_ref[...] = jnp.where(lane == i1, w1, 0.0) + jnp.where(lane == i2, w2, 0.0)


def _merge(x2, oa, ob, gates, mod, wua, wub, wout, n2g, wr, br, *, t_len, tm):
    n = x2.shape[0]
    nb = max(1, tm // t_len)
    tiles_per_batch = max(1, t_len // tm)
    row = lambda i: (i, 0)
    const2 = lambda i: (0, 0)
    return pl.pallas_call(
        functools.partial(_merge_kernel, nb=nb),
        grid=(n // tm,),
        in_specs=[
            pl.BlockSpec((tm, D_MODEL), row),
            pl.BlockSpec((tm, A_WIDTH), row),
            pl.BlockSpec((tm, B_WIDTH), row),
            pl.BlockSpec((tm, 2 * D_MODEL), row),
            pl.BlockSpec((nb, 6, D_MODEL), lambda i: (i // tiles_per_batch, 0, 0)),
            pl.BlockSpec(wua.shape, const2),
            pl.BlockSpec(wub.shape, const2),
            pl.BlockSpec(wout.shape, const2),
            pl.BlockSpec((1, D_MODEL), const2),
            pl.BlockSpec(wr.shape, const2),
            pl.BlockSpec((1, ROUTER_LANES), const2),
        ],
        out_specs=[pl.BlockSpec((tm, D_MODEL), row), pl.BlockSpec((tm, D_MODEL), row),
                   pl.BlockSpec((tm, ROUTER_LANES), row)],
        out_shape=[jax.ShapeDtypeStruct((n, D_MODEL), F32), jax.ShapeDtypeStruct((n, D_MODEL), BF16),
                   jax.ShapeDtypeStruct((n, ROUTER_LANES), F32)],
        compiler_params=pltpu.CompilerParams(dimension_semantics=("arbitrary",), vmem_limit_bytes=VMEM_LIMIT),
        name="merge_router",
    )(x2, oa, ob, gates, mod, wua, wub, wout, n2g, wr, br)


def _moe_kernel(y_ref, h2_ref, comb_ref, mod_ref, tri_ref, wg_ref, wu_ref, wd_ref, o_ref,
                rank_scr, hc_scr, wc_scr, yc_scr, cnt_smem, *, nb, ch):
    g = pl.program_id(1)
    e = pl.program_id(2)
    tm = h2_ref.shape[0]
    lane = lax.broadcasted_iota(I32, (tm, ROUTER_LANES), 1)
    lane0 = EXPERT_LANE0 + g * EXPERTS_PER_GROUP

    @pl.when((g == 0) & (e == 0))
    def _():
        o_ref[...] = jnp.zeros(o_ref.shape, F32)

    @pl.when(e == 0)
    def _():
        comb = comb_ref[...]
        in_g = (lane >= lane0) & (lane < lane0 + EXPERTS_PER_GROUP)
        member = jnp.sum(jnp.where(in_g, comb, 0.0), axis=-1, keepdims=True) > 0.0
        mf = jnp.where(member, 1.0, 0.0)
        running = _dot(tri_ref[...], jnp.broadcast_to(mf, (tm, LANES)).astype(BF16))
        rank = jnp.where(member, running - 1.0, -1.0)
        rank_scr[...] = rank
        n_tok = jnp.sum(mf).astype(I32)
        cnt_smem[0] = n_tok
        rank_row = rank.T[0:1, :]
        h2 = h2_ref[...]
        c_hi, c_lo = _split_bf16(comb)

        def gather(c, carry):
            r0 = pl.multiple_of(c * ch, ch)
            slot = (r0 + lax.broadcasted_iota(I32, (ch, tm), 0)).astype(F32)
            pick = jnp.where(rank_row == slot, 1.0, 0.0).astype(BF16)
            hc_scr[pl.ds(r0, ch), :] = _dot(pick, h2).astype(BF16)
            wc_scr[pl.ds(r0, ch), :] = _dot(pick, c_hi) + _dot(pick, c_lo)
            yc_scr[pl.ds(r0, ch), :] = jnp.zeros((ch, D_MODEL), F32)
            return carry

        lax.fori_loop(0, (n_tok + ch - 1) // ch, gather, 0)

    n_chunks = (cnt_smem[0] + ch - 1) // ch
    w_gate = wg_ref[0].astype(BF16)
    w_up = wu_ref[0].astype(BF16)
    w_down = wd_ref[0].astype(BF16)
    lane_c = lax.broadcasted_iota(I32, (ch, ROUTER_LANES), 1)

    def expert(c, carry):
        r0 = pl.multiple_of(c * ch, ch)
        hc = hc_scr[pl.ds(r0, ch), :]
        act = _silu(_dot(hc, w_gate)) * _dot(hc, w_up)
        contrib = _dot(act.astype(BF16), w_down)
        w_e = jnp.sum(jnp.where(lane_c == lane0 + e, wc_scr[pl.ds(r0, ch), :], 0.0), axis=-1, keepdims=True)
        yc_scr[pl.ds(r0, ch), :] += w_e * contrib
        return carry

    lax.fori_loop(0, n_chunks, expert, 0)

    @pl.when(e == EXPERTS_PER_GROUP - 1)
    def _():
        rank = jnp.tile(rank_scr[...], (1, ch // LANES))

        def scatter(c, carry):
            r0 = pl.multiple_of(c * ch, ch)
            slot = (r0 + lax.broadcasted_iota(I32, (tm, ch), 1)).astype(F32)
            place = jnp.where(rank == slot, 1.0, 0.0).astype(BF16)
            y_hi, y_lo = _split_bf16(yc_scr[pl.ds(r0, ch), :])
            o_ref[...] += _dot(place, y_hi) + _dot(place, y_lo)
            return carry

        lax.fori_loop(0, n_chunks, scatter, 0)

    @pl.when((g == N_GROUPS - 1) & (e == EXPERTS_PER_GROUP - 1))
    def _():
        o_ref[...] = y_ref[...] + _rows_times(o_ref[...], mod_ref[:, 5, :], nb)


def _moe(y, h2, comb, mod, w_gate, w_up, w_down, *, t_len, tm, ch):
    n = y.shape[0]
    nb = max(1, tm // t_len)
    tiles_per_batch = max(1, t_len // tm)
    row = lambda i, g, e: (i, 0)
    expert = lambda i, g, e: (g * EXPERTS_PER_GROUP + e, 0, 0)
    return pl.pallas_call(
        functools.partial(_moe_kernel, nb=nb, ch=ch),
        grid=(n // tm, N_GROUPS, EXPERTS_PER_GROUP),
        in_specs=[
            pl.BlockSpec((tm, D_MODEL), row),
            pl.BlockSpec((tm, D_MODEL), row),
            pl.BlockSpec((tm, ROUTER_LANES), row),
            pl.BlockSpec((nb, 6, D_MODEL), lambda i, g, e: (i // tiles_per_batch, 0, 0)),
            pl.BlockSpec((tm, tm), lambda i, g, e: (0, 0)),
            pl.BlockSpec((1, D_MODEL, D_EXPERT), expert),
            pl.BlockSpec((1, D_MODEL, D_EXPERT), expert),
            pl.BlockSpec((1, D_EXPERT, D_MODEL), expert),
        ],
        out_specs=pl.BlockSpec((tm, D_MODEL), row),
        out_shape=jax.ShapeDtypeStruct((n, D_MODEL), F32),
        scratch_shapes=[
            pltpu.VMEM((tm, LANES), F32),
            pltpu.VMEM((tm, D_MODEL), BF16),
            pltpu.VMEM((tm, ROUTER_LANES), F32),
            pltpu.VMEM((tm, D_MODEL), F32),
            pltpu.SMEM((1,), I32),
        ],
        compiler_params=pltpu.CompilerParams(dimension_semantics=("arbitrary", "arbitrary", "arbitrary"),
                                             vmem_limit_bytes=VMEM_LIMIT),
        name="moe_grouped",
    )(y, h2, comb, mod, _lower_tri(tm), w_gate, w_up, w_down)


def _rope_tables(pos):
    half = ROT_DIM // 2
    freqs = ROPE_THETA ** (-jnp.arange(0, ROT_DIM, 2, dtype=F32) / ROT_DIM)
    ang = pos.astype(F32)[:, None] * freqs[None, :]
    cos, sin = jnp.cos(ang), jnp.sin(ang)
    dh = jnp.arange(LANES) % HEAD_DIM
    fi = dh % half
    cos_t = jnp.where(dh < ROT_DIM, cos[:, fi], 1.0)
    sin_lo = jnp.where(dh < half, -sin[:, fi], 0.0)
    sin_hi = jnp.where((dh >= half) & (dh < ROT_DIM), sin[:, fi], 0.0)
    return jnp.stack([cos_t, sin_lo, sin_hi]).astype(F32)


def _round_up(a, b):
    return (a + b - 1) // b * b


def _layer(x, mod, past_ak, past_av, past_aki, past_bk, past_bv, w, *, cfg):
    bsz, t_len, _ = x.shape
    past = past_ak.shape[1]
    s_real = past + t_len
    n = bsz * t_len
    x2 = x.reshape(n, D_MODEL)
    pos = past + jnp.arange(t_len, dtype=jnp.int32)

    tm = cfg["tm"]
    tab = jnp.tile(_rope_tables(pos), (1, max(1, tm // t_len), 1))
    pr = _proj(x2, mod, w["n1g"], w["wa"], w["wv"], w["wi"], w["wg"], w["bd"], w["gqk"], tab, t_len=t_len, tm=tm)

    new_ak = pr["ka"].reshape(bsz, t_len, A_HEADS, HEAD_DIM)
    new_av = pr["va"].reshape(bsz, t_len, A_HEADS, HEAD_DIM)
    new_aki = pr["ki2"][:, :IDX_DIM].reshape(bsz, t_len, IDX_DIM)
    new_bk = pr["kb"].reshape(bsz, t_len, B_HEADS, HEAD_DIM)
    new_bv = pr["vb"].reshape(bsz, t_len, B_HEADS, HEAD_DIM)

    def all_keys(past_x, new_bf16, s_pad):
        width = new_bf16.shape[-1]
        parts = [new_bf16.reshape(bsz, t_len, width)]
        if past:
            parts.insert(0, past_x.astype(BF16))
        if s_pad > s_real:
            parts.append(jnp.zeros((bsz, s_pad - s_real, width), BF16))
        return parts[0] if len(parts) == 1 else jnp.concatenate(parts, axis=1)

    tq, kb = cfg["dsa_tq"], cfg["dsa_kb"]
    s_pad = _round_up(s_real, kb)
    topk = min(TOPK_MAX, s_real // 4)
    past_ki2 = jnp.concatenate([past_aki, past_aki], axis=-1) if past else None
    oa = _dsa(pr["qi"].reshape(bsz, t_len, -1), pr["wi"].reshape(bsz, t_len, LANES), pr["qa"].reshape(bsz, t_len, -1),
              all_keys(past_ki2, pr["ki2b"], s_pad),
              all_keys(past_ak.reshape(bsz, past, A_WIDTH), pr["kab"], s_pad),
              all_keys(past_av.reshape(bsz, past, A_WIDTH), pr["vab"], s_pad),
              tq=tq, kb=kb, s_real=s_real, past=past, topk=topk)

    tqb, kbb = cfg["sb_tq"], cfg["sb_kb"]
    s_pad_b = _round_up(s_real, kbb)
    ob = _stick(pr["qb"].reshape(bsz, t_len, -1),
                all_keys(past_bk.reshape(bsz, past, B_WIDTH), pr["kbb"], s_pad_b),
                all_keys(past_bv.reshape(bsz, past, B_WIDTH), pr["vbb"], s_pad_b),
                tq=tqb, kb=kbb, past=past)

    y, h2, comb = _merge(x2, oa.reshape(n, A_WIDTH), ob.reshape(n, B_WIDTH), pr["gates"], mod,
                         w["wua"], w["wub"], w["wout"], w["n2g"], w["wr"], w["br"], t_len=t_len, tm=cfg["tm_merge"])
    out = _moe(y, h2, comb, mod, w["w_gate"], w["w_up"], w["w_down"], t_len=t_len, tm=cfg["tm_moe"],
               ch=cfg["moe_ch"])
    return out.reshape(bsz, t_len, D_MODEL), (new_ak, new_av, new_aki, new_bk, new_bv)


def _prep_weights(w_in, norm1_g, qnorm_g, knorm_g, w_up_a, w_up_b, w_out, norm2_g, w_rg, b_rg, w_re, b_re,
                  w_e_gate, w_e_up, w_e_down):
    o = 0
    seg = {}
    for name, width in (("qa", A_WIDTH), ("ka", A_WIDTH), ("va", A_WIDTH), ("qi", IDX_HEADS * IDX_DIM),
                        ("ki", IDX_DIM), ("wi", IDX_HEADS), ("qb", B_WIDTH), ("kb", B_WIDTH), ("vb", B_WIDTH),
                        ("gates", 2 * D_MODEL)):
        seg[name] = w_in[:, o:o + width]
        o += width
    pad_i = jnp.zeros((D_MODEL, LANES - IDX_HEADS), w_in.dtype)
    hd = jnp.arange(2 * A_WIDTH) // HEAD_DIM
    n_r = N_GROUPS + N_EXPERTS
    return dict(
        n1g=norm1_g.reshape(1, D_MODEL),
        wa=jnp.concatenate([seg["qa"], seg["ka"]], axis=1).astype(BF16),
        wv=jnp.concatenate([seg["va"], seg["qb"], seg["kb"], seg["vb"]], axis=1).astype(BF16),
        wi=jnp.concatenate([seg["qi"], seg["ki"], seg["ki"], seg["wi"], pad_i], axis=1).astype(BF16),
        wg=seg["gates"].astype(BF16),
        bd=((hd[:, None] == hd[None, :]).astype(F32) / HEAD_DIM).astype(BF16),
        gqk=jnp.concatenate([jnp.tile(qnorm_g, A_HEADS), jnp.tile(knorm_g, A_HEADS)]).reshape(1, 2 * A_WIDTH),
        wua=w_up_a.astype(BF16), wub=w_up_b.astype(BF16), wout=w_out.astype(BF16),
        n2g=norm2_g.reshape(1, D_MODEL),
        wr=jnp.pad(jnp.concatenate([w_rg, w_re], axis=1), ((0, 0), (0, ROUTER_LANES - n_r))),
        br=jnp.pad(jnp.concatenate([b_rg, b_re]), (0, ROUTER_LANES - n_r)).reshape(1, ROUTER_LANES),
        w_gate=w_e_gate, w_up=w_e_up, w_down=w_e_down,
    )


def _group_cfg(t_len):
    if t_len >= 512:
        return dict(tm=256, tm_merge=256, tm_moe=1024, dsa_tq=256, dsa_kb=512, sb_tq=256, sb_kb=256, moe_ch=256)
    return dict(tm=512, tm_merge=512, tm_moe=512, dsa_tq=t_len, dsa_kb=384, sb_tq=t_len, sb_kb=256, moe_ch=128)


def kernel(x_prompt, x_sample, cache_a_k, cache_a_v, cache_a_kidx, cache_b_k, cache_b_v, c_prompt, c_sample,
           w_ada, b_ada, norm1_g, w_in, qnorm_g, knorm_g, w_up_a, w_up_b, w_out, norm2_g,
           w_rg, b_rg, w_re, b_re, w_e_gate, w_e_up, w_e_down):
    depth = w_ada.shape[0]
    bp, bs = x_prompt.shape[0], x_sample.shape[0]
    dt = x_prompt.dtype
    rows = _round_up(bp + bs, 8)
    empty_a = jnp.zeros((bp, 0, A_HEADS, HEAD_DIM), dt)
    empty_i = jnp.zeros((bp, 0, IDX_DIM), dt)
    empty_b = jnp.zeros((bp, 0, B_HEADS, HEAD_DIM), dt)
    y_p, y_s = x_prompt, x_sample
    rows_p, rows_s = [], []
    c_all = jnp.concatenate([c_prompt, c_sample, jnp.zeros((rows - bp - bs, D_MODEL), dt)], axis=0)
    for l in range(depth):
        w = _prep_weights(w_in[l], norm1_g[l], qnorm_g[l], knorm_g[l], w_up_a[l], w_up_b[l], w_out[l], norm2_g[l],
                          w_rg[l], b_rg[l], w_re[l], b_re[l], w_e_gate[l], w_e_up[l], w_e_down[l])
        mod = _ada(c_all, w_ada[l], b_ada[l].reshape(1, -1))
        mod_p = mod[:bp].reshape(bp, 6, D_MODEL)
        mod_s = mod[bp:bp + bs].reshape(bs, 6, D_MODEL)
        y_p, new_p = _layer(y_p, mod_p, empty_a, empty_a, empty_i, empty_b, empty_b, w,
                            cfg=_group_cfg(y_p.shape[1]))
        y_s, new_s = _layer(y_s, mod_s, cache_a_k[l], cache_a_v[l], cache_a_kidx[l], cache_b_k[l], cache_b_v[l], w,
                            cfg=_group_cfg(y_s.shape[1]))
        rows_p.append(new_p)
        rows_s.append(new_s)
    stack = lambda rws, i: jnp.stack([r[i] for r in rws])
    return (y_p, y_s) + tuple(stack(rows_p, i) for i in range(5)) + tuple(stack(rows_s, i) for i in range(5))
```

```python
import functools

import jax
import jax.numpy as jnp
from jax import lax
from jax.experimental import pallas as pl
from jax.experimental.pallas import tpu as pltpu

D_MODEL = 1024
CHUNK = 64
CHUNK_SHIFT = 6
HEAD_DIM = 64
A_HEADS = 8
B_HEADS = 8
IDX_HEADS = 4
IDX_DIM = 64
TOPK_MAX = 256
ROPE_THETA = 500000.0
ROT_DIM = HEAD_DIM // 4
N_GROUPS = 4
EXPERTS_PER_GROUP = 8
N_EXPERTS = N_GROUPS * EXPERTS_PER_GROUP
D_EXPERT = 256
RMS_EPS = 1e-6
A_WIDTH = A_HEADS * HEAD_DIM
B_WIDTH = B_HEADS * HEAD_DIM

LANES = 128
IDX_SEG = 512
PAIRS = 4
COUNT_ROWS = 64
ROUTER_LANES = 128
EXPERT_LANE0 = N_GROUPS
VMEM_LIMIT = 56 * 1024 * 1024

F32 = jnp.float32
BF16 = jnp.bfloat16
I32 = jnp.int32
NEG_BIG = -1e30
KEY_NEG_INF = (0xFF800000 - (1 << 32)) ^ 0x7FFFFFFF
KEY_END = 0x7F800001
VALUE_PASSES = 40


def _dot(a, b):
    return jnp.dot(a, b, preferred_element_type=F32)


def _dot_nt(a, b):
    return lax.dot_general(a, b, (((1,), (1,)), ((), ())), preferred_element_type=F32)


def _dot_tn(a, b):
    return lax.dot_general(a, b, (((0,), (0,)), ((), ())), preferred_element_type=F32)


def _split_bf16(a):
    hi = a.astype(BF16)
    lo = (a - hi.astype(F32)).astype(BF16)
    return hi, lo


def _dot3(a, b):
    ah, al = _split_bf16(a)
    bh, bl = _split_bf16(b)
    return _dot(ah, bh) + (_dot(ah, bl) + _dot(al, bh))


def _silu(x):
    return x * (1.0 / (1.0 + jnp.exp(-x)))


def _sigmoid(x):
    return 1.0 / (1.0 + jnp.exp(-x))


def _rows_times(x, vec, nb):
    if nb == 1:
        return x * vec
    tm, d = x.shape
    return (x.reshape(nb, tm // nb, d) * vec[:, None, :]).reshape(tm, d)


def _rows_plus(x, vec, nb):
    if nb == 1:
        return x + vec
    tm, d = x.shape
    return (x.reshape(nb, tm // nb, d) + vec[:, None, :]).reshape(tm, d)


def _rms_rows(x):
    return x * lax.rsqrt(jnp.mean(x * x, axis=-1, keepdims=True) + RMS_EPS)


def _ada_kernel(c_ref, w_ref, b_ref, o_ref):
    o_ref[...] = _dot3(_silu(c_ref[...]), w_ref[...]) + b_ref[...]


def _ada(c_all, w_ada, b_ada):
    rows = c_all.shape[0]
    n = w_ada.shape[1]
    tn = 1536
    return pl.pallas_call(
        _ada_kernel,
        grid=(n // tn,),
        in_specs=[pl.BlockSpec((rows, D_MODEL), lambda j: (0, 0)),
                  pl.BlockSpec((D_MODEL, tn), lambda j: (0, j)),
                  pl.BlockSpec((1, tn), lambda j: (0, j))],
        out_specs=pl.BlockSpec((rows, tn), lambda j: (0, j)),
        out_shape=jax.ShapeDtypeStruct((rows, n), F32),
        compiler_params=pltpu.CompilerParams(dimension_semantics=("arbitrary",), vmem_limit_bytes=VMEM_LIMIT),
        name="ada_mod",
    )(c_all, w_ada, b_ada)


def _rope_lanes(a, tab_ref, reps):
    width = a.shape[-1]
    cos = jnp.tile(tab_ref[0], (1, reps)) if reps > 1 else tab_ref[0]
    sin_lo = jnp.tile(tab_ref[1], (1, reps)) if reps > 1 else tab_ref[1]
    sin_hi = jnp.tile(tab_ref[2], (1, reps)) if reps > 1 else tab_ref[2]
    half = ROT_DIM // 2
    return a * cos + pltpu.roll(a, width - half, 1) * sin_lo + pltpu.roll(a, half, 1) * sin_hi


def _proj_kernel(x_ref, mod_ref, n1g_ref, wa_ref, wv_ref, wi_ref, wg_ref, bd_ref, gqk_ref, tab_ref,
                 qa_ref, ka_ref, kab_ref, va_ref, vab_ref, qb_ref, kb_ref, kbb_ref, vb_ref, vbb_ref,
                 qi_ref, ki2_ref, ki2b_ref, wi_out_ref, gates_ref, *, nb):
    x = x_ref[...]
    sh1 = mod_ref[:, 0, :]
    sc1 = mod_ref[:, 1, :]
    h = _rows_plus(_rows_times(_rms_rows(x) * n1g_ref[...], 1.0 + sc1, nb), sh1, nb)
    hb = h.astype(BF16)

    a = _dot(hb, wa_ref[...])
    ms = _dot((a * a).astype(BF16), bd_ref[...])
    a = a * lax.rsqrt(ms + RMS_EPS) * gqk_ref[...]
    a = _rope_lanes(a, tab_ref, a.shape[-1] // LANES)
    qa_ref[...] = (a[:, :A_WIDTH] * (HEAD_DIM ** -0.5)).astype(BF16)
    ka = a[:, A_WIDTH:]
    ka_ref[...] = ka
    kab_ref[...] = ka.astype(BF16)

    v = _dot(hb, wv_ref[...])
    va = v[:, :A_WIDTH]
    va_ref[...] = va
    vab_ref[...] = va.astype(BF16)
    qb_ref[...] = (v[:, A_WIDTH:A_WIDTH + B_WIDTH] * (HEAD_DIM ** -0.5)).astype(BF16)
    kb = v[:, A_WIDTH + B_WIDTH:A_WIDTH + 2 * B_WIDTH]
    kb_ref[...] = kb
    kbb_ref[...] = kb.astype(BF16)
    vb = v[:, A_WIDTH + 2 * B_WIDTH:]
    vb_ref[...] = vb
    vbb_ref[...] = vb.astype(BF16)

    i = _dot(hb, wi_ref[...])
    nr = IDX_SEG - LANES
    ir = _rope_lanes(i[:, :nr], tab_ref, nr // LANES)
    nq = IDX_HEADS * IDX_DIM
    qi_ref[...] = (ir[:, :nq] * (IDX_DIM ** -0.5)).astype(BF16)
    ki2 = ir[:, nq:]
    ki2_ref[...] = ki2
    ki2b_ref[...] = ki2.astype(BF16)
    wi_out_ref[...] = i[:, nr:]

    gates_ref[...] = _sigmoid(_dot(hb, wg_ref[...]))


def _proj(x2, mod, n1g, wa, wv, wi, wg, bd, gqk, tab, *, t_len, tm):
    n = x2.shape[0]
    nb = max(1, tm // t_len)
    tiles_per_batch = max(1, t_len // tm)
    tab_tiles = tab.shape[1] // tm
    row = lambda i: (i, 0)
    const2 = lambda i: (0, 0)
    tabmap = lambda i: (0, i % tab_tiles, 0)
    in_specs = [
        pl.BlockSpec((tm, D_MODEL), row),
        pl.BlockSpec((nb, 6, D_MODEL), lambda i: (i // tiles_per_batch, 0, 0)),
        pl.BlockSpec((1, D_MODEL), const2),
        pl.BlockSpec(wa.shape, const2),
        pl.BlockSpec(wv.shape, const2),
        pl.BlockSpec(wi.shape, const2),
        pl.BlockSpec(wg.shape, const2),
        pl.BlockSpec(bd.shape, const2),
        pl.BlockSpec((1, 2 * A_WIDTH), const2),
        pl.BlockSpec((3, tm, LANES), tabmap),
    ]
    outs = [
        ("qa", A_WIDTH, BF16), ("ka", A_WIDTH, F32), ("kab", A_WIDTH, BF16),
        ("va", A_WIDTH, F32), ("vab", A_WIDTH, BF16),
        ("qb", B_WIDTH, BF16), ("kb", B_WIDTH, F32), ("kbb", B_WIDTH, BF16),
        ("vb", B_WIDTH, F32), ("vbb", B_WIDTH, BF16),
        ("qi", IDX_HEADS * IDX_DIM, BF16), ("ki2", LANES, F32), ("ki2b", LANES, BF16), ("wi", LANES, F32),
        ("gates", 2 * D_MODEL, F32),
    ]
    res = pl.pallas_call(
        functools.partial(_proj_kernel, nb=nb),
        grid=(n // tm,),
        in_specs=in_specs,
        out_specs=[pl.BlockSpec((tm, w), row) for _, w, _ in outs],
        out_shape=[jax.ShapeDtypeStruct((n, w), dt) for _, w, dt in outs],
        compiler_params=pltpu.CompilerParams(dimension_semantics=("arbitrary",), vmem_limit_bytes=VMEM_LIMIT),
        name="in_proj",
    )(x2, mod, n1g, wa, wv, wi, wg, bd, gqk, tab)
    return dict(zip([o[0] for o in outs], res))


def _lower_tri(n):
    idx = jnp.arange(n)
    return (idx[:, None] >= idx[None, :]).astype(BF16)


def _head_halves(x):
    lane = lax.broadcasted_iota(I32, x.shape, 1)
    zero = jnp.zeros_like(x)
    return jnp.where(lane < HEAD_DIM, x, zero), jnp.where(lane >= HEAD_DIM, x, zero)


def _dsa_kernel(qi_ref, w_ref, qa_ref, ki_ref, ka_ref, va_ref, tri_ref, o_ref,
                s_scr, bias_scr, m_scr, l_scr, acc_scr, *, tq, kb, s_real, past, topk):
    qt = pl.program_id(1)
    q0 = past + qt * tq
    kmax = jnp.minimum((lax.shift_right_logical(q0 + tq - 1, CHUNK_SHIFT) + 1) * CHUNK, s_real)
    nblk = (kmax + kb - 1) // kb
    qchunk = lax.shift_right_logical(q0 + lax.broadcasted_iota(I32, (1, tq), 1), CHUNK_SHIFT)
    kf = float(topk)
    chunks = lambda ref, i: ref[0, :, i * LANES:(i + 1) * LANES]
    qi_heads = [q for i in range(IDX_HEADS // 2) for q in _head_halves(chunks(qi_ref, i))]
    qa_heads = [q for i in range(PAIRS) for q in _head_halves(chunks(qa_ref, i))]
    w_rows = w_ref[0].T

    def fold_rows(x, op):
        part = op(x.reshape(kb // COUNT_ROWS, COUNT_ROWS, tq), axis=0)
        return op(part, axis=0, keepdims=True)

    def score_block(j, masked):
        r0 = pl.multiple_of(j * kb, kb)
        kib = ki_ref[0, pl.ds(r0, kb), :]
        acc = jnp.zeros((kb, tq), F32)
        for h in range(IDX_HEADS):
            rel = jnp.maximum(_dot_nt(kib, qi_heads[h]), 0.0)
            acc = acc + w_rows[h:h + 1, :] * rel
        s = acc + 0.0
        if masked:
            kpos = r0 + lax.broadcasted_iota(I32, (kb, tq), 0)
            adm = (lax.shift_right_logical(kpos, CHUNK_SHIFT) <= qchunk) & (kpos < s_real)
            s = jnp.where(adm, s, -jnp.inf)
        s_scr[pl.ds(r0, kb), :] = s

    n_open = jnp.minimum((lax.shift_right_logical(q0, CHUNK_SHIFT) + 1) * CHUNK, s_real) // kb

    def open_step(j, c):
        score_block(j, False)
        return c

    def masked_step(j, c):
        score_block(j, True)
        return c

    lax.fori_loop(0, n_open, open_step, 0)
    lax.fori_loop(n_open, nblk, masked_step, 0)

    def count(pred):
        def body(j, acc):
            r0 = pl.multiple_of(j * kb, kb)
            hit = jnp.where(pred(s_scr[pl.ds(r0, kb), :]), 1.0, 0.0)
            return acc + jnp.sum(hit.reshape(kb // COUNT_ROWS, COUNT_ROWS, tq), axis=0)
        acc = lax.fori_loop(0, nblk, body, jnp.zeros((COUNT_ROWS, tq), F32))
        return jnp.sum(acc, axis=0, keepdims=True)

    def key_to_score(key):
        return pltpu.bitcast(jnp.where(key < 0, key ^ 0x7FFFFFFF, key), F32)

    ge0 = count(lambda blk: blk >= 0.0)
    gt0 = count(lambda blk: blk > 0.0)
    at_zero = (ge0 >= kf) & (gt0 < kf)
    above = gt0 >= kf
    lo0 = jnp.where(at_zero, 0, jnp.where(above, 1, KEY_NEG_INF)).astype(I32)
    hi0 = jnp.where(above, KEY_END, 0).astype(I32)

    def search_cond(state):
        return (state[4] == 0) & (state[3] < VALUE_PASSES + 32)

    def one_pass(lo, hi, done, it):
        mid_v = 0.5 * key_to_score(lo) + 0.5 * key_to_score(hi)
        mid_bits = pltpu.bitcast(mid_v, I32)
        mid_k = jnp.where(mid_bits < 0, mid_bits ^ 0x7FFFFFFF, mid_bits)
        inside = (mid_k > lo) & (mid_k < hi) & (it < VALUE_PASSES)
        cand = jnp.where(inside, mid_k, lo + lax.shift_right_logical(hi - lo, 1))
        cand_f = key_to_score(cand)
        c = count(lambda blk: blk >= cand_f)
        open_ = done == 0
        lo = jnp.where(open_ & (c >= kf), cand, lo)
        hi = jnp.where(open_ & (c < kf), cand, hi)
        done = jnp.where((open_ & (c == kf)) | (hi - lo == 1), 1, done)
        return lo, hi, done, it + 1

    def search_pass(state):
        lo, hi, done, it = one_pass(*one_pass(*state[:4]))
        return lo, hi, done, it, jnp.min(done)

    state0 = (lo0, hi0, jnp.where(at_zero, 1, 0).astype(I32), jnp.int32(0), jnp.int32(0))
    t = key_to_score(lax.while_loop(search_cond, search_pass, state0)[0])

    need = kf - count(lambda blk: blk > t)
    tri = tri_ref[...]

    def select_block(j, seen):
        r0 = pl.multiple_of(j * kb, kb)
        blk = s_scr[pl.ds(r0, kb), :]
        tie = jnp.where((blk == t) & (blk > -jnp.inf), 1.0, 0.0)
        rank = _dot(tri, tie.astype(BF16)) + seen
        chosen = (blk > t) | ((tie > 0.0) & (rank <= need))
        bias_scr[pl.ds(r0, kb), :] = jnp.where(chosen, 0.0, NEG_BIG)
        return seen + fold_rows(tie, jnp.sum)

    lax.fori_loop(0, nblk, select_block, jnp.zeros((1, tq), F32))

    m_scr[...] = jnp.full(m_scr.shape, NEG_BIG, F32)
    l_scr[...] = jnp.zeros(l_scr.shape, F32)
    acc_scr[...] = jnp.zeros(acc_scr.shape, F32)

    def attend_block(j, carry):
        r0 = pl.multiple_of(j * kb, kb)
        bias = bias_scr[pl.ds(r0, kb), :]
        heads = range(A_HEADS)
        m_old = m_scr[...]
        l_old = l_scr[...]
        k_pair = [ka_ref[0, pl.ds(r0, kb), i * LANES:(i + 1) * LANES] for i in range(PAIRS)]
        v_pair = [va_ref[0, pl.ds(r0, kb), i * LANES:(i + 1) * LANES] for i in range(PAIRS)]
        ss = [_dot_nt(k_pair[h // 2], qa_heads[h]) + bias for h in heads]
        m_new = [jnp.maximum(m_old[h:h + 1, :], fold_rows(ss[h], jnp.max)) for h in heads]
        ps = [jnp.exp(ss[h] - m_new[h]) for h in heads]
        pvs = [_dot_tn(v_pair[h // 2], ps[h].astype(BF16)) for h in heads]
        alphas = [jnp.exp(m_old[h:h + 1, :] - m_new[h]) for h in heads]
        for i in range(PAIRS):
            e, o = 2 * i, 2 * i + 1
            acc_scr[i] = (jnp.where(even_rows, alphas[e], alphas[o]) * acc_scr[i]
                          + jnp.where(even_rows, pvs[e], pvs[o]))
        for h in heads:
            l_scr[h:h + 1, :] = alphas[h] * l_old[h:h + 1, :] + fold_rows(ps[h], jnp.sum)
            m_scr[h:h + 1, :] = m_new[h]
        return carry

    even_rows = lax.broadcasted_iota(I32, (LANES, tq), 0) < HEAD_DIM
    lax.fori_loop(0, nblk, attend_block, 0)

    l_fin = l_scr[...]
    for i in range(PAIRS):
        denom = jnp.where(even_rows, l_fin[2 * i:2 * i + 1, :], l_fin[2 * i + 1:2 * i + 2, :])
        o_ref[0, :, i * LANES:(i + 1) * LANES] = (acc_scr[i] / denom).T.astype(o_ref.dtype)


def _dsa(qi, wi, qa, ki2, ka, va, *, tq, kb, s_real, past, topk):
    bsz, t_len, _ = qa.shape
    s_pad = ki2.shape[1]
    kern = functools.partial(_dsa_kernel, tq=tq, kb=kb, s_real=s_real, past=past, topk=topk)
    qmap = lambda b, q: (b, q, 0)
    kmap = lambda b, q: (b, 0, 0)
    return pl.pallas_call(
        kern,
        grid=(bsz, t_len // tq),
        in_specs=[
            pl.BlockSpec((1, tq, IDX_HEADS * IDX_DIM), qmap),
            pl.BlockSpec((1, tq, LANES), qmap),
            pl.BlockSpec((1, tq, A_WIDTH), qmap),
            pl.BlockSpec((1, s_pad, LANES), kmap),
            pl.BlockSpec((1, s_pad, A_WIDTH), kmap),
            pl.BlockSpec((1, s_pad, A_WIDTH), kmap),
            pl.BlockSpec((kb, kb), lambda b, q: (0, 0)),
        ],
        out_specs=pl.BlockSpec((1, tq, A_WIDTH), qmap),
        out_shape=jax.ShapeDtypeStruct((bsz, t_len, A_WIDTH), BF16),
        scratch_shapes=[
            pltpu.VMEM((s_pad, tq), F32),
            pltpu.VMEM((s_pad, tq), F32),
            pltpu.VMEM((A_HEADS, tq), F32),
            pltpu.VMEM((A_HEADS, tq), F32),
            pltpu.VMEM((PAIRS, LANES, tq), F32),
        ],
        compiler_params=pltpu.CompilerParams(dimension_semantics=("arbitrary", "arbitrary"),
                                             vmem_limit_bytes=VMEM_LIMIT),
        name="dsa_attention",
    )(qi, wi, qa, ki2, ka, va, _lower_tri(kb))


def _stick_kernel(q_ref, k_ref, v_ref, tri_ref, o_ref, carry_scr, acc_scr, *, tq, kb, past):
    qt = pl.program_id(1)
    q0 = past + qt * tq
    nblk = (q0 + tq - 1 + kb - 1) // kb
    n_full = q0 // kb
    qpos = q0 + lax.broadcasted_iota(I32, (tq, kb), 0)
    carry_scr[...] = jnp.zeros(carry_scr.shape, F32)
    acc_scr[...] = jnp.zeros(acc_scr.shape, F32)
    tri2 = tri_ref[...]
    heads = range(B_HEADS)
    q_heads = [q for i in range(PAIRS) for q in _head_halves(q_ref[0, :, i * LANES:(i + 1) * LANES])]
    even_lanes = lax.broadcasted_iota(I32, (tq, LANES), 1) < HEAD_DIM

    def block(j, masked):
        r0 = pl.multiple_of(j * kb, kb)
        causal = ((r0 + lax.broadcasted_iota(I32, (tq, kb), 1)) < qpos) if masked else None
        carries = [carry_scr[h] for h in heads]
        k_pair = [k_ref[0, pl.ds(r0, kb), i * LANES:(i + 1) * LANES] for i in range(PAIRS)]
        v_pair = [v_ref[0, pl.ds(r0, kb), i * LANES:(i + 1) * LANES] for i in range(PAIRS)]
        zs = [_dot_nt(q_heads[h], k_pair[h // 2]) for h in heads]
        sps = []
        for z in zs:
            sp = jnp.maximum(z, 0.0) + jnp.log(1.0 + jnp.exp(-jnp.abs(z)))
            sps.append(jnp.where(causal, sp, 0.0) if masked else sp)
        sufs = []
        for h in heads:
            hi, lo = _split_bf16(sps[h])
            hilo = jnp.concatenate([hi, lo], axis=1)
            sufs.append(_dot(hilo, tri2) + jnp.tile(carries[h], (1, kb // LANES)))
        pvs = []
        for h in heads:
            a = jnp.exp(zs[h] - sufs[h])
            if masked:
                a = jnp.where(causal, a, 0.0)
            pvs.append(_dot(a.astype(BF16), v_pair[h // 2]))
        for i in range(PAIRS):
            acc_scr[i] = acc_scr[i] + jnp.where(even_lanes, pvs[2 * i], pvs[2 * i + 1])
        for h in heads:
            carry_scr[h] = carries[h] + jnp.sum(sps[h], axis=-1, keepdims=True)

    def masked_step(jj, c):
        block(nblk - 1 - jj, True)
        return c

    def full_step(jj, c):
        block(n_full - 1 - jj, False)
        return c

    lax.fori_loop(0, nblk - n_full, masked_step, 0)
    lax.fori_loop(0, n_full, full_step, 0)
    for i in range(PAIRS):
        o_ref[0, :, i * LANES:(i + 1) * LANES] = acc_scr[i].astype(o_ref.dtype)


def _stick(q, k, v, *, tq, kb, past):
    bsz, t_len, _ = q.shape
    s_pad = k.shape[1]
    kern = functools.partial(_stick_kernel, tq=tq, kb=kb, past=past)
    kmap = lambda b, q: (b, 0, 0)
    return pl.pallas_call(
        kern,
        grid=(bsz, t_len // tq),
        in_specs=[
            pl.BlockSpec((1, tq, B_WIDTH), lambda b, q: (b, q, 0)),
            pl.BlockSpec((1, s_pad, B_WIDTH), kmap),
            pl.BlockSpec((1, s_pad, B_WIDTH), kmap),
            pl.BlockSpec((2 * kb, kb), lambda b, q: (0, 0)),
        ],
        out_specs=pl.BlockSpec((1, tq, B_WIDTH), lambda b, q: (b, q, 0)),
        out_shape=jax.ShapeDtypeStruct((bsz, t_len, B_WIDTH), BF16),
        scratch_shapes=[
            pltpu.VMEM((B_HEADS, tq, LANES), F32),
            pltpu.VMEM((PAIRS, tq, LANES), F32),
        ],
        compiler_params=pltpu.CompilerParams(dimension_semantics=("arbitrary", "arbitrary"),
                                             vmem_limit_bytes=VMEM_LIMIT),
        name="stick_attention",
    )(q, k, v, jnp.tile(_lower_tri(kb), (2, 1)))


def _merge_kernel(x_ref, oa_ref, ob_ref, gates_ref, mod_ref, wua_ref, wub_ref, wout_ref, n2g_ref, wr_ref, br_ref,
                  y_ref, h2_ref, comb_ref, *, nb):
    g1 = mod_ref[:, 2, :]
    sh2 = mod_ref[:, 3, :]
    sc2 = mod_ref[:, 4, :]
    gates = gates_ref[...]
    merged = gates[:, :D_MODEL] * _dot(oa_ref[...], wua_ref[...]) + gates[:, D_MODEL:] * _dot(ob_ref[...], wub_ref[...])
    y = x_ref[...] + _rows_times(_dot(merged.astype(BF16), wout_ref[...]), g1, nb)
    y_ref[...] = y
    h2 = _rows_plus(_rows_times(_rms_rows(y) * n2g_ref[...], 1.0 + sc2, nb), sh2, nb)
    h2_ref[...] = h2.astype(BF16)

    logits = _dot3(h2, wr_ref[...]) + br_ref[...]
    lane = lax.broadcasted_iota(I32, logits.shape, 1).astype(F32)
    far = float(ROUTER_LANES)
    is_g = lane < float(N_GROUPS)
    gl = jnp.where(is_g, logits, -jnp.inf)
    gmax = jnp.max(gl, axis=-1, keepdims=True)
    gsel = jnp.min(jnp.where(is_g & (gl == gmax), lane, far), axis=-1, keepdims=True)
    p_group = 1.0 / jnp.sum(jnp.where(is_g, jnp.exp(gl - gmax), 0.0), axis=-1, keepdims=True)
    e_lo = float(EXPERT_LANE0) + gsel * float(EXPERTS_PER_GROUP)
    in_grp = (lane >= e_lo) & (lane < e_lo + float(EXPERTS_PER_GROUP))
    el = jnp.where(in_grp, logits, -jnp.inf)
    v1 = jnp.max(el, axis=-1, keepdims=True)
    i1 = jnp.min(jnp.where(in_grp & (el == v1), lane, far), axis=-1, keepdims=True)
    el2 = jnp.where(lane == i1, -jnp.inf, el)
    v2 = jnp.max(el2, axis=-1, keepdims=True)
    i2 = jnp.min(jnp.where(in_grp & (lane != i1) & (el2 == v2), lane, far), axis=-1, keepdims=True)
    e2 = jnp.exp(v2 - v1)
    w1 = p_group / (1.0 + e2)
    w2 = p_group * e2 / (1.0 + e2)
    comb_ref[...] = jnp.where(lane == i1, w1, 0.0) + jnp.where(lane == i2, w2, 0.0)


def _merge(x2, oa, ob, gates, mod, wua, wub, wout, n2g, wr, br, *, t_len, tm):
    n = x2.shape[0]
    nb = max(1, tm // t_len)
    tiles_per_batch = max(1, t_len // tm)
    row = lambda i: (i, 0)
    const2 = lambda i: (0, 0)
    return pl.pallas_call(
        functools.partial(_merge_kernel, nb=nb),
        grid=(n // tm,),
        in_specs=[
            pl.BlockSpec((tm, D_MODEL), row),
            pl.BlockSpec((tm, A_WIDTH), row),
            pl.BlockSpec((tm, B_WIDTH), row),
            pl.BlockSpec((tm, 2 * D_MODEL), row),
            pl.BlockSpec((nb, 6, D_MODEL), lambda i: (i // tiles_per_batch, 0, 0)),
            pl.BlockSpec(wua.shape, const2),
            pl.BlockSpec(wub.shape, const2),
            pl.BlockSpec(wout.shape, const2),
            pl.BlockSpec((1, D_MODEL), const2),
            pl.BlockSpec(wr.shape, const2),
            pl.BlockSpec((1, ROUTER_LANES), const2),
        ],
        out_specs=[pl.BlockSpec((tm, D_MODEL), row), pl.BlockSpec((tm, D_MODEL), row),
                   pl.BlockSpec((tm, ROUTER_LANES), row)],
        out_shape=[jax.ShapeDtypeStruct((n, D_MODEL), F32), jax.ShapeDtypeStruct((n, D_MODEL), BF16),
                   jax.ShapeDtypeStruct((n, ROUTER_LANES), F32)],
        compiler_params=pltpu.CompilerParams(dimension_semantics=("arbitrary",), vmem_limit_bytes=VMEM_LIMIT),
        name="merge_router",
    )(x2, oa, ob, gates, mod, wua, wub, wout, n2g, wr, br)


def _moe_kernel(y_ref, h2_ref, comb_ref, mod_ref, wg_ref, wu_ref, wd_ref, o_ref, acc_scr, *, nb):
    e = pl.program_id(1)

    @pl.when(e == 0)
    def _():
        acc_scr[...] = jnp.zeros(acc_scr.shape, F32)

    h2 = h2_ref[...]
    act = _silu(_dot(h2, wg_ref[0].astype(BF16))) * _dot(h2, wu_ref[0].astype(BF16))
    contrib = _dot(act.astype(BF16), wd_ref[0].astype(BF16))
    comb = comb_ref[...]
    lane = lax.broadcasted_iota(I32, comb.shape, 1)
    w_e = jnp.sum(jnp.where(lane == e + EXPERT_LANE0, comb, 0.0), axis=-1, keepdims=True)
    acc_scr[...] += w_e * contrib

    @pl.when(e == pl.num_programs(1) - 1)
    def _():
        o_ref[...] = y_ref[...] + _rows_times(acc_scr[...], mod_ref[:, 5, :], nb)


def _moe(y, h2, comb, mod, w_gate, w_up, w_down, *, t_len, tm):
    n = y.shape[0]
    nb = max(1, tm // t_len)
    tiles_per_batch = max(1, t_len // tm)
    row = lambda i, e: (i, 0)
    return pl.pallas_call(
        functools.partial(_moe_kernel, nb=nb),
        grid=(n // tm, N_EXPERTS),
        in_specs=[
            pl.BlockSpec((tm, D_MODEL), row),
            pl.BlockSpec((tm, D_MODEL), row),
            pl.BlockSpec((tm, ROUTER_LANES), row),
            pl.BlockSpec((nb, 6, D_MODEL), lambda i, e: (i // tiles_per_batch, 0, 0)),
            pl.BlockSpec((1, D_MODEL, D_EXPERT), lambda i, e: (e, 0, 0)),
            pl.BlockSpec((1, D_MODEL, D_EXPERT), lambda i, e: (e, 0, 0)),
            pl.BlockSpec((1, D_EXPERT, D_MODEL), lambda i, e: (e, 0, 0)),
        ],
        out_specs=pl.BlockSpec((tm, D_MODEL), row),
        out_shape=jax.ShapeDtypeStruct((n, D_MODEL), F32),
        scratch_shapes=[pltpu.VMEM((tm, D_MODEL), F32)],
        compiler_params=pltpu.CompilerParams(dimension_semantics=("arbitrary", "arbitrary"),
                                             vmem_limit_bytes=VMEM_LIMIT),
        name="moe_dense",
    )(y, h2, comb, mod, w_gate, w_up, w_down)


def _rope_tables(pos):
    half = ROT_DIM // 2
    freqs = ROPE_THETA ** (-jnp.arange(0, ROT_DIM, 2, dtype=F32) / ROT_DIM)
    ang = pos.astype(F32)[:, None] * freqs[None, :]
    cos, sin = jnp.cos(ang), jnp.sin(ang)
    dh = jnp.arange(LANES) % HEAD_DIM
    fi = dh % half
    cos_t = jnp.where(dh < ROT_DIM, cos[:, fi], 1.0)
    sin_lo = jnp.where(dh < half, -sin[:, fi], 0.0)
    sin_hi = jnp.where((dh >= half) & (dh < ROT_DIM), sin[:, fi], 0.0)
    return jnp.stack([cos_t, sin_lo, sin_hi]).astype(F32)


def _round_up(a, b):
    return (a + b - 1) // b * b


def _layer(x, mod, past_ak, past_av, past_aki, past_bk, past_bv, w, *, cfg):
    bsz, t_len, _ = x.shape
    past = past_ak.shape[1]
    s_real = past + t_len
    n = bsz * t_len
    x2 = x.reshape(n, D_MODEL)
    pos = past + jnp.arange(t_len, dtype=jnp.int32)

    tm = cfg["tm"]
    tab = jnp.tile(_rope_tables(pos), (1, max(1, tm // t_len), 1))
    pr = _proj(x2, mod, w["n1g"], w["wa"], w["wv"], w["wi"], w["wg"], w["bd"], w["gqk"], tab, t_len=t_len, tm=tm)

    new_ak = pr["ka"].reshape(bsz, t_len, A_HEADS, HEAD_DIM)
    new_av = pr["va"].reshape(bsz, t_len, A_HEADS, HEAD_DIM)
    new_aki = pr["ki2"][:, :IDX_DIM].reshape(bsz, t_len, IDX_DIM)
    new_bk = pr["kb"].reshape(bsz, t_len, B_HEADS, HEAD_DIM)
    new_bv = pr["vb"].reshape(bsz, t_len, B_HEADS, HEAD_DIM)

    def all_keys(past_x, new_bf16, s_pad):
        width = new_bf16.shape[-1]
        parts = [new_bf16.reshape(bsz, t_len, width)]
        if past:
            parts.insert(0, past_x.astype(BF16))
        if s_pad > s_real:
            parts.append(jnp.zeros((bsz, s_pad - s_real, width), BF16))
        return parts[0] if len(parts) == 1 else jnp.concatenate(parts, axis=1)

    tq, kb = cfg["dsa_tq"], cfg["dsa_kb"]
    s_pad = _round_up(s_real, kb)
    topk = min(TOPK_MAX, s_real // 4)
    past_ki2 = jnp.concatenate([past_aki, past_aki], axis=-1) if past else None
    oa = _dsa(pr["qi"].reshape(bsz, t_len, -1), pr["wi"].reshape(bsz, t_len, LANES), pr["qa"].reshape(bsz, t_len, -1),
              all_keys(past_ki2, pr["ki2b"], s_pad),
              all_keys(past_ak.reshape(bsz, past, A_WIDTH), pr["kab"], s_pad),
              all_keys(past_av.reshape(bsz, past, A_WIDTH), pr["vab"], s_pad),
              tq=tq, kb=kb, s_real=s_real, past=past, topk=topk)

    tqb, kbb = cfg["sb_tq"], cfg["sb_kb"]
    s_pad_b = _round_up(s_real, kbb)
    ob = _stick(pr["qb"].reshape(bsz, t_len, -1),
                all_keys(past_bk.reshape(bsz, past, B_WIDTH), pr["kbb"], s_pad_b),
                all_keys(past_bv.reshape(bsz, past, B_WIDTH), pr["vbb"], s_pad_b),
                tq=tqb, kb=kbb, past=past)

    y, h2, comb = _merge(x2, oa.reshape(n, A_WIDTH), ob.reshape(n, B_WIDTH), pr["gates"], mod,
                         w["wua"], w["wub"], w["wout"], w["n2g"], w["wr"], w["br"], t_len=t_len, tm=cfg["tm_merge"])
    out = _moe(y, h2, comb, mod, w["w_gate"], w["w_up"], w["w_down"], t_len=t_len, tm=cfg["tm_moe"])
    return out.reshape(bsz, t_len, D_MODEL), (new_ak, new_av, new_aki, new_bk, new_bv)


def _prep_weights(w_in, norm1_g, qnorm_g, knorm_g, w_up_a, w_up_b, w_out, norm2_g, w_rg, b_rg, w_re, b_re,
                  w_e_gate, w_e_up, w_e_down):
    o = 0
    seg = {}
    for name, width in (("qa", A_WIDTH), ("ka", A_WIDTH), ("va", A_WIDTH), ("qi", IDX_HEADS * IDX_DIM),
                        ("ki", IDX_DIM), ("wi", IDX_HEADS), ("qb", B_WIDTH), ("kb", B_WIDTH), ("vb", B_WIDTH),
                        ("gates", 2 * D_MODEL)):
        seg[name] = w_in[:, o:o + width]
        o += width
    pad_i = jnp.zeros((D_MODEL, LANES - IDX_HEADS), w_in.dtype)
    hd = jnp.arange(2 * A_WIDTH) // HEAD_DIM
    n_r = N_GROUPS + N_EXPERTS
    return dict(
        n1g=norm1_g.reshape(1, D_MODEL),
        wa=jnp.concatenate([seg["qa"], seg["ka"]], axis=1).astype(BF16),
        wv=jnp.concatenate([seg["va"], seg["qb"], seg["kb"], seg["vb"]], axis=1).astype(BF16),
        wi=jnp.concatenate([seg["qi"], seg["ki"], seg["ki"], seg["wi"], pad_i], axis=1).astype(BF16),
        wg=seg["gates"].astype(BF16),
        bd=((hd[:, None] == hd[None, :]).astype(F32) / HEAD_DIM).astype(BF16),
        gqk=jnp.concatenate([jnp.tile(qnorm_g, A_HEADS), jnp.tile(knorm_g, A_HEADS)]).reshape(1, 2 * A_WIDTH),
        wua=w_up_a.astype(BF16), wub=w_up_b.astype(BF16), wout=w_out.astype(BF16),
        n2g=norm2_g.reshape(1, D_MODEL),
        wr=jnp.pad(jnp.concatenate([w_rg, w_re], axis=1), ((0, 0), (0, ROUTER_LANES - n_r))),
        br=jnp.pad(jnp.concatenate([b_rg, b_re]), (0, ROUTER_LANES - n_r)).reshape(1, ROUTER_LANES),
        w_gate=w_e_gate, w_up=w_e_up, w_down=w_e_down,
    )


def _group_cfg(t_len):
    if t_len >= 512:
        return dict(tm=256, tm_merge=256, tm_moe=1024, dsa_tq=256, dsa_kb=512, sb_tq=256, sb_kb=256)
    return dict(tm=512, tm_merge=512, tm_moe=512, dsa_tq=t_len, dsa_kb=384, sb_tq=t_len, sb_kb=256)


def kernel(x_prompt, x_sample, cache_a_k, cache_a_v, cache_a_kidx, cache_b_k, cache_b_v, c_prompt, c_sample,
           w_ada, b_ada, norm1_g, w_in, qnorm_g, knorm_g, w_up_a, w_up_b, w_out, norm2_g,
           w_rg, b_rg, w_re, b_re, w_e_gate, w_e_up, w_e_down):
    depth = w_ada.shape[0]
    bp, bs = x_prompt.shape[0], x_sample.shape[0]
    dt = x_prompt.dtype
    rows = _round_up(bp + bs, 8)
    empty_a = jnp.zeros((bp, 0, A_HEADS, HEAD_DIM), dt)
    empty_i = jnp.zeros((bp, 0, IDX_DIM), dt)
    empty_b = jnp.zeros((bp, 0, B_HEADS, HEAD_DIM), dt)
    y_p, y_s = x_prompt, x_sample
    rows_p, rows_s = [], []
    c_all = jnp.concatenate([c_prompt, c_sample, jnp.zeros((rows - bp - bs, D_MODEL), dt)], axis=0)
    for l in range(depth):
        w = _prep_weights(w_in[l], norm1_g[l], qnorm_g[l], knorm_g[l], w_up_a[l], w_up_b[l], w_out[l], norm2_g[l],
                          w_rg[l], b_rg[l], w_re[l], b_re[l], w_e_gate[l], w_e_up[l], w_e_down[l])
        mod = _ada(c_all, w_ada[l], b_ada[l].reshape(1, -1))
        mod_p = mod[:bp].reshape(bp, 6, D_MODEL)
        mod_s = mod[bp:bp + bs].reshape(bs, 6, D_MODEL)
        y_p, new_p = _layer(y_p, mod_p, empty_a, empty_a, empty_i, empty_b, empty_b, w,
                            cfg=_group_cfg(y_p.shape[1]))
        y_s, new_s = _layer(y_s, mod_s, cache_a_k[l], cache_a_v[l], cache_a_kidx[l], cache_b_k[l], cache_b_v[l], w,
                            cfg=_group_cfg(y_s.shape[1]))
        rows_p.append(new_p)
        rows_s.append(new_s)
    stack = lambda rws, i: jnp.stack([r[i] for r in rws])
    return (y_p, y_s) + tuple(stack(rows_p, i) for i in range(5)) + tuple(stack(rows_s, i) for i in range(5))
```

```python
import functools

import jax
import jax.numpy as jnp
from jax import lax
from jax.experimental import pallas as pl
from jax.experimental.pallas import tpu as pltpu

D_MODEL = 1024
CHUNK = 64
CHUNK_SHIFT = 6
HEAD_DIM = 64
A_HEADS = 8
B_HEADS = 8
IDX_HEADS = 4
IDX_DIM = 64
TOPK_MAX = 256
ROPE_THETA = 500000.0
ROT_DIM = HEAD_DIM // 4
N_GROUPS = 4
EXPERTS_PER_GROUP = 8
N_EXPERTS = N_GROUPS * EXPERTS_PER_GROUP
D_EXPERT = 256
RMS_EPS = 1e-6
A_WIDTH = A_HEADS * HEAD_DIM
B_WIDTH = B_HEADS * HEAD_DIM

LANES = 128
CACHE_OUTS = ("ka", "va", "kb", "vb")
IDX_SEG = 512
PAIRS = 4
COUNT_ROWS = 64
ROUTER_LANES = 128
EXPERT_LANE0 = N_GROUPS
VMEM_LIMIT = 56 * 1024 * 1024

F32 = jnp.float32
BF16 = jnp.bfloat16
I32 = jnp.int32
NEG_BIG = -1e30
INT_MIN = -(2 ** 31)
KEY_NEG_INF = (0xFF800000 - (1 << 32)) ^ 0x7FFFFFFF


def _dot(a, b):
    return jnp.dot(a, b, preferred_element_type=F32)


def _dot_nt(a, b):
    return lax.dot_general(a, b, (((1,), (1,)), ((), ())), preferred_element_type=F32)


def _dot_tn(a, b):
    return lax.dot_general(a, b, (((0,), (0,)), ((), ())), preferred_element_type=F32)


def _split_bf16(a):
    hi = a.astype(BF16)
    lo = (a - hi.astype(F32)).astype(BF16)
    return hi, lo


def _dot3(a, b):
    ah, al = _split_bf16(a)
    bh, bl = _split_bf16(b)
    return _dot(ah, bh) + (_dot(ah, bl) + _dot(al, bh))


def _silu(x):
    return x * (1.0 / (1.0 + jnp.exp(-x)))


def _sigmoid(x):
    return 1.0 / (1.0 + jnp.exp(-x))


def _rows_times(x, vec, nb):
    if nb == 1:
        return x * vec
    tm, d = x.shape
    return (x.reshape(nb, tm // nb, d) * vec[:, None, :]).reshape(tm, d)


def _rows_plus(x, vec, nb):
    if nb == 1:
        return x + vec
    tm, d = x.shape
    return (x.reshape(nb, tm // nb, d) + vec[:, None, :]).reshape(tm, d)


def _rms_rows(x):
    return x * lax.rsqrt(jnp.mean(x * x, axis=-1, keepdims=True) + RMS_EPS)


def _ada_kernel(c_ref, w_ref, b_ref, o_ref):
    o_ref[...] = _dot3(_silu(c_ref[...]), w_ref[...]) + b_ref[...]


def _ada(c_all, w_ada, b_ada):
    rows = c_all.shape[0]
    n = w_ada.shape[1]
    tn = 1536
    return pl.pallas_call(
        _ada_kernel,
        grid=(n // tn,),
        in_specs=[pl.BlockSpec((rows, D_MODEL), lambda j: (0, 0)),
                  pl.BlockSpec((D_MODEL, tn), lambda j: (0, j)),
                  pl.BlockSpec((1, tn), lambda j: (0, j))],
        out_specs=pl.BlockSpec((rows, tn), lambda j: (0, j)),
        out_shape=jax.ShapeDtypeStruct((rows, n), F32),
        compiler_params=pltpu.CompilerParams(dimension_semantics=("arbitrary",), vmem_limit_bytes=VMEM_LIMIT),
        name="ada_mod",
    )(c_all, w_ada, b_ada)


def _rope_lanes(a, tab_ref, reps):
    width = a.shape[-1]
    cos = jnp.tile(tab_ref[0], (1, reps)) if reps > 1 else tab_ref[0]
    sin_lo = jnp.tile(tab_ref[1], (1, reps)) if reps > 1 else tab_ref[1]
    sin_hi = jnp.tile(tab_ref[2], (1, reps)) if reps > 1 else tab_ref[2]
    half = ROT_DIM // 2
    return a * cos + pltpu.roll(a, width - half, 1) * sin_lo + pltpu.roll(a, half, 1) * sin_hi


def _proj_kernel(x_ref, mod_ref, n1g_ref, wa_ref, wv_ref, wi_ref, wg_ref, bd_ref, gqk_ref, tab_ref,
                 qa_ref, ka_ref, kab_ref, va_ref, vab_ref, qb_ref, kb_ref, kbb_ref, vb_ref, vbb_ref,
                 qi_ref, ki2_ref, ki2b_ref, wi_out_ref, gates_ref, *, nb):
    x = x_ref[...]
    sh1 = mod_ref[:, 0, :]
    sc1 = mod_ref[:, 1, :]
    h = _rows_plus(_rows_times(_rms_rows(x) * n1g_ref[...], 1.0 + sc1, nb), sh1, nb)
    hb = h.astype(BF16)

    a = _dot(hb, wa_ref[...])
    ms = _dot((a * a).astype(BF16), bd_ref[...])
    a = a * lax.rsqrt(ms + RMS_EPS) * gqk_ref[...]
    a = _rope_lanes(a, tab_ref, a.shape[-1] // LANES)
    qa_ref[...] = (a[:, :A_WIDTH] * (HEAD_DIM ** -0.5)).astype(BF16)
    ka = a[:, A_WIDTH:]
    ka_ref[...] = ka.reshape(ka_ref.shape)
    kab_ref[...] = ka.astype(BF16)

    v = _dot(hb, wv_ref[...])
    va = v[:, :A_WIDTH]
    va_ref[...] = va.reshape(va_ref.shape)
    vab_ref[...] = va.astype(BF16)
    qb_ref[...] = (v[:, A_WIDTH:A_WIDTH + B_WIDTH] * (HEAD_DIM ** -0.5)).astype(BF16)
    kb = v[:, A_WIDTH + B_WIDTH:A_WIDTH + 2 * B_WIDTH]
    kb_ref[...] = kb.reshape(kb_ref.shape)
    kbb_ref[...] = kb.astype(BF16)
    vb = v[:, A_WIDTH + 2 * B_WIDTH:]
    vb_ref[...] = vb.reshape(vb_ref.shape)
    vbb_ref[...] = vb.astype(BF16)

    i = _dot(hb, wi_ref[...])
    nr = IDX_SEG - LANES
    ir = _rope_lanes(i[:, :nr], tab_ref, nr // LANES)
    nq = IDX_HEADS * IDX_DIM
    qi_ref[...] = (ir[:, :nq] * (IDX_DIM ** -0.5)).astype(BF16)
    ki2 = ir[:, nq:]
    ki2_ref[...] = ki2
    ki2b_ref[...] = ki2.astype(BF16)
    wi_out_ref[...] = i[:, nr:]

    gates_ref[...] = _sigmoid(_dot(hb, wg_ref[...])).astype(BF16)


def _proj(x2, mod, n1g, wa, wv, wi, wg, bd, gqk, tab, *, t_len, tm):
    n = x2.shape[0]
    nb = max(1, tm // t_len)
    tiles_per_batch = max(1, t_len // tm)
    tab_tiles = tab.shape[1] // tm
    row = lambda i: (i, 0)
    const2 = lambda i: (0, 0)
    tabmap = lambda i: (0, i % tab_tiles, 0)
    in_specs = [
        pl.BlockSpec((tm, D_MODEL), row),
        pl.BlockSpec((nb, 6, D_MODEL), lambda i: (i // tiles_per_batch, 0, 0)),
        pl.BlockSpec((1, D_MODEL), const2),
        pl.BlockSpec(wa.shape, const2),
        pl.BlockSpec(wv.shape, const2),
        pl.BlockSpec(wi.shape, const2),
        pl.BlockSpec(wg.shape, const2),
        pl.BlockSpec(bd.shape, const2),
        pl.BlockSpec((1, 2 * A_WIDTH), const2),
        pl.BlockSpec((3, tm, LANES), tabmap),
    ]
    outs = [
        ("qa", A_WIDTH, BF16), ("ka", A_WIDTH, F32), ("kab", A_WIDTH, BF16),
        ("va", A_WIDTH, F32), ("vab", A_WIDTH, BF16),
        ("qb", B_WIDTH, BF16), ("kb", B_WIDTH, F32), ("kbb", B_WIDTH, BF16),
        ("vb", B_WIDTH, F32), ("vbb", B_WIDTH, BF16),
        ("qi", IDX_HEADS * IDX_DIM, BF16), ("ki2", LANES, F32), ("ki2b", LANES, BF16), ("wi", LANES, F32),
        ("gates", 2 * D_MODEL, BF16),
    ]
    res = pl.pallas_call(
        functools.partial(_proj_kernel, nb=nb),
        grid=(n // tm,),
        in_specs=in_specs,
        out_specs=[pl.BlockSpec((tm, w // HEAD_DIM, HEAD_DIM), lambda i: (i, 0, 0)) if name in CACHE_OUTS
                   else pl.BlockSpec((tm, w), row) for name, w, _ in outs],
        out_shape=[jax.ShapeDtypeStruct((n, w // HEAD_DIM, HEAD_DIM) if name in CACHE_OUTS else (n, w), dt)
                   for name, w, dt in outs],
        compiler_params=pltpu.CompilerParams(dimension_semantics=("arbitrary",), vmem_limit_bytes=VMEM_LIMIT),
        name="in_proj",
    )(x2, mod, n1g, wa, wv, wi, wg, bd, gqk, tab)
    return dict(zip([o[0] for o in outs], res))


def _lower_tri(n):
    idx = jnp.arange(n)
    return (idx[:, None] >= idx[None, :]).astype(BF16)


def _head_halves(x):
    lane = lax.broadcasted_iota(I32, x.shape, 1)
    zero = jnp.zeros_like(x)
    return jnp.where(lane < HEAD_DIM, x, zero), jnp.where(lane >= HEAD_DIM, x, zero)


def _dsa_kernel(qi_ref, w_ref, qa_ref, ki_ref, ka_ref, va_ref, tri_ref, o_ref,
                s_scr, bias_scr, m_scr, l_scr, acc_scr, *, tq, kb, s_real, past, topk):
    qt = pl.program_id(1)
    q0 = past + qt * tq
    kmax = jnp.minimum((lax.shift_right_logical(q0 + tq - 1, CHUNK_SHIFT) + 1) * CHUNK, s_real)
    nblk = (kmax + kb - 1) // kb
    qchunk = lax.shift_right_logical(q0 + lax.broadcasted_iota(I32, (1, tq), 1), CHUNK_SHIFT)
    kf = float(topk)
    chunks = lambda ref, i: ref[0, :, i * LANES:(i + 1) * LANES]
    qi_heads = [q for i in range(IDX_HEADS // 2) for q in _head_halves(chunks(qi_ref, i))]
    qa_heads = [q for i in range(PAIRS) for q in _head_halves(chunks(qa_ref, i))]
    w_rows = w_ref[0].T

    def fold_rows(x, op):
        part = op(x.reshape(kb // COUNT_ROWS, COUNT_ROWS, tq), axis=0)
        return op(part, axis=0, keepdims=True)

    def score_block(j, masked):
        r0 = pl.multiple_of(j * kb, kb)
        kib = ki_ref[0, pl.ds(r0, kb), :]
        acc = jnp.zeros((kb, tq), F32)
        for h in range(IDX_HEADS):
            rel = jnp.maximum(_dot_nt(kib, qi_heads[h]), 0.0)
            acc = acc + w_rows[h:h + 1, :] * rel
        s = acc + 0.0
        if masked:
            kpos = r0 + lax.broadcasted_iota(I32, (kb, tq), 0)
            adm = (lax.shift_right_logical(kpos, CHUNK_SHIFT) <= qchunk) & (kpos < s_real)
            s = jnp.where(adm, s, -jnp.inf)
        s_scr[pl.ds(r0, kb), :] = s

    n_open = jnp.minimum((lax.shift_right_logical(q0, CHUNK_SHIFT) + 1) * CHUNK, s_real) // kb

    def open_step(j, c):
        score_block(j, False)
        return c

    def masked_step(j, c):
        score_block(j, True)
        return c

    lax.fori_loop(0, n_open, open_step, 0)
    lax.fori_loop(n_open, nblk, masked_step, 0)

    def count(pred):
        def body(j, acc):
            r0 = pl.multiple_of(j * kb, kb)
            hit = jnp.where(pred(s_scr[pl.ds(r0, kb), :]), 1.0, 0.0)
            return acc + jnp.sum(hit.reshape(kb // COUNT_ROWS, COUNT_ROWS, tq), axis=0)
        acc = lax.fori_loop(0, nblk, body, jnp.zeros((COUNT_ROWS, tq), F32))
        return jnp.sum(acc, axis=0, keepdims=True)

    def key_to_score(key):
        return pltpu.bitcast(jnp.where(key < 0, key ^ 0x7FFFFFFF, key), F32)

    c0 = count(lambda blk: blk >= 0.0)
    t0 = jnp.where(c0 >= kf, 0, INT_MIN).astype(I32)

    def value_bit(i, t):
        cand = t | jnp.left_shift(jnp.int32(1), 30 - i)
        cand_f = key_to_score(cand)
        return jnp.where(count(lambda blk: blk >= cand_f) >= kf, cand, t)

    t_key = lax.fori_loop(0, 31, value_bit, t0)
    t = key_to_score(jnp.maximum(t_key, KEY_NEG_INF))

    need = kf - count(lambda blk: blk > t)
    tri = tri_ref[...]

    def select_block(j, seen):
        r0 = pl.multiple_of(j * kb, kb)
        blk = s_scr[pl.ds(r0, kb), :]
        tie = jnp.where((blk == t) & (blk > -jnp.inf), 1.0, 0.0)
        rank = _dot(tri, tie.astype(BF16)) + seen
        chosen = (blk > t) | ((tie > 0.0) & (rank <= need))
        bias_scr[pl.ds(r0, kb), :] = jnp.where(chosen, 0.0, NEG_BIG)
        return seen + fold_rows(tie, jnp.sum)

    lax.fori_loop(0, nblk, select_block, jnp.zeros((1, tq), F32))

    m_scr[...] = jnp.full(m_scr.shape, NEG_BIG, F32)
    l_scr[...] = jnp.zeros(l_scr.shape, F32)
    acc_scr[...] = jnp.zeros(acc_scr.shape, F32)

    def attend_block(j, carry):
        r0 = pl.multiple_of(j * kb, kb)
        bias = bias_scr[pl.ds(r0, kb), :]
        heads = range(A_HEADS)
        m_old = m_scr[...]
        l_old = l_scr[...]
        k_pair = [ka_ref[0, pl.ds(r0, kb), i * LANES:(i + 1) * LANES] for i in range(PAIRS)]
        v_pair = [va_ref[0, pl.ds(r0, kb), i * LANES:(i + 1) * LANES] for i in range(PAIRS)]
        ss = [_dot_nt(k_pair[h // 2], qa_heads[h]) + bias for h in heads]
        m_new = [jnp.maximum(m_old[h:h + 1, :], fold_rows(ss[h], jnp.max)) for h in heads]
        ps = [jnp.exp(ss[h] - m_new[h]) for h in heads]
        pvs = [_dot_tn(v_pair[h // 2], ps[h].astype(BF16)) for h in heads]
        alphas = [jnp.exp(m_old[h:h + 1, :] - m_new[h]) for h in heads]
        for i in range(PAIRS):
            e, o = 2 * i, 2 * i + 1
            acc_scr[i] = (jnp.where(even_rows, alphas[e], alphas[o]) * acc_scr[i]
                          + jnp.where(even_rows, pvs[e], pvs[o]))
        for h in heads:
            l_scr[h:h + 1, :] = alphas[h] * l_old[h:h + 1, :] + fold_rows(ps[h], jnp.sum)
            m_scr[h:h + 1, :] = m_new[h]
        return carry

    even_rows = lax.broadcasted_iota(I32, (LANES, tq), 0) < HEAD_DIM
    lax.fori_loop(0, nblk, attend_block, 0)

    l_fin = l_scr[...]
    for i in range(PAIRS):
        denom = jnp.where(even_rows, l_fin[2 * i:2 * i + 1, :], l_fin[2 * i + 1:2 * i + 2, :])
        o_ref[0, :, i * LANES:(i + 1) * LANES] = (acc_scr[i] / denom).T.astype(o_ref.dtype)


def _dsa(qi, wi, qa, ki2, ka, va, *, tq, kb, s_real, past, topk):
    bsz, t_len, _ = qa.shape
    s_pad = ki2.shape[1]
    kern = functools.partial(_dsa_kernel, tq=tq, kb=kb, s_real=s_real, past=past, topk=topk)
    qmap = lambda b, q: (b, q, 0)
    kmap = lambda b, q: (b, 0, 0)
    return pl.pallas_call(
        kern,
        grid=(bsz, t_len // tq),
        in_specs=[
            pl.BlockSpec((1, tq, IDX_HEADS * IDX_DIM), qmap),
            pl.BlockSpec((1, tq, LANES), qmap),
            pl.BlockSpec((1, tq, A_WIDTH), qmap),
            pl.BlockSpec((1, s_pad, LANES), kmap),
            pl.BlockSpec((1, s_pad, A_WIDTH), kmap),
            pl.BlockSpec((1, s_pad, A_WIDTH), kmap),
            pl.BlockSpec((kb, kb), lambda b, q: (0, 0)),
        ],
        out_specs=pl.BlockSpec((1, tq, A_WIDTH), qmap),
        out_shape=jax.ShapeDtypeStruct((bsz, t_len, A_WIDTH), BF16),
        scratch_shapes=[
            pltpu.VMEM((s_pad, tq), F32),
            pltpu.VMEM((s_pad, tq), F32),
            pltpu.VMEM((A_HEADS, tq), F32),
            pltpu.VMEM((A_HEADS, tq), F32),
            pltpu.VMEM((PAIRS, LANES, tq), F32),
        ],
        compiler_params=pltpu.CompilerParams(dimension_semantics=("arbitrary", "arbitrary"),
                                             vmem_limit_bytes=VMEM_LIMIT),
        name="dsa_attention",
    )(qi, wi, qa, ki2, ka, va, _lower_tri(kb))


def _stick_kernel(q_ref, k_ref, v_ref, tri_ref, o_ref, carry_scr, acc_scr, *, tq, kb, past):
    qt = pl.program_id(1)
    q0 = past + qt * tq
    nblk = (q0 + tq - 1 + kb - 1) // kb
    n_full = q0 // kb
    qpos = q0 + lax.broadcasted_iota(I32, (tq, kb), 0)
    carry_scr[...] = jnp.zeros(carry_scr.shape, F32)
    acc_scr[...] = jnp.zeros(acc_scr.shape, F32)
    tri2 = tri_ref[...]
    heads = range(B_HEADS)
    q_heads = [q for i in range(PAIRS) for q in _head_halves(q_ref[0, :, i * LANES:(i + 1) * LANES])]
    even_lanes = lax.broadcasted_iota(I32, (tq, LANES), 1) < HEAD_DIM

    def block(j, masked):
        r0 = pl.multiple_of(j * kb, kb)
        causal = ((r0 + lax.broadcasted_iota(I32, (tq, kb), 1)) < qpos) if masked else None
        carries = [carry_scr[h] for h in heads]
        k_pair = [k_ref[0, pl.ds(r0, kb), i * LANES:(i + 1) * LANES] for i in range(PAIRS)]
        v_pair = [v_ref[0, pl.ds(r0, kb), i * LANES:(i + 1) * LANES] for i in range(PAIRS)]
        zs = [_dot_nt(q_heads[h], k_pair[h // 2]) for h in heads]
        sps = []
        for z in zs:
            sp = jnp.maximum(z, 0.0) + jnp.log(1.0 + jnp.exp(-jnp.abs(z)))
            sps.append(jnp.where(causal, sp, 0.0) if masked else sp)
        sufs = []
        for h in heads:
            hi, lo = _split_bf16(sps[h])
            hilo = jnp.concatenate([hi, lo], axis=1)
            sufs.append(_dot(hilo, tri2) + jnp.tile(carries[h], (1, kb // LANES)))
        pvs = []
        for h in heads:
            a = jnp.exp(zs[h] - sufs[h])
            if masked:
                a = jnp.where(causal, a, 0.0)
            pvs.append(_dot(a.astype(BF16), v_pair[h // 2]))
        for i in range(PAIRS):
            acc_scr[i] = acc_scr[i] + jnp.where(even_lanes, pvs[2 * i], pvs[2 * i + 1])
        for h in heads:
            carry_scr[h] = carries[h] + jnp.sum(sps[h], axis=-1, keepdims=True)

    def masked_step(jj, c):
        block(nblk - 1 - jj, True)
        return c

    def full_step(jj, c):
        block(n_full - 1 - jj, False)
        return c

    lax.fori_loop(0, nblk - n_full, masked_step, 0)
    lax.fori_loop(0, n_full, full_step, 0)
    for i in range(PAIRS):
        o_ref[0, :, i * LANES:(i + 1) * LANES] = acc_scr[i].astype(o_ref.dtype)


def _stick(q, k, v, *, tq, kb, past):
    bsz, t_len, _ = q.shape
    s_pad = k.shape[1]
    kern = functools.partial(_stick_kernel, tq=tq, kb=kb, past=past)
    kmap = lambda b, q: (b, 0, 0)
    return pl.pallas_call(
        kern,
        grid=(bsz, t_len // tq),
        in_specs=[
            pl.BlockSpec((1, tq, B_WIDTH), lambda b, q: (b, q, 0)),
            pl.BlockSpec((1, s_pad, B_WIDTH), kmap),
            pl.BlockSpec((1, s_pad, B_WIDTH), kmap),
            pl.BlockSpec((2 * kb, kb), lambda b, q: (0, 0)),
        ],
        out_specs=pl.BlockSpec((1, tq, B_WIDTH), lambda b, q: (b, q, 0)),
        out_shape=jax.ShapeDtypeStruct((bsz, t_len, B_WIDTH), BF16),
        scratch_shapes=[
            pltpu.VMEM((B_HEADS, tq, LANES), F32),
            pltpu.VMEM((PAIRS, tq, LANES), F32),
        ],
        compiler_params=pltpu.CompilerParams(dimension_semantics=("arbitrary", "arbitrary"),
                                             vmem_limit_bytes=VMEM_LIMIT),
        name="stick_attention",
    )(q, k, v, jnp.tile(_lower_tri(kb), (2, 1)))


def _merge_kernel(x_ref, oa_ref, ob_ref, gates_ref, mod_ref, wua_ref, wub_ref, wout_ref, n2g_ref, wr_ref, br_ref,
                  y_ref, h2_ref, comb_ref, *, nb):
    g1 = mod_ref[:, 2, :]
    sh2 = mod_ref[:, 3, :]
    sc2 = mod_ref[:, 4, :]
    gates = gates_ref[...].astype(F32)
    merged = gates[:, :D_MODEL] * _dot(oa_ref[...], wua_ref[...]) + gates[:, D_MODEL:] * _dot(ob_ref[...], wub_ref[...])
    y = x_ref[...] + _rows_times(_dot(merged.astype(BF16), wout_ref[...]), g1, nb)
    y_ref[...] = y
    h2 = _rows_plus(_rows_times(_rms_rows(y) * n2g_ref[...], 1.0 + sc2, nb), sh2, nb)
    h2_ref[...] = h2.astype(BF16)

    logits = _dot3(h2, wr_ref[...]) + br_ref[...]
    lane = lax.broadcasted_iota(I32, logits.shape, 1).astype(F32)
    far = float(ROUTER_LANES)
    is_g = lane < float(N_GROUPS)
    gl = jnp.where(is_g, logits, -jnp.inf)
    gmax = jnp.max(gl, axis=-1, keepdims=True)
    gsel = jnp.min(jnp.where(is_g & (gl == gmax), lane, far), axis=-1, keepdims=True)
    p_group = 1.0 / jnp.sum(jnp.where(is_g, jnp.exp(gl - gmax), 0.0), axis=-1, keepdims=True)
    e_lo = float(EXPERT_LANE0) + gsel * float(EXPERTS_PER_GROUP)
    in_grp = (lane >= e_lo) & (lane < e_lo + float(EXPERTS_PER_GROUP))
    el = jnp.where(in_grp, logits, -jnp.inf)
    v1 = jnp.max(el, axis=-1, keepdims=True)
    i1 = jnp.min(jnp.where(in_grp & (el == v1), lane, far), axis=-1, keepdims=True)
    el2 = jnp.where(lane == i1, -jnp.inf, el)
    v2 = jnp.max(el2, axis=-1, keepdims=True)
    i2 = jnp.min(jnp.where(in_grp & (lane != i1) & (el2 == v2), lane, far), axis=-1, keepdims=True)
    e2 = jnp.exp(v2 - v1)
    w1 = p_group / (1.0 + e2)
    w2 = p_group * e2 / (1.0 + e2)
    comb_ref[...] = jnp.where(lane == i1, w1, 0.0) + jnp.where(lane == i2, w2, 0.0)


def _merge(x2, oa, ob, gates, mod, wua, wub, wout, n2g, wr, br, *, t_len, tm):
    n = x2.shape[0]
    nb = max(1, tm // t_len)
    tiles_per_batch = max(1, t_len // tm)
    row = lambda i: (i, 0)
    const2 = lambda i: (0, 0)
    return pl.pallas_call(
        functools.partial(_merge_kernel, nb=nb),
        grid=(n // tm,),
        in_specs=[
            pl.BlockSpec((tm, D_MODEL), row),
            pl.BlockSpec((tm, A_WIDTH), row),
            pl.BlockSpec((tm, B_WIDTH), row),
            pl.BlockSpec((tm, 2 * D_MODEL), row),
            pl.BlockSpec((nb, 6, D_MODEL), lambda i: (i // tiles_per_batch, 0, 0)),
            pl.BlockSpec(wua.shape, const2),
            pl.BlockSpec(wub.shape, const2),
            pl.BlockSpec(wout.shape, const2),
            pl.BlockSpec((1, D_MODEL), const2),
            pl.BlockSpec(wr.shape, const2),
            pl.BlockSpec((1, ROUTER_LANES), const2),
        ],
        out_specs=[pl.BlockSpec((tm, D_MODEL), row), pl.BlockSpec((tm, D_MODEL), row),
                   pl.BlockSpec((tm, ROUTER_LANES), row)],
        out_shape=[jax.ShapeDtypeStruct((n, D_MODEL), F32), jax.ShapeDtypeStruct((n, D_MODEL), BF16),
                   jax.ShapeDtypeStruct((n, ROUTER_LANES), F32)],
        compiler_params=pltpu.CompilerParams(dimension_semantics=("arbitrary",), vmem_limit_bytes=VMEM_LIMIT),
        name="merge_router",
    )(x2, oa, ob, gates, mod, wua, wub, wout, n2g, wr, br)


def _moe_kernel(y_ref, h2_ref, comb_ref, mod_ref, wg_ref, wu_ref, wd_ref, o_ref, acc_scr, *, nb):
    e = pl.program_id(1)

    @pl.when(e == 0)
    def _():
        acc_scr[...] = jnp.zeros(acc_scr.shape, F32)

    h2 = h2_ref[...]
    act = _silu(_dot(h2, wg_ref[0].astype(BF16))) * _dot(h2, wu_ref[0].astype(BF16))
    contrib = _dot(act.astype(BF16), wd_ref[0].astype(BF16))
    comb = comb_ref[...]
    lane = lax.broadcasted_iota(I32, comb.shape, 1)
    w_e = jnp.sum(jnp.where(lane == e + EXPERT_LANE0, comb, 0.0), axis=-1, keepdims=True)
    acc_scr[...] += w_e * contrib

    @pl.when(e == pl.num_programs(1) - 1)
    def _():
        o_ref[...] = y_ref[...] + _rows_times(acc_scr[...], mod_ref[:, 5, :], nb)


def _moe(y, h2, comb, mod, w_gate, w_up, w_down, *, t_len, tm):
    n = y.shape[0]
    nb = max(1, tm // t_len)
    tiles_per_batch = max(1, t_len // tm)
    row = lambda i, e: (i, 0)
    return pl.pallas_call(
        functools.partial(_moe_kernel, nb=nb),
        grid=(n // tm, N_EXPERTS),
        in_specs=[
            pl.BlockSpec((tm, D_MODEL), row),
            pl.BlockSpec((tm, D_MODEL), row),
            pl.BlockSpec((tm, ROUTER_LANES), row),
            pl.BlockSpec((nb, 6, D_MODEL), lambda i, e: (i // tiles_per_batch, 0, 0)),
            pl.BlockSpec((1, D_MODEL, D_EXPERT), lambda i, e: (e, 0, 0)),
            pl.BlockSpec((1, D_MODEL, D_EXPERT), lambda i, e: (e, 0, 0)),
            pl.BlockSpec((1, D_EXPERT, D_MODEL), lambda i, e: (e, 0, 0)),
        ],
        out_specs=pl.BlockSpec((tm, D_MODEL), row),
        out_shape=jax.ShapeDtypeStruct((n, D_MODEL), F32),
        scratch_shapes=[pltpu.VMEM((tm, D_MODEL), F32)],
        compiler_params=pltpu.CompilerParams(dimension_semantics=("arbitrary", "arbitrary"),
                                             vmem_limit_bytes=VMEM_LIMIT),
        name="moe_dense",
    )(y, h2, comb, mod, w_gate, w_up, w_down)


def _rope_tables(pos):
    half = ROT_DIM // 2
    freqs = ROPE_THETA ** (-jnp.arange(0, ROT_DIM, 2, dtype=F32) / ROT_DIM)
    ang = pos.astype(F32)[:, None] * freqs[None, :]
    cos, sin = jnp.cos(ang), jnp.sin(ang)
    dh = jnp.arange(LANES) % HEAD_DIM
    fi = dh % half
    cos_t = jnp.where(dh < ROT_DIM, cos[:, fi], 1.0)
    sin_lo = jnp.where(dh < half, -sin[:, fi], 0.0)
    sin_hi = jnp.where((dh >= half) & (dh < ROT_DIM), sin[:, fi], 0.0)
    return jnp.stack([cos_t, sin_lo, sin_hi]).astype(F32)


def _round_up(a, b):
    return (a + b - 1) // b * b


def _layer(x, mod, past_ak, past_av, past_aki, past_bk, past_bv, w, *, cfg):
    bsz, t_len, _ = x.shape
    past = past_ak.shape[1]
    s_real = past + t_len
    n = bsz * t_len
    x2 = x.reshape(n, D_MODEL)
    pos = past + jnp.arange(t_len, dtype=jnp.int32)

    tm = cfg["tm"]
    tab = jnp.tile(_rope_tables(pos), (1, max(1, tm // t_len), 1))
    pr = _proj(x2, mod, w["n1g"], w["wa"], w["wv"], w["wi"], w["wg"], w["bd"], w["gqk"], tab, t_len=t_len, tm=tm)

    new_ak = pr["ka"].reshape(bsz, t_len, A_HEADS, HEAD_DIM)
    new_av = pr["va"].reshape(bsz, t_len, A_HEADS, HEAD_DIM)
    new_aki = pr["ki2"][:, :IDX_DIM].reshape(bsz, t_len, IDX_DIM)
    new_bk = pr["kb"].reshape(bsz, t_len, B_HEADS, HEAD_DIM)
    new_bv = pr["vb"].reshape(bsz, t_len, B_HEADS, HEAD_DIM)

    def all_keys(past_x, new_bf16, s_pad):
        width = new_bf16.shape[-1]
        parts = [new_bf16.reshape(bsz, t_len, width)]
        if past:
            parts.insert(0, past_x.astype(BF16))
        if s_pad > s_real:
            parts.append(jnp.zeros((bsz, s_pad - s_real, width), BF16))
        return parts[0] if len(parts) == 1 else jnp.concatenate(parts, axis=1)

    tq, kb = cfg["dsa_tq"], cfg["dsa_kb"]
    s_pad = _round_up(s_real, kb)
    topk = min(TOPK_MAX, s_real // 4)
    past_ki2 = jnp.concatenate([past_aki, past_aki], axis=-1) if past else None
    oa = _dsa(pr["qi"].reshape(bsz, t_len, -1), pr["wi"].reshape(bsz, t_len, LANES), pr["qa"].reshape(bsz, t_len, -1),
              all_keys(past_ki2, pr["ki2b"], s_pad),
              all_keys(past_ak.reshape(bsz, past, A_WIDTH), pr["kab"], s_pad),
              all_keys(past_av.reshape(bsz, past, A_WIDTH), pr["vab"], s_pad),
              tq=tq, kb=kb, s_real=s_real, past=past, topk=topk)

    tqb, kbb = cfg["sb_tq"], cfg["sb_kb"]
    s_pad_b = _round_up(s_real, kbb)
    ob = _stick(pr["qb"].reshape(bsz, t_len, -1),
                all_keys(past_bk.reshape(bsz, past, B_WIDTH), pr["kbb"], s_pad_b),
                all_keys(past_bv.reshape(bsz, past, B_WIDTH), pr["vbb"], s_pad_b),
                tq=tqb, kb=kbb, past=past)

    y, h2, comb = _merge(x2, oa.reshape(n, A_WIDTH), ob.reshape(n, B_WIDTH), pr["gates"], mod,
                         w["wua"], w["wub"], w["wout"], w["n2g"], w["wr"], w["br"], t_len=t_len, tm=cfg["tm_merge"])
    out = _moe(y, h2, comb, mod, w["w_gate"], w["w_up"], w["w_down"], t_len=t_len, tm=cfg["tm_moe"])
    return out.reshape(bsz, t_len, D_MODEL), (new_ak, new_av, new_aki, new_bk, new_bv)


def _prep_weights(w_in, norm1_g, qnorm_g, knorm_g, w_up_a, w_up_b, w_out, norm2_g, w_rg, b_rg, w_re, b_re,
                  w_e_gate, w_e_up, w_e_down):
    o = 0
    seg = {}
    for name, width in (("qa", A_WIDTH), ("ka", A_WIDTH), ("va", A_WIDTH), ("qi", IDX_HEADS * IDX_DIM),
                        ("ki", IDX_DIM), ("wi", IDX_HEADS), ("qb", B_WIDTH), ("kb", B_WIDTH), ("vb", B_WIDTH),
                        ("gates", 2 * D_MODEL)):
        seg[name] = w_in[:, o:o + width]
        o += width
    pad_i = jnp.zeros((D_MODEL, LANES - IDX_HEADS), w_in.dtype)
    hd = jnp.arange(2 * A_WIDTH) // HEAD_DIM
    n_r = N_GROUPS + N_EXPERTS
    return dict(
        n1g=norm1_g.reshape(1, D_MODEL),
        wa=jnp.concatenate([seg["qa"], seg["ka"]], axis=1).astype(BF16),
        wv=jnp.concatenate([seg["va"], seg["qb"], seg["kb"], seg["vb"]], axis=1).astype(BF16),
        wi=jnp.concatenate([seg["qi"], seg["ki"], seg["ki"], seg["wi"], pad_i], axis=1).astype(BF16),
        wg=seg["gates"].astype(BF16),
        bd=((hd[:, None] == hd[None, :]).astype(F32) / HEAD_DIM).astype(BF16),
        gqk=jnp.concatenate([jnp.tile(qnorm_g, A_HEADS), jnp.tile(knorm_g, A_HEADS)]).reshape(1, 2 * A_WIDTH),
        wua=w_up_a.astype(BF16), wub=w_up_b.astype(BF16), wout=w_out.astype(BF16),
        n2g=norm2_g.reshape(1, D_MODEL),
        wr=jnp.pad(jnp.concatenate([w_rg, w_re], axis=1), ((0, 0), (0, ROUTER_LANES - n_r))),
        br=jnp.pad(jnp.concatenate([b_rg, b_re]), (0, ROUTER_LANES - n_r)).reshape(1, ROUTER_LANES),
        w_gate=w_e_gate, w_up=w_e_up, w_down=w_e_down,
    )


def _group_cfg(t_len):
    if t_len >= 512:
        return dict(tm=256, tm_merge=256, tm_moe=1024, dsa_tq=256, dsa_kb=512, sb_tq=256, sb_kb=256)
    return dict(tm=512, tm_merge=512, tm_moe=512, dsa_tq=t_len, dsa_kb=384, sb_tq=t_len, sb_kb=256)


def kernel(x_prompt, x_sample, cache_a_k, cache_a_v, cache_a_kidx, cache_b_k, cache_b_v, c_prompt, c_sample,
           w_ada, b_ada, norm1_g, w_in, qnorm_g, knorm_g, w_up_a, w_up_b, w_out, norm2_g,
           w_rg, b_rg, w_re, b_re, w_e_gate, w_e_up, w_e_down):
    depth = w_ada.shape[0]
    bp, bs = x_prompt.shape[0], x_sample.shape[0]
    dt = x_prompt.dtype
    rows = _round_up(bp + bs, 8)
    empty_a = jnp.zeros((bp, 0, A_HEADS, HEAD_DIM), dt)
    empty_i = jnp.zeros((bp, 0, IDX_DIM), dt)
    empty_b = jnp.zeros((bp, 0, B_HEADS, HEAD_DIM), dt)
    y_p, y_s = x_prompt, x_sample
    rows_p, rows_s = [], []
    c_all = jnp.concatenate([c_prompt, c_sample, jnp.zeros((rows - bp - bs, D_MODEL), dt)], axis=0)
    for l in range(depth):
        w = _prep_weights(w_in[l], norm1_g[l], qnorm_g[l], knorm_g[l], w_up_a[l], w_up_b[l], w_out[l], norm2_g[l],
                          w_rg[l], b_rg[l], w_re[l], b_re[l], w_e_gate[l], w_e_up[l], w_e_down[l])
        mod = _ada(c_all, w_ada[l], b_ada[l].reshape(1, -1))
        mod_p = mod[:bp].reshape(bp, 6, D_MODEL)
        mod_s = mod[bp:bp + bs].reshape(bs, 6, D_MODEL)
        y_p, new_p = _layer(y_p, mod_p, empty_a, empty_a, empty_i, empty_b, empty_b, w,
                            cfg=_group_cfg(y_p.shape[1]))
        y_s, new_s = _layer(y_s, mod_s, cache_a_k[l], cache_a_v[l], cache_a_kidx[l], cache_b_k[l], cache_b_v[l], w,
                            cfg=_group_cfg(y_s.shape[1]))
        rows_p.append(new_p)
        rows_s.append(new_s)
    stack = lambda rws, i: jnp.stack([r[i] for r in rws])
    return (y_p, y_s) + tuple(stack(rows_p, i) for i in range(5)) + tuple(stack(rows_s, i) for i in range(5))
```

```python
import functools

import jax
import jax.numpy as jnp
from jax import lax
from jax.experimental import pallas as pl
from jax.experimental.pallas import tpu as pltpu

D_MODEL = 1024
CHUNK = 64
CHUNK_SHIFT = 6
HEAD_DIM = 64
A_HEADS = 8
B_HEADS = 8
IDX_HEADS = 4
IDX_DIM = 64
TOPK_MAX = 256
ROPE_THETA = 500000.0
ROT_DIM = HEAD_DIM // 4
N_GROUPS = 4
EXPERTS_PER_GROUP = 8
N_EXPERTS = N_GROUPS * EXPERTS_PER_GROUP
D_EXPERT = 256
RMS_EPS = 1e-6
A_WIDTH = A_HEADS * HEAD_DIM
B_WIDTH = B_HEADS * HEAD_DIM

LANES = 128
CACHE_OUTS = ("ka", "va", "kb", "vb")
IDX_SEG = 512
PAIRS = 4
COUNT_ROWS = 64
ROUTER_LANES = 128
EXPERT_LANE0 = N_GROUPS
VMEM_LIMIT = 56 * 1024 * 1024

F32 = jnp.float32
BF16 = jnp.bfloat16
I32 = jnp.int32
NEG_BIG = -1e30
INT_MIN = -(2 ** 31)
KEY_NEG_INF = (0xFF800000 - (1 << 32)) ^ 0x7FFFFFFF


def _dot(a, b):
    return jnp.dot(a, b, preferred_element_type=F32)


def _dot_nt(a, b):
    return lax.dot_general(a, b, (((1,), (1,)), ((), ())), preferred_element_type=F32)


def _dot_tn(a, b):
    return lax.dot_general(a, b, (((0,), (0,)), ((), ())), preferred_element_type=F32)


def _split_bf16(a):
    hi = a.astype(BF16)
    lo = (a - hi.astype(F32)).astype(BF16)
    return hi, lo


def _dot3(a, b):
    ah, al = _split_bf16(a)
    bh, bl = _split_bf16(b)
    return _dot(ah, bh) + (_dot(ah, bl) + _dot(al, bh))


def _silu(x):
    return x * (1.0 / (1.0 + jnp.exp(-x)))


def _sigmoid(x):
    return 1.0 / (1.0 + jnp.exp(-x))


def _rows_times(x, vec, nb):
    if nb == 1:
        return x * vec
    tm, d = x.shape
    return (x.reshape(nb, tm // nb, d) * vec[:, None, :]).reshape(tm, d)


def _rows_plus(x, vec, nb):
    if nb == 1:
        return x + vec
    tm, d = x.shape
    return (x.reshape(nb, tm // nb, d) + vec[:, None, :]).reshape(tm, d)


def _rms_rows(x):
    return x * lax.rsqrt(jnp.mean(x * x, axis=-1, keepdims=True) + RMS_EPS)


def _ada_kernel(c_ref, w_ref, b_ref, o_ref):
    o_ref[...] = _dot3(_silu(c_ref[...]), w_ref[...]) + b_ref[...]


def _ada(c_all, w_ada, b_ada):
    rows = c_all.shape[0]
    n = w_ada.shape[1]
    tn = 1536
    return pl.pallas_call(
        _ada_kernel,
        grid=(n // tn,),
        in_specs=[pl.BlockSpec((rows, D_MODEL), lambda j: (0, 0)),
                  pl.BlockSpec((D_MODEL, tn), lambda j: (0, j)),
                  pl.BlockSpec((1, tn), lambda j: (0, j))],
        out_specs=pl.BlockSpec((rows, tn), lambda j: (0, j)),
        out_shape=jax.ShapeDtypeStruct((rows, n), F32),
        compiler_params=pltpu.CompilerParams(dimension_semantics=("arbitrary",), vmem_limit_bytes=VMEM_LIMIT),
        name="ada_mod",
    )(c_all, w_ada, b_ada)


def _rope_lanes(a, tab_ref, reps):
    width = a.shape[-1]
    cos = jnp.tile(tab_ref[0], (1, reps)) if reps > 1 else tab_ref[0]
    sin_lo = jnp.tile(tab_ref[1], (1, reps)) if reps > 1 else tab_ref[1]
    sin_hi = jnp.tile(tab_ref[2], (1, reps)) if reps > 1 else tab_ref[2]
    half = ROT_DIM // 2
    return a * cos + pltpu.roll(a, width - half, 1) * sin_lo + pltpu.roll(a, half, 1) * sin_hi


def _proj_kernel(x_ref, mod_ref, n1g_ref, wa_ref, wv_ref, wi_ref, wg_ref, bd_ref, gqk_ref, tab_ref,
                 qa_ref, ka_ref, kab_ref, va_ref, vab_ref, qb_ref, kb_ref, kbb_ref, vb_ref, vbb_ref,
                 qi_ref, ki2_ref, ki2b_ref, wi_out_ref, gates_ref, *, nb):
    x = x_ref[...]
    sh1 = mod_ref[:, 0, :]
    sc1 = mod_ref[:, 1, :]
    h = _rows_plus(_rows_times(_rms_rows(x) * n1g_ref[...], 1.0 + sc1, nb), sh1, nb)
    hb = h.astype(BF16)

    a = _dot(hb, wa_ref[...])
    ms = _dot((a * a).astype(BF16), bd_ref[...])
    a = a * lax.rsqrt(ms + RMS_EPS) * gqk_ref[...]
    a = _rope_lanes(a, tab_ref, a.shape[-1] // LANES)
    qa_ref[...] = (a[:, :A_WIDTH] * (HEAD_DIM ** -0.5)).astype(BF16)
    ka = a[:, A_WIDTH:]
    ka_ref[...] = ka.reshape(ka_ref.shape)
    kab_ref[...] = ka.astype(BF16)

    v = _dot(hb, wv_ref[...])
    va = v[:, :A_WIDTH]
    va_ref[...] = va.reshape(va_ref.shape)
    vab_ref[...] = va.astype(BF16)
    qb_ref[...] = (v[:, A_WIDTH:A_WIDTH + B_WIDTH] * (HEAD_DIM ** -0.5)).astype(BF16)
    kb = v[:, A_WIDTH + B_WIDTH:A_WIDTH + 2 * B_WIDTH]
    kb_ref[...] = kb.reshape(kb_ref.shape)
    kbb_ref[...] = kb.astype(BF16)
    vb = v[:, A_WIDTH + 2 * B_WIDTH:]
    vb_ref[...] = vb.reshape(vb_ref.shape)
    vbb_ref[...] = vb.astype(BF16)

    i = _dot(hb, wi_ref[...])
    nr = IDX_SEG - LANES
    ir = _rope_lanes(i[:, :nr], tab_ref, nr // LANES)
    nq = IDX_HEADS * IDX_DIM
    qi_ref[...] = (ir[:, :nq] * (IDX_DIM ** -0.5)).astype(BF16)
    ki2 = ir[:, nq:]
    ki2_ref[...] = ki2
    ki2b_ref[...] = ki2.astype(BF16)
    wi_out_ref[...] = i[:, nr:]

    gates_ref[...] = _sigmoid(_dot(hb, wg_ref[...])).astype(BF16)


def _proj(x2, mod, n1g, wa, wv, wi, wg, bd, gqk, tab, *, t_len, tm):
    n = x2.shape[0]
    nb = max(1, tm // t_len)
    tiles_per_batch = max(1, t_len // tm)
    tab_tiles = tab.shape[1] // tm
    row = lambda i: (i, 0)
    const2 = lambda i: (0, 0)
    tabmap = lambda i: (0, i % tab_tiles, 0)
    in_specs = [
        pl.BlockSpec((tm, D_MODEL), row),
        pl.BlockSpec((nb, 6, D_MODEL), lambda i: (i // tiles_per_batch, 0, 0)),
        pl.BlockSpec((1, D_MODEL), const2),
        pl.BlockSpec(wa.shape, const2),
        pl.BlockSpec(wv.shape, const2),
        pl.BlockSpec(wi.shape, const2),
        pl.BlockSpec(wg.shape, const2),
        pl.BlockSpec(bd.shape, const2),
        pl.BlockSpec((1, 2 * A_WIDTH), const2),
        pl.BlockSpec((3, tm, LANES), tabmap),
    ]
    outs = [
        ("qa", A_WIDTH, BF16), ("ka", A_WIDTH, F32), ("kab", A_WIDTH, BF16),
        ("va", A_WIDTH, F32), ("vab", A_WIDTH, BF16),
        ("qb", B_WIDTH, BF16), ("kb", B_WIDTH, F32), ("kbb", B_WIDTH, BF16),
        ("vb", B_WIDTH, F32), ("vbb", B_WIDTH, BF16),
        ("qi", IDX_HEADS * IDX_DIM, BF16), ("ki2", LANES, F32), ("ki2b", LANES, BF16), ("wi", LANES, F32),
        ("gates", 2 * D_MODEL, BF16),
    ]
    res = pl.pallas_call(
        functools.partial(_proj_kernel, nb=nb),
        grid=(n // tm,),
        in_specs=in_specs,
        out_specs=[pl.BlockSpec((tm, w // HEAD_DIM, HEAD_DIM), lambda i: (i, 0, 0)) if name in CACHE_OUTS
                   else pl.BlockSpec((tm, w), row) for name, w, _ in outs],
        out_shape=[jax.ShapeDtypeStruct((n, w // HEAD_DIM, HEAD_DIM) if name in CACHE_OUTS else (n, w), dt)
                   for name, w, dt in outs],
        compiler_params=pltpu.CompilerParams(dimension_semantics=("arbitrary",), vmem_limit_bytes=VMEM_LIMIT),
        name="in_proj",
    )(x2, mod, n1g, wa, wv, wi, wg, bd, gqk, tab)
    return dict(zip([o[0] for o in outs], res))


def _lower_tri(n):
    idx = jnp.arange(n)
    return (idx[:, None] >= idx[None, :]).astype(BF16)


STAGE_ROWS = 512


def _key_source(new_ref, past_ref, stage):
    if past_ref is None:
        return new_ref.at[0]
    past, t_new, s_pad = past_ref.shape[1], new_ref.shape[1], stage.shape[0]

    def copy(c, carry):
        r0 = pl.multiple_of(c * STAGE_ROWS, STAGE_ROWS)
        stage[pl.ds(r0, STAGE_ROWS), :] = past_ref[0, pl.ds(r0, STAGE_ROWS), :]
        return carry

    whole = past // STAGE_ROWS * STAGE_ROWS
    lax.fori_loop(0, past // STAGE_ROWS, copy, 0)
    if past > whole:
        stage[whole:past, :] = past_ref[0, whole:past, :]
    stage[past:past + t_new, :] = new_ref[0]
    if s_pad > past + t_new:
        stage[past + t_new:, :] = jnp.zeros((s_pad - past - t_new, stage.shape[1]), stage.dtype)
    return stage


def _head_halves(x):
    lane = lax.broadcasted_iota(I32, x.shape, 1)
    zero = jnp.zeros_like(x)
    return jnp.where(lane < HEAD_DIM, x, zero), jnp.where(lane >= HEAD_DIM, x, zero)


def _dsa_kernel(*refs, tq, kb, s_real, past, topk):
    qi_ref, w_ref, qa_ref, ki_ref, ka_ref, va_ref = refs[:6]
    if past:
        pki_ref, pka_ref, pva_ref, tri_ref, o_ref = refs[6:11]
        s_scr, bias_scr, m_scr, l_scr, acc_scr, ki_st, ka_st, va_st = refs[11:]
    else:
        pki_ref = pka_ref = pva_ref = ki_st = ka_st = va_st = None
        tri_ref, o_ref, s_scr, bias_scr, m_scr, l_scr, acc_scr = refs[6:]
    ki_all = _key_source(ki_ref, pki_ref, ki_st)
    ka_all = _key_source(ka_ref, pka_ref, ka_st)
    va_all = _key_source(va_ref, pva_ref, va_st)
    qt = pl.program_id(1)
    q0 = past + qt * tq
    kmax = jnp.minimum((lax.shift_right_logical(q0 + tq - 1, CHUNK_SHIFT) + 1) * CHUNK, s_real)
    nblk = (kmax + kb - 1) // kb
    qchunk = lax.shift_right_logical(q0 + lax.broadcasted_iota(I32, (1, tq), 1), CHUNK_SHIFT)
    kf = float(topk)
    chunks = lambda ref, i: ref[0, :, i * LANES:(i + 1) * LANES]
    qi_heads = [q for i in range(IDX_HEADS // 2) for q in _head_halves(chunks(qi_ref, i))]
    qa_heads = [q for i in range(PAIRS) for q in _head_halves(chunks(qa_ref, i))]
    w_rows = w_ref[0].T

    def fold_rows(x, op):
        part = op(x.reshape(kb // COUNT_ROWS, COUNT_ROWS, tq), axis=0)
        return op(part, axis=0, keepdims=True)

    def score_block(j, masked):
        r0 = pl.multiple_of(j * kb, kb)
        kib = ki_all[pl.ds(r0, kb), :]
        acc = jnp.zeros((kb, tq), F32)
        for h in range(IDX_HEADS):
            rel = jnp.maximum(_dot_nt(kib, qi_heads[h]), 0.0)
            acc = acc + w_rows[h:h + 1, :] * rel
        s = acc + 0.0
        if masked:
            kpos = r0 + lax.broadcasted_iota(I32, (kb, tq), 0)
            adm = (lax.shift_right_logical(kpos, CHUNK_SHIFT) <= qchunk) & (kpos < s_real)
            s = jnp.where(adm, s, -jnp.inf)
        s_scr[pl.ds(r0, kb), :] = s

    n_open = jnp.minimum((lax.shift_right_logical(q0, CHUNK_SHIFT) + 1) * CHUNK, s_real) // kb

    def open_step(j, c):
        score_block(j, False)
        return c

    def masked_step(j, c):
        score_block(j, True)
        return c

    lax.fori_loop(0, n_open, open_step, 0)
    lax.fori_loop(n_open, nblk, masked_step, 0)

    def count(pred):
        def body(j, acc):
            r0 = pl.multiple_of(j * kb, kb)
            hit = jnp.where(pred(s_scr[pl.ds(r0, kb), :]), 1.0, 0.0)
            return acc + jnp.sum(hit.reshape(kb // COUNT_ROWS, COUNT_ROWS, tq), axis=0)
        acc = lax.fori_loop(0, nblk, body, jnp.zeros((COUNT_ROWS, tq), F32))
        return jnp.sum(acc, axis=0, keepdims=True)

    def key_to_score(key):
        return pltpu.bitcast(jnp.where(key < 0, key ^ 0x7FFFFFFF, key), F32)

    c0 = count(lambda blk: blk >= 0.0)
    t0 = jnp.where(c0 >= kf, 0, INT_MIN).astype(I32)

    def value_bit(i, t):
        cand = t | jnp.left_shift(jnp.int32(1), 30 - i)
        cand_f = key_to_score(cand)
        return jnp.where(count(lambda blk: blk >= cand_f) >= kf, cand, t)

    t_key = lax.fori_loop(0, 31, value_bit, t0)
    t = key_to_score(jnp.maximum(t_key, KEY_NEG_INF))

    need = kf - count(lambda blk: blk > t)
    tri = tri_ref[...]

    def select_block(j, seen):
        r0 = pl.multiple_of(j * kb, kb)
        blk = s_scr[pl.ds(r0, kb), :]
        tie = jnp.where((blk == t) & (blk > -jnp.inf), 1.0, 0.0)
        rank = _dot(tri, tie.astype(BF16)) + seen
        chosen = (blk > t) | ((tie > 0.0) & (rank <= need))
        bias_scr[pl.ds(r0, kb), :] = jnp.where(chosen, 0.0, NEG_BIG)
        return seen + fold_rows(tie, jnp.sum)

    lax.fori_loop(0, nblk, select_block, jnp.zeros((1, tq), F32))

    m_scr[...] = jnp.full(m_scr.shape, NEG_BIG, F32)
    l_scr[...] = jnp.zeros(l_scr.shape, F32)
    acc_scr[...] = jnp.zeros(acc_scr.shape, F32)

    def attend_block(j, carry):
        r0 = pl.multiple_of(j * kb, kb)
        bias = bias_scr[pl.ds(r0, kb), :]
        heads = range(A_HEADS)
        m_old = m_scr[...]
        l_old = l_scr[...]
        k_pair = [ka_all[pl.ds(r0, kb), i * LANES:(i + 1) * LANES] for i in range(PAIRS)]
        v_pair = [va_all[pl.ds(r0, kb), i * LANES:(i + 1) * LANES] for i in range(PAIRS)]
        ss = [_dot_nt(k_pair[h // 2], qa_heads[h]) + bias for h in heads]
        m_new = [jnp.maximum(m_old[h:h + 1, :], fold_rows(ss[h], jnp.max)) for h in heads]
        ps = [jnp.exp(ss[h] - m_new[h]) for h in heads]
        pvs = [_dot_tn(v_pair[h // 2], ps[h].astype(BF16)) for h in heads]
        alphas = [jnp.exp(m_old[h:h + 1, :] - m_new[h]) for h in heads]
        for i in range(PAIRS):
            e, o = 2 * i, 2 * i + 1
            acc_scr[i] = (jnp.where(even_rows, alphas[e], alphas[o]) * acc_scr[i]
                          + jnp.where(even_rows, pvs[e], pvs[o]))
        for h in heads:
            l_scr[h:h + 1, :] = alphas[h] * l_old[h:h + 1, :] + fold_rows(ps[h], jnp.sum)
            m_scr[h:h + 1, :] = m_new[h]
        return carry

    even_rows = lax.broadcasted_iota(I32, (LANES, tq), 0) < HEAD_DIM
    lax.fori_loop(0, nblk, attend_block, 0)

    l_fin = l_scr[...]
    for i in range(PAIRS):
        denom = jnp.where(even_rows, l_fin[2 * i:2 * i + 1, :], l_fin[2 * i + 1:2 * i + 2, :])
        o_ref[0, :, i * LANES:(i + 1) * LANES] = (acc_scr[i] / denom).T.astype(o_ref.dtype)


def _key_specs(new, cached):
    kmap = lambda b, q: (b, 0, 0)
    return [pl.BlockSpec((1,) + x.shape[1:], kmap) for x in list(new) + list(cached)]


def _dsa(qi, wi, qa, new, cached, *, tq, kb, s_real, topk):
    bsz, t_len, _ = qa.shape
    past = cached[0].shape[1] if cached else 0
    s_pad = _round_up(s_real, kb)
    assert cached or s_pad == s_real
    kern = functools.partial(_dsa_kernel, tq=tq, kb=kb, s_real=s_real, past=past, topk=topk)
    qmap = lambda b, q: (b, q, 0)
    stage = [pltpu.VMEM((s_pad, x.shape[-1]), BF16) for x in cached]
    return pl.pallas_call(
        kern,
        grid=(bsz, t_len // tq),
        in_specs=[
            pl.BlockSpec((1, tq, IDX_HEADS * IDX_DIM), qmap),
            pl.BlockSpec((1, tq, LANES), qmap),
            pl.BlockSpec((1, tq, A_WIDTH), qmap),
        ] + _key_specs(new, cached) + [pl.BlockSpec((kb, kb), lambda b, q: (0, 0))],
        out_specs=pl.BlockSpec((1, tq, A_WIDTH), qmap),
        out_shape=jax.ShapeDtypeStruct((bsz, t_len, A_WIDTH), BF16),
        scratch_shapes=[
            pltpu.VMEM((s_pad, tq), F32),
            pltpu.VMEM((s_pad, tq), F32),
            pltpu.VMEM((A_HEADS, tq), F32),
            pltpu.VMEM((A_HEADS, tq), F32),
            pltpu.VMEM((PAIRS, LANES, tq), F32),
        ] + stage,
        compiler_params=pltpu.CompilerParams(dimension_semantics=("arbitrary", "arbitrary"),
                                             vmem_limit_bytes=VMEM_LIMIT),
        name="dsa_attention",
    )(qi, wi, qa, *new, *cached, _lower_tri(kb))


def _stick_kernel(*refs, tq, kb, past):
    q_ref, k_ref, v_ref = refs[:3]
    if past:
        pk_ref, pv_ref, tri_ref, o_ref, carry_scr, acc_scr, k_st, v_st = refs[3:]
    else:
        pk_ref = pv_ref = k_st = v_st = None
        tri_ref, o_ref, carry_scr, acc_scr = refs[3:]
    k_all = _key_source(k_ref, pk_ref, k_st)
    v_all = _key_source(v_ref, pv_ref, v_st)
    qt = pl.program_id(1)
    q0 = past + qt * tq
    nblk = (q0 + tq - 1 + kb - 1) // kb
    n_full = q0 // kb
    qpos = q0 + lax.broadcasted_iota(I32, (tq, kb), 0)
    carry_scr[...] = jnp.zeros(carry_scr.shape, F32)
    acc_scr[...] = jnp.zeros(acc_scr.shape, F32)
    tri2 = tri_ref[...]
    heads = range(B_HEADS)
    q_heads = [q for i in range(PAIRS) for q in _head_halves(q_ref[0, :, i * LANES:(i + 1) * LANES])]
    even_lanes = lax.broadcasted_iota(I32, (tq, LANES), 1) < HEAD_DIM

    def block(j, masked):
        r0 = pl.multiple_of(j * kb, kb)
        causal = ((r0 + lax.broadcasted_iota(I32, (tq, kb), 1)) < qpos) if masked else None
        carries = [carry_scr[h] for h in heads]
        k_pair = [k_all[pl.ds(r0, kb), i * LANES:(i + 1) * LANES] for i in range(PAIRS)]
        v_pair = [v_all[pl.ds(r0, kb), i * LANES:(i + 1) * LANES] for i in range(PAIRS)]
        zs = [_dot_nt(q_heads[h], k_pair[h // 2]) for h in heads]
        sps = []
        for z in zs:
            sp = jnp.maximum(z, 0.0) + jnp.log(1.0 + jnp.exp(-jnp.abs(z)))
            sps.append(jnp.where(causal, sp, 0.0) if masked else sp)
        sufs = []
        for h in heads:
            hi, lo = _split_bf16(sps[h])
            hilo = jnp.concatenate([hi, lo], axis=1)
            sufs.append(_dot(hilo, tri2) + jnp.tile(carries[h], (1, kb // LANES)))
        pvs = []
        for h in heads:
            a = jnp.exp(zs[h] - sufs[h])
            if masked:
                a = jnp.where(causal, a, 0.0)
            pvs.append(_dot(a.astype(BF16), v_pair[h // 2]))
        for i in range(PAIRS):
            acc_scr[i] = acc_scr[i] + jnp.where(even_lanes, pvs[2 * i], pvs[2 * i + 1])
        for h in heads:
            carry_scr[h] = carries[h] + jnp.sum(sps[h], axis=-1, keepdims=True)

    def masked_step(jj, c):
        block(nblk - 1 - jj, True)
        return c

    def full_step(jj, c):
        block(n_full - 1 - jj, False)
        return c

    lax.fori_loop(0, nblk - n_full, masked_step, 0)
    lax.fori_loop(0, n_full, full_step, 0)
    for i in range(PAIRS):
        o_ref[0, :, i * LANES:(i + 1) * LANES] = acc_scr[i].astype(o_ref.dtype)


def _stick(q, new, cached, *, tq, kb, s_real):
    bsz, t_len, _ = q.shape
    past = cached[0].shape[1] if cached else 0
    s_pad = _round_up(s_real, kb)
    assert cached or s_pad == s_real
    kern = functools.partial(_stick_kernel, tq=tq, kb=kb, past=past)
    stage = [pltpu.VMEM((s_pad, x.shape[-1]), BF16) for x in cached]
    return pl.pallas_call(
        kern,
        grid=(bsz, t_len // tq),
        in_specs=[pl.BlockSpec((1, tq, B_WIDTH), lambda b, q: (b, q, 0))] + _key_specs(new, cached)
        + [pl.BlockSpec((2 * kb, kb), lambda b, q: (0, 0))],
        out_specs=pl.BlockSpec((1, tq, B_WIDTH), lambda b, q: (b, q, 0)),
        out_shape=jax.ShapeDtypeStruct((bsz, t_len, B_WIDTH), BF16),
        scratch_shapes=[
            pltpu.VMEM((B_HEADS, tq, LANES), F32),
            pltpu.VMEM((PAIRS, tq, LANES), F32),
        ] + stage,
        compiler_params=pltpu.CompilerParams(dimension_semantics=("arbitrary", "arbitrary"),
                                             vmem_limit_bytes=VMEM_LIMIT),
        name="stick_attention",
    )(q, *new, *cached, jnp.tile(_lower_tri(kb), (2, 1)))


def _merge_kernel(x_ref, oa_ref, ob_ref, gates_ref, mod_ref, wua_ref, wub_ref, wout_ref, n2g_ref, wr_ref, br_ref,
                  y_ref, h2_ref, comb_ref, *, nb):
    g1 = mod_ref[:, 2, :]
    sh2 = mod_ref[:, 3, :]
    sc2 = mod_ref[:, 4, :]
    gates = gates_ref[...].astype(F32)
    merged = gates[:, :D_MODEL] * _dot(oa_ref[...], wua_ref[...]) + gates[:, D_MODEL:] * _dot(ob_ref[...], wub_ref[...])
    y = x_ref[...] + _rows_times(_dot(merged.astype(BF16), wout_ref[...]), g1, nb)
    y_ref[...] = y
    h2 = _rows_plus(_rows_times(_rms_rows(y) * n2g_ref[...], 1.0 + sc2, nb), sh2, nb)
    h2_ref[...] = h2.astype(BF16)

    logits = _dot3(h2, wr_ref[...]) + br_ref[...]
    lane = lax.broadcasted_iota(I32, logits.shape, 1).astype(F32)
    far = float(ROUTER_LANES)
    is_g = lane < float(N_GROUPS)
    gl = jnp.where(is_g, logits, -jnp.inf)
    gmax = jnp.max(gl, axis=-1, keepdims=True)
    gsel = jnp.min(jnp.where(is_g & (gl == gmax), lane, far), axis=-1, keepdims=True)
    p_group = 1.0 / jnp.sum(jnp.where(is_g, jnp.exp(gl - gmax), 0.0), axis=-1, keepdims=True)
    e_lo = float(EXPERT_LANE0) + gsel * float(EXPERTS_PER_GROUP)
    in_grp = (lane >= e_lo) & (lane < e_lo + float(EXPERTS_PER_GROUP))
    el = jnp.where(in_grp, logits, -jnp.inf)
    v1 = jnp.max(el, axis=-1, keepdims=True)
    i1 = jnp.min(jnp.where(in_grp & (el == v1), lane, far), axis=-1, keepdims=True)
    el2 = jnp.where(lane == i1, -jnp.inf, el)
    v2 = jnp.max(el2, axis=-1, keepdims=True)
    i2 = jnp.min(jnp.where(in_grp & (lane != i1) & (el2 == v2), lane, far), axis=-1, keepdims=True)
    e2 = jnp.exp(v2 - v1)
    w1 = p_group / (1.0 + e2)
    w2 = p_group * e2 / (1.0 + e2)
    comb_ref[...] = jnp.where(lane == i1, w1, 0.0) + jnp.where(lane == i2, w2, 0.0)


def _merge(x2, oa, ob, gates, mod, wua, wub, wout, n2g, wr, br, *, t_len, tm):
    n = x2.shape[0]
    nb = max(1, tm // t_len)
    tiles_per_batch = max(1, t_len // tm)
    row = lambda i: (i, 0)
    const2 = lambda i: (0, 0)
    return pl.pallas_call(
        functools.partial(_merge_kernel, nb=nb),
        grid=(n // tm,),
        in_specs=[
            pl.BlockSpec((tm, D_MODEL), row),
            pl.BlockSpec((tm, A_WIDTH), row),
            pl.BlockSpec((tm, B_WIDTH), row),
            pl.BlockSpec((tm, 2 * D_MODEL), row),
            pl.BlockSpec((nb, 6, D_MODEL), lambda i: (i // tiles_per_batch, 0, 0)),
            pl.BlockSpec(wua.shape, const2),
            pl.BlockSpec(wub.shape, const2),
            pl.BlockSpec(wout.shape, const2),
            pl.BlockSpec((1, D_MODEL), const2),
            pl.BlockSpec(wr.shape, const2),
            pl.BlockSpec((1, ROUTER_LANES), const2),
        ],
        out_specs=[pl.BlockSpec((tm, D_MODEL), row), pl.BlockSpec((tm, D_MODEL), row),
                   pl.BlockSpec((tm, ROUTER_LANES), row)],
        out_shape=[jax.ShapeDtypeStruct((n, D_MODEL), F32), jax.ShapeDtypeStruct((n, D_MODEL), BF16),
                   jax.ShapeDtypeStruct((n, ROUTER_LANES), F32)],
        compiler_params=pltpu.CompilerParams(dimension_semantics=("arbitrary",), vmem_limit_bytes=VMEM_LIMIT),
        name="merge_router",
    )(x2, oa, ob, gates, mod, wua, wub, wout, n2g, wr, br)


def _moe_kernel(y_ref, h2_ref, comb_ref, mod_ref, wg_ref, wu_ref, wd_ref, o_ref, acc_scr, *, nb):
    e = pl.program_id(1)

    @pl.when(e == 0)
    def _():
        acc_scr[...] = jnp.zeros(acc_scr.shape, F32)

    h2 = h2_ref[...]
    act = _silu(_dot(h2, wg_ref[0].astype(BF16))) * _dot(h2, wu_ref[0].astype(BF16))
    contrib = _dot(act.astype(BF16), wd_ref[0].astype(BF16))
    comb = comb_ref[...]
    lane = lax.broadcasted_iota(I32, comb.shape, 1)
    w_e = jnp.sum(jnp.where(lane == e + EXPERT_LANE0, comb, 0.0), axis=-1, keepdims=True)
    acc_scr[...] += w_e * contrib

    @pl.when(e == pl.num_programs(1) - 1)
    def _():
        o_ref[...] = y_ref[...] + _rows_times(acc_scr[...], mod_ref[:, 5, :], nb)


def _moe(y, h2, comb, mod, w_gate, w_up, w_down, *, t_len, tm):
    n = y.shape[0]
    nb = max(1, tm // t_len)
    tiles_per_batch = max(1, t_len // tm)
    row = lambda i, e: (i, 0)
    return pl.pallas_call(
        functools.partial(_moe_kernel, nb=nb),
        grid=(n // tm, N_EXPERTS),
        in_specs=[
            pl.BlockSpec((tm, D_MODEL), row),
            pl.BlockSpec((tm, D_MODEL), row),
            pl.BlockSpec((tm, ROUTER_LANES), row),
            pl.BlockSpec((nb, 6, D_MODEL), lambda i, e: (i // tiles_per_batch, 0, 0)),
            pl.BlockSpec((1, D_MODEL, D_EXPERT), lambda i, e: (e, 0, 0)),
            pl.BlockSpec((1, D_MODEL, D_EXPERT), lambda i, e: (e, 0, 0)),
            pl.BlockSpec((1, D_EXPERT, D_MODEL), lambda i, e: (e, 0, 0)),
        ],
        out_specs=pl.BlockSpec((tm, D_MODEL), row),
        out_shape=jax.ShapeDtypeStruct((n, D_MODEL), F32),
        scratch_shapes=[pltpu.VMEM((tm, D_MODEL), F32)],
        compiler_params=pltpu.CompilerParams(dimension_semantics=("arbitrary", "arbitrary"),
                                             vmem_limit_bytes=VMEM_LIMIT),
        name="moe_dense",
    )(y, h2, comb, mod, w_gate, w_up, w_down)


def _rope_tables(pos):
    half = ROT_DIM // 2
    freqs = ROPE_THETA ** (-jnp.arange(0, ROT_DIM, 2, dtype=F32) / ROT_DIM)
    ang = pos.astype(F32)[:, None] * freqs[None, :]
    cos, sin = jnp.cos(ang), jnp.sin(ang)
    dh = jnp.arange(LANES) % HEAD_DIM
    fi = dh % half
    cos_t = jnp.where(dh < ROT_DIM, cos[:, fi], 1.0)
    sin_lo = jnp.where(dh < half, -sin[:, fi], 0.0)
    sin_hi = jnp.where((dh >= half) & (dh < ROT_DIM), sin[:, fi], 0.0)
    return jnp.stack([cos_t, sin_lo, sin_hi]).astype(F32)


def _round_up(a, b):
    return (a + b - 1) // b * b


def _layer(x, mod, past_ak, past_av, past_aki, past_bk, past_bv, w, *, cfg):
    bsz, t_len, _ = x.shape
    past = past_ak.shape[1]
    s_real = past + t_len
    n = bsz * t_len
    x2 = x.reshape(n, D_MODEL)
    pos = past + jnp.arange(t_len, dtype=jnp.int32)

    tm = cfg["tm"]
    tab = jnp.tile(_rope_tables(pos), (1, max(1, tm // t_len), 1))
    pr = _proj(x2, mod, w["n1g"], w["wa"], w["wv"], w["wi"], w["wg"], w["bd"], w["gqk"], tab, t_len=t_len, tm=tm)

    new_ak = pr["ka"].reshape(bsz, t_len, A_HEADS, HEAD_DIM)
    new_av = pr["va"].reshape(bsz, t_len, A_HEADS, HEAD_DIM)
    new_aki = pr["ki2"][:, :IDX_DIM].reshape(bsz, t_len, IDX_DIM)
    new_bk = pr["kb"].reshape(bsz, t_len, B_HEADS, HEAD_DIM)
    new_bv = pr["vb"].reshape(bsz, t_len, B_HEADS, HEAD_DIM)

    def per_batch(x2):
        return x2.reshape(bsz, t_len, x2.shape[-1])

    def cached(*caches):
        return tuple(c.reshape(bsz, past, -1).astype(BF16) for c in caches) if past else ()

    topk = min(TOPK_MAX, s_real // 4)
    past_ki2 = jnp.concatenate([past_aki, past_aki], axis=-1)
    oa = _dsa(per_batch(pr["qi"]), per_batch(pr["wi"]), per_batch(pr["qa"]),
              (per_batch(pr["ki2b"]), per_batch(pr["kab"]), per_batch(pr["vab"])),
              cached(past_ki2, past_ak, past_av),
              tq=cfg["dsa_tq"], kb=cfg["dsa_kb"], s_real=s_real, topk=topk)

    ob = _stick(per_batch(pr["qb"]), (per_batch(pr["kbb"]), per_batch(pr["vbb"])), cached(past_bk, past_bv),
                tq=cfg["sb_tq"], kb=cfg["sb_kb"], s_real=s_real)

    y, h2, comb = _merge(x2, oa.reshape(n, A_WIDTH), ob.reshape(n, B_WIDTH), pr["gates"], mod,
                         w["wua"], w["wub"], w["wout"], w["n2g"], w["wr"], w["br"], t_len=t_len, tm=cfg["tm_merge"])
    out = _moe(y, h2, comb, mod, w["w_gate"], w["w_up"], w["w_down"], t_len=t_len, tm=cfg["tm_moe"])
    return out.reshape(bsz, t_len, D_MODEL), (new_ak, new_av, new_aki, new_bk, new_bv)


def _prep_weights(w_in, norm1_g, qnorm_g, knorm_g, w_up_a, w_up_b, w_out, norm2_g, w_rg, b_rg, w_re, b_re,
                  w_e_gate, w_e_up, w_e_down):
    o = 0
    seg = {}
    for name, width in (("qa", A_WIDTH), ("ka", A_WIDTH), ("va", A_WIDTH), ("qi", IDX_HEADS * IDX_DIM),
                        ("ki", IDX_DIM), ("wi", IDX_HEADS), ("qb", B_WIDTH), ("kb", B_WIDTH), ("vb", B_WIDTH),
                        ("gates", 2 * D_MODEL)):
        seg[name] = w_in[:, o:o + width]
        o += width
    pad_i = jnp.zeros((D_MODEL, LANES - IDX_HEADS), w_in.dtype)
    hd = jnp.arange(2 * A_WIDTH) // HEAD_DIM
    n_r = N_GROUPS + N_EXPERTS
    return dict(
        n1g=norm1_g.reshape(1, D_MODEL),
        wa=jnp.concatenate([seg["qa"], seg["ka"]], axis=1).astype(BF16),
        wv=jnp.concatenate([seg["va"], seg["qb"], seg["kb"], seg["vb"]], axis=1).astype(BF16),
        wi=jnp.concatenate([seg["qi"], seg["ki"], seg["ki"], seg["wi"], pad_i], axis=1).astype(BF16),
        wg=seg["gates"].astype(BF16),
        bd=((hd[:, None] == hd[None, :]).astype(F32) / HEAD_DIM).astype(BF16),
        gqk=jnp.concatenate([jnp.tile(qnorm_g, A_HEADS), jnp.tile(knorm_g, A_HEADS)]).reshape(1, 2 * A_WIDTH),
        wua=w_up_a.astype(BF16), wub=w_up_b.astype(BF16), wout=w_out.astype(BF16),
        n2g=norm2_g.reshape(1, D_MODEL),
        wr=jnp.pad(jnp.concatenate([w_rg, w_re], axis=1), ((0, 0), (0, ROUTER_LANES - n_r))),
        br=jnp.pad(jnp.concatenate([b_rg, b_re]), (0, ROUTER_LANES - n_r)).reshape(1, ROUTER_LANES),
        w_gate=w_e_gate, w_up=w_e_up, w_down=w_e_down,
    )


def _group_cfg(t_len):
    if t_len >= 512:
        return dict(tm=256, tm_merge=256, tm_moe=1024, dsa_tq=256, dsa_kb=512, sb_tq=256, sb_kb=256)
    return dict(tm=512, tm_merge=512, tm_moe=512, dsa_tq=t_len, dsa_kb=384, sb_tq=t_len, sb_kb=256)


def kernel(x_prompt, x_sample, cache_a_k, cache_a_v, cache_a_kidx, cache_b_k, cache_b_v, c_prompt, c_sample,
           w_ada, b_ada, norm1_g, w_in, qnorm_g, knorm_g, w_up_a, w_up_b, w_out, norm2_g,
           w_rg, b_rg, w_re, b_re, w_e_gate, w_e_up, w_e_down):
    depth = w_ada.shape[0]
    bp, bs = x_prompt.shape[0], x_sample.shape[0]
    dt = x_prompt.dtype
    rows = _round_up(bp + bs, 8)
    empty_a = jnp.zeros((bp, 0, A_HEADS, HEAD_DIM), dt)
    empty_i = jnp.zeros((bp, 0, IDX_DIM), dt)
    empty_b = jnp.zeros((bp, 0, B_HEADS, HEAD_DIM), dt)
    y_p, y_s = x_prompt, x_sample
    rows_p, rows_s = [], []
    c_all = jnp.concatenate([c_prompt, c_sample, jnp.zeros((rows - bp - bs, D_MODEL), dt)], axis=0)
    for l in range(depth):
        w = _prep_weights(w_in[l], norm1_g[l], qnorm_g[l], knorm_g[l], w_up_a[l], w_up_b[l], w_out[l], norm2_g[l],
                          w_rg[l], b_rg[l], w_re[l], b_re[l], w_e_gate[l], w_e_up[l], w_e_down[l])
        mod = _ada(c_all, w_ada[l], b_ada[l].reshape(1, -1))
        mod_p = mod[:bp].reshape(bp, 6, D_MODEL)
        mod_s = mod[bp:bp + bs].reshape(bs, 6, D_MODEL)
        y_p, new_p = _layer(y_p, mod_p, empty_a, empty_a, empty_i, empty_b, empty_b, w,
                            cfg=_group_cfg(y_p.shape[1]))
        y_s, new_s = _layer(y_s, mod_s, cache_a_k[l], cache_a_v[l], cache_a_kidx[l], cache_b_k[l], cache_b_v[l], w,
                            cfg=_group_cfg(y_s.shape[1]))
        rows_p.append(new_p)
        rows_s.append(new_s)
    stack = lambda rws, i: jnp.stack([r[i] for r in rws])
    return (y_p, y_s) + tuple(stack(rows_p, i) for i in range(5)) + tuple(stack(rows_s, i) for i in range(5))
```

```python
import functools

import jax
import jax.numpy as jnp
from jax import lax
from jax.experimental import pallas as pl
from jax.experimental.pallas import tpu as pltpu

D_MODEL = 1024
CHUNK = 64
CHUNK_SHIFT = 6
HEAD_DIM = 64
A_HEADS = 8
B_HEADS = 8
IDX_HEADS = 4
IDX_DIM = 64
TOPK_MAX = 256
ROPE_THETA = 500000.0
ROT_DIM = HEAD_DIM // 4
N_GROUPS = 4
EXPERTS_PER_GROUP = 8
N_EXPERTS = N_GROUPS * EXPERTS_PER_GROUP
D_EXPERT = 256
RMS_EPS = 1e-6
A_WIDTH = A_HEADS * HEAD_DIM
B_WIDTH = B_HEADS * HEAD_DIM

LANES = 128
CACHE_OUTS = ("ka", "va", "kb", "vb")
IDX_SEG = 512
PAIRS = 4
COUNT_ROWS = 64
ROUTER_LANES = 128
EXPERT_LANE0 = N_GROUPS
VMEM_LIMIT = 56 * 1024 * 1024

F32 = jnp.float32
BF16 = jnp.bfloat16
I32 = jnp.int32
NEG_BIG = -1e30
INT_MIN = -(2 ** 31)
KEY_NEG_INF = (0xFF800000 - (1 << 32)) ^ 0x7FFFFFFF


def _dot(a, b):
    return jnp.dot(a, b, preferred_element_type=F32)


def _dot_nt(a, b):
    return lax.dot_general(a, b, (((1,), (1,)), ((), ())), preferred_element_type=F32)


def _dot_tn(a, b):
    return lax.dot_general(a, b, (((0,), (0,)), ((), ())), preferred_element_type=F32)


def _split_bf16(a):
    hi = a.astype(BF16)
    lo = (a - hi.astype(F32)).astype(BF16)
    return hi, lo


def _dot3(a, b):
    ah, al = _split_bf16(a)
    bh, bl = _split_bf16(b)
    return _dot(ah, bh) + (_dot(ah, bl) + _dot(al, bh))


def _silu(x):
    return x * (1.0 / (1.0 + jnp.exp(-x)))


def _sigmoid(x):
    return 1.0 / (1.0 + jnp.exp(-x))


def _rows_times(x, vec, nb):
    if nb == 1:
        return x * vec
    tm, d = x.shape
    return (x.reshape(nb, tm // nb, d) * vec[:, None, :]).reshape(tm, d)


def _rows_plus(x, vec, nb):
    if nb == 1:
        return x + vec
    tm, d = x.shape
    return (x.reshape(nb, tm // nb, d) + vec[:, None, :]).reshape(tm, d)


def _rms_rows(x):
    return x * lax.rsqrt(jnp.mean(x * x, axis=-1, keepdims=True) + RMS_EPS)


def _ada_kernel(c_ref, w_ref, b_ref, o_ref):
    o_ref[...] = _dot3(_silu(c_ref[...]), w_ref[...]) + b_ref[...]


def _ada(c_all, w_ada, b_ada):
    rows = c_all.shape[0]
    n = w_ada.shape[1]
    tn = 1536
    return pl.pallas_call(
        _ada_kernel,
        grid=(n // tn,),
        in_specs=[pl.BlockSpec((rows, D_MODEL), lambda j: (0, 0)),
                  pl.BlockSpec((D_MODEL, tn), lambda j: (0, j)),
                  pl.BlockSpec((1, tn), lambda j: (0, j))],
        out_specs=pl.BlockSpec((rows, tn), lambda j: (0, j)),
        out_shape=jax.ShapeDtypeStruct((rows, n), F32),
        compiler_params=pltpu.CompilerParams(dimension_semantics=("arbitrary",), vmem_limit_bytes=VMEM_LIMIT),
        name="ada_mod",
    )(c_all, w_ada, b_ada)


def _rope_lanes(a, tab_ref, reps):
    width = a.shape[-1]
    cos = jnp.tile(tab_ref[0], (1, reps)) if reps > 1 else tab_ref[0]
    sin_lo = jnp.tile(tab_ref[1], (1, reps)) if reps > 1 else tab_ref[1]
    sin_hi = jnp.tile(tab_ref[2], (1, reps)) if reps > 1 else tab_ref[2]
    half = ROT_DIM // 2
    return a * cos + pltpu.roll(a, width - half, 1) * sin_lo + pltpu.roll(a, half, 1) * sin_hi


def _proj_kernel(x_ref, mod_ref, n1g_ref, wa_ref, wv_ref, wi_ref, wg_ref, bd_ref, gqk_ref, tab_ref,
                 qa_ref, ka_ref, kab_ref, va_ref, vab_ref, qb_ref, kb_ref, kbb_ref, vb_ref, vbb_ref,
                 qi_ref, ki2_ref, ki2b_ref, wi_out_ref, gates_ref, *, nb):
    x = x_ref[...]
    sh1 = mod_ref[:, 0, :]
    sc1 = mod_ref[:, 1, :]
    h = _rows_plus(_rows_times(_rms_rows(x) * n1g_ref[...], 1.0 + sc1, nb), sh1, nb)
    hb = h.astype(BF16)

    a = _dot(hb, wa_ref[...])
    ms = _dot((a * a).astype(BF16), bd_ref[...])
    a = a * lax.rsqrt(ms + RMS_EPS) * gqk_ref[...]
    a = _rope_lanes(a, tab_ref, a.shape[-1] // LANES)
    qa_ref[...] = (a[:, :A_WIDTH] * (HEAD_DIM ** -0.5)).astype(BF16)
    ka = a[:, A_WIDTH:]
    ka_ref[...] = ka.reshape(ka_ref.shape)
    kab_ref[...] = ka.astype(BF16)

    v = _dot(hb, wv_ref[...])
    va = v[:, :A_WIDTH]
    va_ref[...] = va.reshape(va_ref.shape)
    vab_ref[...] = va.astype(BF16)
    qb_ref[...] = (v[:, A_WIDTH:A_WIDTH + B_WIDTH] * (HEAD_DIM ** -0.5)).astype(BF16)
    kb = v[:, A_WIDTH + B_WIDTH:A_WIDTH + 2 * B_WIDTH]
    kb_ref[...] = kb.reshape(kb_ref.shape)
    kbb_ref[...] = kb.astype(BF16)
    vb = v[:, A_WIDTH + 2 * B_WIDTH:]
    vb_ref[...] = vb.reshape(vb_ref.shape)
    vbb_ref[...] = vb.astype(BF16)

    i = _dot(hb, wi_ref[...])
    nr = IDX_SEG - LANES
    ir = _rope_lanes(i[:, :nr], tab_ref, nr // LANES)
    nq = IDX_HEADS * IDX_DIM
    qi_ref[...] = (ir[:, :nq] * (IDX_DIM ** -0.5)).astype(BF16)
    ki2 = ir[:, nq:]
    ki2_ref[...] = ki2
    ki2b_ref[...] = ki2.astype(BF16)
    wi_out_ref[...] = i[:, nr:]

    gates_ref[...] = _sigmoid(_dot(hb, wg_ref[...])).astype(BF16)


def _proj(x2, mod, n1g, wa, wv, wi, wg, bd, gqk, tab, *, t_len, tm):
    n = x2.shape[0]
    nb = max(1, tm // t_len)
    tiles_per_batch = max(1, t_len // tm)
    tab_tiles = tab.shape[1] // tm
    row = lambda i: (i, 0)
    const2 = lambda i: (0, 0)
    tabmap = lambda i: (0, i % tab_tiles, 0)
    in_specs = [
        pl.BlockSpec((tm, D_MODEL), row),
        pl.BlockSpec((nb, 6, D_MODEL), lambda i: (i // tiles_per_batch, 0, 0)),
        pl.BlockSpec((1, D_MODEL), const2),
        pl.BlockSpec(wa.shape, const2),
        pl.BlockSpec(wv.shape, const2),
        pl.BlockSpec(wi.shape, const2),
        pl.BlockSpec(wg.shape, const2),
        pl.BlockSpec(bd.shape, const2),
        pl.BlockSpec((1, 2 * A_WIDTH), const2),
        pl.BlockSpec((3, tm, LANES), tabmap),
    ]
    outs = [
        ("qa", A_WIDTH, BF16), ("ka", A_WIDTH, F32), ("kab", A_WIDTH, BF16),
        ("va", A_WIDTH, F32), ("vab", A_WIDTH, BF16),
        ("qb", B_WIDTH, BF16), ("kb", B_WIDTH, F32), ("kbb", B_WIDTH, BF16),
        ("vb", B_WIDTH, F32), ("vbb", B_WIDTH, BF16),
        ("qi", IDX_HEADS * IDX_DIM, BF16), ("ki2", LANES, F32), ("ki2b", LANES, BF16), ("wi", LANES, F32),
        ("gates", 2 * D_MODEL, BF16),
    ]
    res = pl.pallas_call(
        functools.partial(_proj_kernel, nb=nb),
        grid=(n // tm,),
        in_specs=in_specs,
        out_specs=[pl.BlockSpec((tm, w // HEAD_DIM, HEAD_DIM), lambda i: (i, 0, 0)) if name in CACHE_OUTS
                   else pl.BlockSpec((tm, w), row) for name, w, _ in outs],
        out_shape=[jax.ShapeDtypeStruct((n, w // HEAD_DIM, HEAD_DIM) if name in CACHE_OUTS else (n, w), dt)
                   for name, w, dt in outs],
        compiler_params=pltpu.CompilerParams(dimension_semantics=("arbitrary",), vmem_limit_bytes=VMEM_LIMIT),
        name="in_proj",
    )(x2, mod, n1g, wa, wv, wi, wg, bd, gqk, tab)
    return dict(zip([o[0] for o in outs], res))


def _lower_tri(n):
    idx = jnp.arange(n)
    return (idx[:, None] >= idx[None, :]).astype(BF16)


STAGE_ROWS = 512


def _stage_chunk(past):
    return STAGE_ROWS if past % STAGE_ROWS == 0 else past


def _staged(past, chunk, sources, run):
    c = pl.program_id(1)
    n_chunks = past // chunk

    @pl.when(c < n_chunks)
    def _():
        r0 = pl.multiple_of(c * chunk, chunk)
        for stage, cache_ref, _, twice in sources:
            x = cache_ref[0].reshape(chunk, -1)
            if twice:
                x = jnp.concatenate([x, x], axis=1)
            stage[pl.ds(r0, chunk), :] = x.astype(stage.dtype)

    @pl.when(c == n_chunks)
    def _():
        for stage, _, new_ref, _ in sources:
            t_new, s_pad = new_ref.shape[1], stage.shape[0]
            stage[past:past + t_new, :] = new_ref[0]
            if s_pad > past + t_new:
                stage[past + t_new:, :] = jnp.zeros((s_pad - past - t_new, stage.shape[1]), stage.dtype)
        run()


def _head_halves(x):
    lane = lax.broadcasted_iota(I32, x.shape, 1)
    zero = jnp.zeros_like(x)
    return jnp.where(lane < HEAD_DIM, x, zero), jnp.where(lane >= HEAD_DIM, x, zero)


def _dsa_kernel(*refs, past, chunk, **tile_args):
    qi_ref, w_ref, qa_ref, ki_ref, ka_ref, va_ref = refs[:6]
    if not past:
        _dsa_tile(pl.program_id(1), ki_ref.at[0], ka_ref.at[0], va_ref.at[0], qi_ref, w_ref, qa_ref, *refs[6:],
                  past=0, **tile_args)
        return
    pki_ref, pka_ref, pva_ref = refs[6:9]
    ki_st, ka_st, va_st = refs[-3:]
    _staged(past, chunk,
            [(ki_st, pki_ref, ki_ref, True), (ka_st, pka_ref, ka_ref, False), (va_st, pva_ref, va_ref, False)],
            lambda: _dsa_tile(0, ki_st, ka_st, va_st, qi_ref, w_ref, qa_ref, *refs[9:-3], past=past, **tile_args))


def _dsa_tile(qt, ki_all, ka_all, va_all, qi_ref, w_ref, qa_ref, tri_ref, o_ref,
              s_scr, bias_scr, m_scr, l_scr, acc_scr, *, tq, kb, s_real, past, topk):
    q0 = past + qt * tq
    kmax = jnp.minimum((lax.shift_right_logical(q0 + tq - 1, CHUNK_SHIFT) + 1) * CHUNK, s_real)
    nblk = (kmax + kb - 1) // kb
    qchunk = lax.shift_right_logical(q0 + lax.broadcasted_iota(I32, (1, tq), 1), CHUNK_SHIFT)
    kf = float(topk)
    chunks = lambda ref, i: ref[0, :, i * LANES:(i + 1) * LANES]
    qi_heads = [q for i in range(IDX_HEADS // 2) for q in _head_halves(chunks(qi_ref, i))]
    qa_heads = [q for i in range(PAIRS) for q in _head_halves(chunks(qa_ref, i))]
    w_rows = w_ref[0].T

    def fold_rows(x, op):
        part = op(x.reshape(kb // COUNT_ROWS, COUNT_ROWS, tq), axis=0)
        return op(part, axis=0, keepdims=True)

    def score_block(j, masked):
        r0 = pl.multiple_of(j * kb, kb)
        kib = ki_all[pl.ds(r0, kb), :]
        acc = jnp.zeros((kb, tq), F32)
        for h in range(IDX_HEADS):
            rel = jnp.maximum(_dot_nt(kib, qi_heads[h]), 0.0)
            acc = acc + w_rows[h:h + 1, :] * rel
        s = acc + 0.0
        if masked:
            kpos = r0 + lax.broadcasted_iota(I32, (kb, tq), 0)
            adm = (lax.shift_right_logical(kpos, CHUNK_SHIFT) <= qchunk) & (kpos < s_real)
            s = jnp.where(adm, s, -jnp.inf)
        s_scr[pl.ds(r0, kb), :] = s

    n_open = jnp.minimum((lax.shift_right_logical(q0, CHUNK_SHIFT) + 1) * CHUNK, s_real) // kb

    def open_step(j, c):
        score_block(j, False)
        return c

    def masked_step(j, c):
        score_block(j, True)
        return c

    lax.fori_loop(0, n_open, open_step, 0)
    lax.fori_loop(n_open, nblk, masked_step, 0)

    def count(pred):
        def body(j, acc):
            r0 = pl.multiple_of(j * kb, kb)
            hit = jnp.where(pred(s_scr[pl.ds(r0, kb), :]), 1.0, 0.0)
            return acc + jnp.sum(hit.reshape(kb // COUNT_ROWS, COUNT_ROWS, tq), axis=0)
        acc = lax.fori_loop(0, nblk, body, jnp.zeros((COUNT_ROWS, tq), F32))
        return jnp.sum(acc, axis=0, keepdims=True)

    def key_to_score(key):
        return pltpu.bitcast(jnp.where(key < 0, key ^ 0x7FFFFFFF, key), F32)

    c0 = count(lambda blk: blk >= 0.0)
    t0 = jnp.where(c0 >= kf, 0, INT_MIN).astype(I32)

    def value_bit(i, t):
        cand = t | jnp.left_shift(jnp.int32(1), 30 - i)
        cand_f = key_to_score(cand)
        return jnp.where(count(lambda blk: blk >= cand_f) >= kf, cand, t)

    t_key = lax.fori_loop(0, 31, value_bit, t0)
    t = key_to_score(jnp.maximum(t_key, KEY_NEG_INF))

    need = kf - count(lambda blk: blk > t)
    tri = tri_ref[...]

    def select_block(j, seen):
        r0 = pl.multiple_of(j * kb, kb)
        blk = s_scr[pl.ds(r0, kb), :]
        tie = jnp.where((blk == t) & (blk > -jnp.inf), 1.0, 0.0)
        rank = _dot(tri, tie.astype(BF16)) + seen
        chosen = (blk > t) | ((tie > 0.0) & (rank <= need))
        bias_scr[pl.ds(r0, kb), :] = jnp.where(chosen, 0.0, NEG_BIG)
        return seen + fold_rows(tie, jnp.sum)

    lax.fori_loop(0, nblk, select_block, jnp.zeros((1, tq), F32))

    m_scr[...] = jnp.full(m_scr.shape, NEG_BIG, F32)
    l_scr[...] = jnp.zeros(l_scr.shape, F32)
    acc_scr[...] = jnp.zeros(acc_scr.shape, F32)

    def attend_block(j, carry):
        r0 = pl.multiple_of(j * kb, kb)
        bias = bias_scr[pl.ds(r0, kb), :]
        heads = range(A_HEADS)
        m_old = m_scr[...]
        l_old = l_scr[...]
        k_pair = [ka_all[pl.ds(r0, kb), i * LANES:(i + 1) * LANES] for i in range(PAIRS)]
        v_pair = [va_all[pl.ds(r0, kb), i * LANES:(i + 1) * LANES] for i in range(PAIRS)]
        ss = [_dot_nt(k_pair[h // 2], qa_heads[h]) + bias for h in heads]
        m_new = [jnp.maximum(m_old[h:h + 1, :], fold_rows(ss[h], jnp.max)) for h in heads]
        ps = [jnp.exp(ss[h] - m_new[h]) for h in heads]
        pvs = [_dot_tn(v_pair[h // 2], ps[h].astype(BF16)) for h in heads]
        alphas = [jnp.exp(m_old[h:h + 1, :] - m_new[h]) for h in heads]
        for i in range(PAIRS):
            e, o = 2 * i, 2 * i + 1
            acc_scr[i] = (jnp.where(even_rows, alphas[e], alphas[o]) * acc_scr[i]
                          + jnp.where(even_rows, pvs[e], pvs[o]))
        for h in heads:
            l_scr[h:h + 1, :] = alphas[h] * l_old[h:h + 1, :] + fold_rows(ps[h], jnp.sum)
            m_scr[h:h + 1, :] = m_new[h]
        return carry

    even_rows = lax.broadcasted_iota(I32, (LANES, tq), 0) < HEAD_DIM
    lax.fori_loop(0, nblk, attend_block, 0)

    l_fin = l_scr[...]
    for i in range(PAIRS):
        denom = jnp.where(even_rows, l_fin[2 * i:2 * i + 1, :], l_fin[2 * i + 1:2 * i + 2, :])
        o_ref[0, :, i * LANES:(i + 1) * LANES] = (acc_scr[i] / denom).T.astype(o_ref.dtype)


def _attn_layout(q_arrays, new, cached, tq):
    bsz, t_len = q_arrays[0].shape[:2]
    whole = lambda b, j: (b, 0, 0)
    new_specs = [pl.BlockSpec((1,) + x.shape[1:], whole) for x in new]
    if not cached:
        return (bsz, t_len // tq), (lambda b, j: (b, j, 0)), new_specs, 0, 0
    assert t_len == tq
    past = cached[0].shape[1]
    chunk = _stage_chunk(past)
    last = past // chunk - 1
    cache_specs = [pl.BlockSpec((1, chunk) + x.shape[2:],
                                lambda b, j, nd=x.ndim: (b, jnp.minimum(j, last)) + (0,) * (nd - 2)) for x in cached]
    return (bsz, past // chunk + 1), whole, new_specs + cache_specs, past, chunk


def _dsa(qi, wi, qa, new, cached, *, tq, kb, s_real, topk):
    bsz, t_len, _ = qa.shape
    grid, qmap, key_specs, past, chunk = _attn_layout((qa,), new, cached, tq)
    s_pad = _round_up(s_real, kb)
    assert cached or s_pad == s_real
    kern = functools.partial(_dsa_kernel, tq=tq, kb=kb, s_real=s_real, past=past, chunk=chunk, topk=topk)
    stage = [pltpu.VMEM((s_pad, x.shape[-1]), BF16) for x in new] if cached else []
    return pl.pallas_call(
        kern,
        grid=grid,
        in_specs=[
            pl.BlockSpec((1, tq, IDX_HEADS * IDX_DIM), qmap),
            pl.BlockSpec((1, tq, LANES), qmap),
            pl.BlockSpec((1, tq, A_WIDTH), qmap),
        ] + key_specs + [pl.BlockSpec((kb, kb), lambda b, j: (0, 0))],
        out_specs=pl.BlockSpec((1, tq, A_WIDTH), qmap),
        out_shape=jax.ShapeDtypeStruct((bsz, t_len, A_WIDTH), BF16),
        scratch_shapes=[
            pltpu.VMEM((s_pad, tq), F32),
            pltpu.VMEM((s_pad, tq), F32),
            pltpu.VMEM((A_HEADS, tq), F32),
            pltpu.VMEM((A_HEADS, tq), F32),
            pltpu.VMEM((PAIRS, LANES, tq), F32),
        ] + stage,
        compiler_params=pltpu.CompilerParams(dimension_semantics=("arbitrary", "arbitrary"),
                                             vmem_limit_bytes=VMEM_LIMIT),
        name="dsa_attention",
    )(qi, wi, qa, *new, *cached, _lower_tri(kb))


def _stick_kernel(*refs, past, chunk, **tile_args):
    q_ref, k_ref, v_ref = refs[:3]
    if not past:
        _stick_tile(pl.program_id(1), k_ref.at[0], v_ref.at[0], q_ref, *refs[3:], past=0, **tile_args)
        return
    pk_ref, pv_ref = refs[3:5]
    k_st, v_st = refs[-2:]
    _staged(past, chunk, [(k_st, pk_ref, k_ref, False), (v_st, pv_ref, v_ref, False)],
            lambda: _stick_tile(0, k_st, v_st, q_ref, *refs[5:-2], past=past, **tile_args))


def _stick_tile(qt, k_all, v_all, q_ref, tri_ref, o_ref, carry_scr, acc_scr, *, tq, kb, past):
    q0 = past + qt * tq
    nblk = (q0 + tq - 1 + kb - 1) // kb
    n_full = q0 // kb
    qpos = q0 + lax.broadcasted_iota(I32, (tq, kb), 0)
    carry_scr[...] = jnp.zeros(carry_scr.shape, F32)
    acc_scr[...] = jnp.zeros(acc_scr.shape, F32)
    tri2 = tri_ref[...]
    heads = range(B_HEADS)
    q_heads = [q for i in range(PAIRS) for q in _head_halves(q_ref[0, :, i * LANES:(i + 1) * LANES])]
    even_lanes = lax.broadcasted_iota(I32, (tq, LANES), 1) < HEAD_DIM

    def block(j, masked):
        r0 = pl.multiple_of(j * kb, kb)
        causal = ((r0 + lax.broadcasted_iota(I32, (tq, kb), 1)) < qpos) if masked else None
        carries = [carry_scr[h] for h in heads]
        k_pair = [k_all[pl.ds(r0, kb), i * LANES:(i + 1) * LANES] for i in range(PAIRS)]
        v_pair = [v_all[pl.ds(r0, kb), i * LANES:(i + 1) * LANES] for i in range(PAIRS)]
        zs = [_dot_nt(q_heads[h], k_pair[h // 2]) for h in heads]
        sps = []
        for z in zs:
            sp = jnp.maximum(z, 0.0) + jnp.log(1.0 + jnp.exp(-jnp.abs(z)))
            sps.append(jnp.where(causal, sp, 0.0) if masked else sp)
        sufs = []
        for h in heads:
            hi, lo = _split_bf16(sps[h])
            hilo = jnp.concatenate([hi, lo], axis=1)
            sufs.append(_dot(hilo, tri2) + jnp.tile(carries[h], (1, kb // LANES)))
        pvs = []
        for h in heads:
            a = jnp.exp(zs[h] - sufs[h])
            if masked:
                a = jnp.where(causal, a, 0.0)
            pvs.append(_dot(a.astype(BF16), v_pair[h // 2]))
        for i in range(PAIRS):
            acc_scr[i] = acc_scr[i] + jnp.where(even_lanes, pvs[2 * i], pvs[2 * i + 1])
        for h in heads:
            carry_scr[h] = carries[h] + jnp.sum(sps[h], axis=-1, keepdims=True)

    def masked_step(jj, c):
        block(nblk - 1 - jj, True)
        return c

    def full_step(jj, c):
        block(n_full - 1 - jj, False)
        return c

    lax.fori_loop(0, nblk - n_full, masked_step, 0)
    lax.fori_loop(0, n_full, full_step, 0)
    for i in range(PAIRS):
        o_ref[0, :, i * LANES:(i + 1) * LANES] = acc_scr[i].astype(o_ref.dtype)


def _stick(q, new, cached, *, tq, kb, s_real):
    bsz, t_len, _ = q.shape
    grid, qmap, key_specs, past, chunk = _attn_layout((q,), new, cached, tq)
    s_pad = _round_up(s_real, kb)
    assert cached or s_pad == s_real
    kern = functools.partial(_stick_kernel, tq=tq, kb=kb, past=past, chunk=chunk)
    stage = [pltpu.VMEM((s_pad, x.shape[-1]), BF16) for x in new] if cached else []
    return pl.pallas_call(
        kern,
        grid=grid,
        in_specs=[pl.BlockSpec((1, tq, B_WIDTH), qmap)] + key_specs
        + [pl.BlockSpec((2 * kb, kb), lambda b, j: (0, 0))],
        out_specs=pl.BlockSpec((1, tq, B_WIDTH), qmap),
        out_shape=jax.ShapeDtypeStruct((bsz, t_len, B_WIDTH), BF16),
        scratch_shapes=[
            pltpu.VMEM((B_HEADS, tq, LANES), F32),
            pltpu.VMEM((PAIRS, tq, LANES), F32),
        ] + stage,
        compiler_params=pltpu.CompilerParams(dimension_semantics=("arbitrary", "arbitrary"),
                                             vmem_limit_bytes=VMEM_LIMIT),
        name="stick_attention",
    )(q, *new, *cached, jnp.tile(_lower_tri(kb), (2, 1)))


def _merge_kernel(x_ref, oa_ref, ob_ref, gates_ref, mod_ref, wua_ref, wub_ref, wout_ref, n2g_ref, wr_ref, br_ref,
                  y_ref, h2_ref, comb_ref, *, nb):
    g1 = mod_ref[:, 2, :]
    sh2 = mod_ref[:, 3, :]
    sc2 = mod_ref[:, 4, :]
    gates = gates_ref[...].astype(F32)
    merged = gates[:, :D_MODEL] * _dot(oa_ref[...], wua_ref[...]) + gates[:, D_MODEL:] * _dot(ob_ref[...], wub_ref[...])
    y = x_ref[...] + _rows_times(_dot(merged.astype(BF16), wout_ref[...]), g1, nb)
    y_ref[...] = y
    h2 = _rows_plus(_rows_times(_rms_rows(y) * n2g_ref[...], 1.0 + sc2, nb), sh2, nb)
    h2_ref[...] = h2.astype(BF16)

    logits = _dot3(h2, wr_ref[...]) + br_ref[...]
    lane = lax.broadcasted_iota(I32, logits.shape, 1).astype(F32)
    far = float(ROUTER_LANES)
    is_g = lane < float(N_GROUPS)
    gl = jnp.where(is_g, logits, -jnp.inf)
    gmax = jnp.max(gl, axis=-1, keepdims=True)
    gsel = jnp.min(jnp.where(is_g & (gl == gmax), lane, far), axis=-1, keepdims=True)
    p_group = 1.0 / jnp.sum(jnp.where(is_g, jnp.exp(gl - gmax), 0.0), axis=-1, keepdims=True)
    e_lo = float(EXPERT_LANE0) + gsel * float(EXPERTS_PER_GROUP)
    in_grp = (lane >= e_lo) & (lane < e_lo + float(EXPERTS_PER_GROUP))
    el = jnp.where(in_grp, logits, -jnp.inf)
    v1 = jnp.max(el, axis=-1, keepdims=True)
    i1 = jnp.min(jnp.where(in_grp & (el == v1), lane, far), axis=-1, keepdims=True)
    el2 = jnp.where(lane == i1, -jnp.inf, el)
    v2 = jnp.max(el2, axis=-1, keepdims=True)
    i2 = jnp.min(jnp.where(in_grp & (lane != i1) & (el2 == v2), lane, far), axis=-1, keepdims=True)
    e2 = jnp.exp(v2 - v1)
    w1 = p_group / (1.0 + e2)
    w2 = p_group * e2 / (1.0 + e2)
    comb_ref[...] = jnp.where(lane == i1, w1, 0.0) + jnp.where(lane == i2, w2, 0.0)


def _merge(x2, oa, ob, gates, mod, wua, wub, wout, n2g, wr, br, *, t_len, tm):
    n = x2.shape[0]
    nb = max(1, tm // t_len)
    tiles_per_batch = max(1, t_len // tm)
    row = lambda i: (i, 0)
    const2 = lambda i: (0, 0)
    return pl.pallas_call(
        functools.partial(_merge_kernel, nb=nb),
        grid=(n // tm,),
        in_specs=[
            pl.BlockSpec((tm, D_MODEL), row),
            pl.BlockSpec((tm, A_WIDTH), row),
            pl.BlockSpec((tm, B_WIDTH), row),
            pl.BlockSpec((tm, 2 * D_MODEL), row),
            pl.BlockSpec((nb, 6, D_MODEL), lambda i: (i // tiles_per_batch, 0, 0)),
            pl.BlockSpec(wua.shape, const2),
            pl.BlockSpec(wub.shape, const2),
            pl.BlockSpec(wout.shape, const2),
            pl.BlockSpec((1, D_MODEL), const2),
            pl.BlockSpec(wr.shape, const2),
            pl.BlockSpec((1, ROUTER_LANES), const2),
        ],
        out_specs=[pl.BlockSpec((tm, D_MODEL), row), pl.BlockSpec((tm, D_MODEL), row),
                   pl.BlockSpec((tm, ROUTER_LANES), row)],
        out_shape=[jax.ShapeDtypeStruct((n, D_MODEL), F32), jax.ShapeDtypeStruct((n, D_MODEL), BF16),
                   jax.ShapeDtypeStruct((n, ROUTER_LANES), F32)],
        compiler_params=pltpu.CompilerParams(dimension_semantics=("arbitrary",), vmem_limit_bytes=VMEM_LIMIT),
        name="merge_router",
    )(x2, oa, ob, gates, mod, wua, wub, wout, n2g, wr, br)


def _moe_kernel(y_ref, h2_ref, comb_ref, mod_ref, wg_ref, wu_ref, wd_ref, o_ref, acc_scr, *, nb):
    e = pl.program_id(1)

    @pl.when(e == 0)
    def _():
        acc_scr[...] = jnp.zeros(acc_scr.shape, F32)

    h2 = h2_ref[...]
    act = _silu(_dot(h2, wg_ref[0].astype(BF16))) * _dot(h2, wu_ref[0].astype(BF16))
    contrib = _dot(act.astype(BF16), wd_ref[0].astype(BF16))
    comb = comb_ref[...]
    lane = lax.broadcasted_iota(I32, comb.shape, 1)
    w_e = jnp.sum(jnp.where(lane == e + EXPERT_LANE0, comb, 0.0), axis=-1, keepdims=True)
    acc_scr[...] += w_e * contrib

    @pl.when(e == pl.num_programs(1) - 1)
    def _():
        o_ref[...] = y_ref[...] + _rows_times(acc_scr[...], mod_ref[:, 5, :], nb)


def _moe(y, h2, comb, mod, w_gate, w_up, w_down, *, t_len, tm):
    n = y.shape[0]
    nb = max(1, tm // t_len)
    tiles_per_batch = max(1, t_len // tm)
    row = lambda i, e: (i, 0)
    return pl.pallas_call(
        functools.partial(_moe_kernel, nb=nb),
        grid=(n // tm, N_EXPERTS),
        in_specs=[
            pl.BlockSpec((tm, D_MODEL), row),
            pl.BlockSpec((tm, D_MODEL), row),
            pl.BlockSpec((tm, ROUTER_LANES), row),
            pl.BlockSpec((nb, 6, D_MODEL), lambda i, e: (i // tiles_per_batch, 0, 0)),
            pl.BlockSpec((1, D_MODEL, D_EXPERT), lambda i, e: (e, 0, 0)),
            pl.BlockSpec((1, D_MODEL, D_EXPERT), lambda i, e: (e, 0, 0)),
            pl.BlockSpec((1, D_EXPERT, D_MODEL), lambda i, e: (e, 0, 0)),
        ],
        out_specs=pl.BlockSpec((tm, D_MODEL), row),
        out_shape=jax.ShapeDtypeStruct((n, D_MODEL), F32),
        scratch_shapes=[pltpu.VMEM((tm, D_MODEL), F32)],
        compiler_params=pltpu.CompilerParams(dimension_semantics=("arbitrary", "arbitrary"),
                                             vmem_limit_bytes=VMEM_LIMIT),
        name="moe_dense",
    )(y, h2, comb, mod, w_gate, w_up, w_down)


def _rope_tables(pos):
    half = ROT_DIM // 2
    freqs = ROPE_THETA ** (-jnp.arange(0, ROT_DIM, 2, dtype=F32) / ROT_DIM)
    ang = pos.astype(F32)[:, None] * freqs[None, :]
    cos, sin = jnp.cos(ang), jnp.sin(ang)
    dh = jnp.arange(LANES) % HEAD_DIM
    fi = dh % half
    cos_t = jnp.where(dh < ROT_DIM, cos[:, fi], 1.0)
    sin_lo = jnp.where(dh < half, -sin[:, fi], 0.0)
    sin_hi = jnp.where((dh >= half) & (dh < ROT_DIM), sin[:, fi], 0.0)
    return jnp.stack([cos_t, sin_lo, sin_hi]).astype(F32)


def _round_up(a, b):
    return (a + b - 1) // b * b


def _layer(x, mod, past_ak, past_av, past_aki, past_bk, past_bv, w, *, cfg):
    bsz, t_len, _ = x.shape
    past = past_ak.shape[1]
    s_real = past + t_len
    n = bsz * t_len
    x2 = x.reshape(n, D_MODEL)
    pos = past + jnp.arange(t_len, dtype=jnp.int32)

    tm = cfg["tm"]
    tab = jnp.tile(_rope_tables(pos), (1, max(1, tm // t_len), 1))
    pr = _proj(x2, mod, w["n1g"], w["wa"], w["wv"], w["wi"], w["wg"], w["bd"], w["gqk"], tab, t_len=t_len, tm=tm)

    new_ak = pr["ka"].reshape(bsz, t_len, A_HEADS, HEAD_DIM)
    new_av = pr["va"].reshape(bsz, t_len, A_HEADS, HEAD_DIM)
    new_aki = pr["ki2"][:, :IDX_DIM].reshape(bsz, t_len, IDX_DIM)
    new_bk = pr["kb"].reshape(bsz, t_len, B_HEADS, HEAD_DIM)
    new_bv = pr["vb"].reshape(bsz, t_len, B_HEADS, HEAD_DIM)

    def per_batch(x2):
        return x2.reshape(bsz, t_len, x2.shape[-1])

    def cached(*caches):
        return caches if past else ()

    topk = min(TOPK_MAX, s_real // 4)
    oa = _dsa(per_batch(pr["qi"]), per_batch(pr["wi"]), per_batch(pr["qa"]),
              (per_batch(pr["ki2b"]), per_batch(pr["kab"]), per_batch(pr["vab"])),
              cached(past_aki, past_ak, past_av),
              tq=cfg["dsa_tq"], kb=cfg["dsa_kb"], s_real=s_real, topk=topk)

    ob = _stick(per_batch(pr["qb"]), (per_batch(pr["kbb"]), per_batch(pr["vbb"])), cached(past_bk, past_bv),
                tq=cfg["sb_tq"], kb=cfg["sb_kb"], s_real=s_real)

    y, h2, comb = _merge(x2, oa.reshape(n, A_WIDTH), ob.reshape(n, B_WIDTH), pr["gates"], mod,
                         w["wua"], w["wub"], w["wout"], w["n2g"], w["wr"], w["br"], t_len=t_len, tm=cfg["tm_merge"])
    out = _moe(y, h2, comb, mod, w["w_gate"], w["w_up"], w["w_down"], t_len=t_len, tm=cfg["tm_moe"])
    return out.reshape(bsz, t_len, D_MODEL), (new_ak, new_av, new_aki, new_bk, new_bv)


def _prep_weights(w_in, norm1_g, qnorm_g, knorm_g, w_up_a, w_up_b, w_out, norm2_g, w_rg, b_rg, w_re, b_re,
                  w_e_gate, w_e_up, w_e_down):
    o = 0
    seg = {}
    for name, width in (("qa", A_WIDTH), ("ka", A_WIDTH), ("va", A_WIDTH), ("qi", IDX_HEADS * IDX_DIM),
                        ("ki", IDX_DIM), ("wi", IDX_HEADS), ("qb", B_WIDTH), ("kb", B_WIDTH), ("vb", B_WIDTH),
                        ("gates", 2 * D_MODEL)):
        seg[name] = w_in[:, o:o + width]
        o += width
    pad_i = jnp.zeros((D_MODEL, LANES - IDX_HEADS), w_in.dtype)
    hd = jnp.arange(2 * A_WIDTH) // HEAD_DIM
    n_r = N_GROUPS + N_EXPERTS
    return dict(
        n1g=norm1_g.reshape(1, D_MODEL),
        wa=jnp.concatenate([seg["qa"], seg["ka"]], axis=1).astype(BF16),
        wv=jnp.concatenate([seg["va"], seg["qb"], seg["kb"], seg["vb"]], axis=1).astype(BF16),
        wi=jnp.concatenate([seg["qi"], seg["ki"], seg["ki"], seg["wi"], pad_i], axis=1).astype(BF16),
        wg=seg["gates"].astype(BF16),
        bd=((hd[:, None] == hd[None, :]).astype(F32) / HEAD_DIM).astype(BF16),
        gqk=jnp.concatenate([jnp.tile(qnorm_g, A_HEADS), jnp.tile(knorm_g, A_HEADS)]).reshape(1, 2 * A_WIDTH),
        wua=w_up_a.astype(BF16), wub=w_up_b.astype(BF16), wout=w_out.astype(BF16),
        n2g=norm2_g.reshape(1, D_MODEL),
        wr=jnp.pad(jnp.concatenate([w_rg, w_re], axis=1), ((0, 0), (0, ROUTER_LANES - n_r))),
        br=jnp.pad(jnp.concatenate([b_rg, b_re]), (0, ROUTER_LANES - n_r)).reshape(1, ROUTER_LANES),
        w_gate=w_e_gate, w_up=w_e_up, w_down=w_e_down,
    )


def _group_cfg(t_len):
    if t_len >= 512:
        return dict(tm=256, tm_merge=256, tm_moe=1024, dsa_tq=256, dsa_kb=512, sb_tq=256, sb_kb=256)
    return dict(tm=512, tm_merge=512, tm_moe=512, dsa_tq=t_len, dsa_kb=384, sb_tq=t_len, sb_kb=256)


def kernel(x_prompt, x_sample, cache_a_k, cache_a_v, cache_a_kidx, cache_b_k, cache_b_v, c_prompt, c_sample,
           w_ada, b_ada, norm1_g, w_in, qnorm_g, knorm_g, w_up_a, w_up_b, w_out, norm2_g,
           w_rg, b_rg, w_re, b_re, w_e_gate, w_e_up, w_e_down):
    depth = w_ada.shape[0]
    bp, bs = x_prompt.shape[0], x_sample.shape[0]
    dt = x_prompt.dtype
    rows = _round_up(bp + bs, 8)
    empty_a = jnp.zeros((bp, 0, A_HEADS, HEAD_DIM), dt)
    empty_i = jnp.zeros((bp, 0, IDX_DIM), dt)
    empty_b = jnp.zeros((bp, 0, B_HEADS, HEAD_DIM), dt)
    y_p, y_s = x_prompt, x_sample
    rows_p, rows_s = [], []
    c_all = jnp.concatenate([c_prompt, c_sample, jnp.zeros((rows - bp - bs, D_MODEL), dt)], axis=0)
    for l in range(depth):
        w = _prep_weights(w_in[l], norm1_g[l], qnorm_g[l], knorm_g[l], w_up_a[l], w_up_b[l], w_out[l], norm2_g[l],
                          w_rg[l], b_rg[l], w_re[l], b_re[l], w_e_gate[l], w_e_up[l], w_e_down[l])
        mod = _ada(c_all, w_ada[l], b_ada[l].reshape(1, -1))
        mod_p = mod[:bp].reshape(bp, 6, D_MODEL)
        mod_s = mod[bp:bp + bs].reshape(bs, 6, D_MODEL)
        y_p, new_p = _layer(y_p, mod_p, empty_a, empty_a, empty_i, empty_b, empty_b, w,
                            cfg=_group_cfg(y_p.shape[1]))
        y_s, new_s = _layer(y_s, mod_s, cache_a_k[l], cache_a_v[l], cache_a_kidx[l], cache_b_k[l], cache_b_v[l], w,
                            cfg=_group_cfg(y_s.shape[1]))
        rows_p.append(new_p)
        rows_s.append(new_s)
    stack = lambda rws, i: jnp.stack([r[i] for r in rws])
    return (y_p, y_s) + tuple(stack(rows_p, i) for i in range(5)) + tuple(stack(rows_s, i) for i in range(5))
```

```python
import functools

import jax
import jax.numpy as jnp
from jax import lax
from jax.experimental import pallas as pl
from jax.experimental.pallas import tpu as pltpu

D_MODEL = 1024
CHUNK = 64
CHUNK_SHIFT = 6
HEAD_DIM = 64
A_HEADS = 8
B_HEADS = 8
IDX_HEADS = 4
IDX_DIM = 64
TOPK_MAX = 256
ROPE_THETA = 500000.0
ROT_DIM = HEAD_DIM // 4
N_GROUPS = 4
EXPERTS_PER_GROUP = 8
N_EXPERTS = N_GROUPS * EXPERTS_PER_GROUP
D_EXPERT = 256
RMS_EPS = 1e-6
A_WIDTH = A_HEADS * HEAD_DIM
B_WIDTH = B_HEADS * HEAD_DIM

LANES = 128
CACHE_OUTS = ("ka", "va", "kb", "vb")
IDX_SEG = 512
PAIRS = 4
COUNT_ROWS = 64
ROUTER_LANES = 128
EXPERT_LANE0 = N_GROUPS
VMEM_LIMIT = 56 * 1024 * 1024

F32 = jnp.float32
BF16 = jnp.bfloat16
I32 = jnp.int32
NEG_BIG = -1e30
INT_MIN = -(2 ** 31)
KEY_NEG_INF = (0xFF800000 - (1 << 32)) ^ 0x7FFFFFFF


def _dot(a, b):
    return jnp.dot(a, b, preferred_element_type=F32)


def _dot_nt(a, b):
    return lax.dot_general(a, b, (((1,), (1,)), ((), ())), preferred_element_type=F32)


def _dot_tn(a, b):
    return lax.dot_general(a, b, (((0,), (0,)), ((), ())), preferred_element_type=F32)


def _split_bf16(a):
    hi = a.astype(BF16)
    lo = (a - hi.astype(F32)).astype(BF16)
    return hi, lo


def _dot3(a, b):
    ah, al = _split_bf16(a)
    bh, bl = _split_bf16(b)
    return _dot(ah, bh) + (_dot(ah, bl) + _dot(al, bh))


def _silu(x):
    return x * (1.0 / (1.0 + jnp.exp(-x)))


def _sigmoid(x):
    return 1.0 / (1.0 + jnp.exp(-x))


def _rows_times(x, vec, nb):
    if nb == 1:
        return x * vec
    tm, d = x.shape
    return (x.reshape(nb, tm // nb, d) * vec[:, None, :]).reshape(tm, d)


def _rows_plus(x, vec, nb):
    if nb == 1:
        return x + vec
    tm, d = x.shape
    return (x.reshape(nb, tm // nb, d) + vec[:, None, :]).reshape(tm, d)


def _rms_rows(x):
    return x * lax.rsqrt(jnp.mean(x * x, axis=-1, keepdims=True) + RMS_EPS)


def _ada_kernel(c_ref, w_ref, b_ref, o_ref):
    o_ref[...] = _dot3(_silu(c_ref[...]), w_ref[...]) + b_ref[...]


def _ada(c_all, w_ada, b_ada):
    rows = c_all.shape[0]
    n = w_ada.shape[1]
    tn = 1536
    return pl.pallas_call(
        _ada_kernel,
        grid=(n // tn,),
        in_specs=[pl.BlockSpec((rows, D_MODEL), lambda j: (0, 0)),
                  pl.BlockSpec((D_MODEL, tn), lambda j: (0, j)),
                  pl.BlockSpec((1, tn), lambda j: (0, j))],
        out_specs=pl.BlockSpec((rows, tn), lambda j: (0, j)),
        out_shape=jax.ShapeDtypeStruct((rows, n), F32),
        compiler_params=pltpu.CompilerParams(dimension_semantics=("arbitrary",), vmem_limit_bytes=VMEM_LIMIT),
        name="ada_mod",
    )(c_all, w_ada, b_ada)


def _rope_lanes(a, tab_ref, reps):
    width = a.shape[-1]
    cos = jnp.tile(tab_ref[0], (1, reps)) if reps > 1 else tab_ref[0]
    sin_lo = jnp.tile(tab_ref[1], (1, reps)) if reps > 1 else tab_ref[1]
    sin_hi = jnp.tile(tab_ref[2], (1, reps)) if reps > 1 else tab_ref[2]
    half = ROT_DIM // 2
    return a * cos + pltpu.roll(a, width - half, 1) * sin_lo + pltpu.roll(a, half, 1) * sin_hi


def _proj_kernel(x_ref, mod_ref, n1g_ref, wa_ref, wv_ref, wi_ref, wg_ref, bd_ref, gqk_ref, tab_ref,
                 qa_ref, ka_ref, kab_ref, va_ref, vab_ref, qb_ref, kb_ref, kbb_ref, vb_ref, vbb_ref,
                 qi_ref, ki2_ref, ki2b_ref, wi_out_ref, gates_ref, *, nb):
    x = x_ref[...]
    sh1 = mod_ref[:, 0, :]
    sc1 = mod_ref[:, 1, :]
    h = _rows_plus(_rows_times(_rms_rows(x) * n1g_ref[...], 1.0 + sc1, nb), sh1, nb)
    hb = h.astype(BF16)

    a = _dot(hb, wa_ref[...])
    ms = _dot((a * a).astype(BF16), bd_ref[...])
    a = a * lax.rsqrt(ms + RMS_EPS) * gqk_ref[...]
    a = _rope_lanes(a, tab_ref, a.shape[-1] // LANES)
    qa_ref[...] = (a[:, :A_WIDTH] * (HEAD_DIM ** -0.5)).astype(BF16)
    ka = a[:, A_WIDTH:]
    ka_ref[...] = ka.reshape(ka_ref.shape)
    kab_ref[...] = ka.astype(BF16)

    v = _dot(hb, wv_ref[...])
    va = v[:, :A_WIDTH]
    va_ref[...] = va.reshape(va_ref.shape)
    vab_ref[...] = va.astype(BF16)
    qb_ref[...] = (v[:, A_WIDTH:A_WIDTH + B_WIDTH] * (HEAD_DIM ** -0.5)).astype(BF16)
    kb = v[:, A_WIDTH + B_WIDTH:A_WIDTH + 2 * B_WIDTH]
    kb_ref[...] = kb.reshape(kb_ref.shape)
    kbb_ref[...] = kb.astype(BF16)
    vb = v[:, A_WIDTH + 2 * B_WIDTH:]
    vb_ref[...] = vb.reshape(vb_ref.shape)
    vbb_ref[...] = vb.astype(BF16)

    i = _dot(hb, wi_ref[...])
    nr = IDX_SEG - LANES
    ir = _rope_lanes(i[:, :nr], tab_ref, nr // LANES)
    nq = IDX_HEADS * IDX_DIM
    qi_ref[...] = (ir[:, :nq] * (IDX_DIM ** -0.5)).astype(BF16)
    ki2 = ir[:, nq:]
    ki2_ref[...] = ki2
    ki2b_ref[...] = ki2.astype(BF16)
    wi_out_ref[...] = i[:, nr:]

    gates_ref[...] = _sigmoid(_dot(hb, wg_ref[...])).astype(BF16)


def _proj(x2, mod, n1g, wa, wv, wi, wg, bd, gqk, tab, *, t_len, tm):
    n = x2.shape[0]
    nb = max(1, tm // t_len)
    tiles_per_batch = max(1, t_len // tm)
    tab_tiles = tab.shape[1] // tm
    row = lambda i: (i, 0)
    const2 = lambda i: (0, 0)
    tabmap = lambda i: (0, i % tab_tiles, 0)
    in_specs = [
        pl.BlockSpec((tm, D_MODEL), row),
        pl.BlockSpec((nb, 6, D_MODEL), lambda i: (i // tiles_per_batch, 0, 0)),
        pl.BlockSpec((1, D_MODEL), const2),
        pl.BlockSpec(wa.shape, const2),
        pl.BlockSpec(wv.shape, const2),
        pl.BlockSpec(wi.shape, const2),
        pl.BlockSpec(wg.shape, const2),
        pl.BlockSpec(bd.shape, const2),
        pl.BlockSpec((1, 2 * A_WIDTH), const2),
        pl.BlockSpec((3, tm, LANES), tabmap),
    ]
    outs = [
        ("qa", A_WIDTH, BF16), ("ka", A_WIDTH, F32), ("kab", A_WIDTH, BF16),
        ("va", A_WIDTH, F32), ("vab", A_WIDTH, BF16),
        ("qb", B_WIDTH, BF16), ("kb", B_WIDTH, F32), ("kbb", B_WIDTH, BF16),
        ("vb", B_WIDTH, F32), ("vbb", B_WIDTH, BF16),
        ("qi", IDX_HEADS * IDX_DIM, BF16), ("ki2", LANES, F32), ("ki2b", LANES, BF16), ("wi", LANES, F32),
        ("gates", 2 * D_MODEL, BF16),
    ]
    res = pl.pallas_call(
        functools.partial(_proj_kernel, nb=nb),
        grid=(n // tm,),
        in_specs=in_specs,
        out_specs=[pl.BlockSpec((tm, w // HEAD_DIM, HEAD_DIM), lambda i: (i, 0, 0)) if name in CACHE_OUTS
                   else pl.BlockSpec((tm, w), row) for name, w, _ in outs],
        out_shape=[jax.ShapeDtypeStruct((n, w // HEAD_DIM, HEAD_DIM) if name in CACHE_OUTS else (n, w), dt)
                   for name, w, dt in outs],
        compiler_params=pltpu.CompilerParams(dimension_semantics=("arbitrary",), vmem_limit_bytes=VMEM_LIMIT),
        name="in_proj",
    )(x2, mod, n1g, wa, wv, wi, wg, bd, gqk, tab)
    return dict(zip([o[0] for o in outs], res))


def _lower_tri(n):
    idx = jnp.arange(n)
    return (idx[:, None] >= idx[None, :]).astype(BF16)


def _head_halves(x):
    lane = lax.broadcasted_iota(I32, x.shape, 1)
    zero = jnp.zeros_like(x)
    return jnp.where(lane < HEAD_DIM, x, zero), jnp.where(lane >= HEAD_DIM, x, zero)


def _dsa_kernel(qi_ref, w_ref, qa_ref, ki_ref, ka_ref, va_ref, tri_ref, o_ref,
                s_scr, bias_scr, m_scr, l_scr, acc_scr, *, tq, kb, s_real, past, topk):
    qt = pl.program_id(1)
    q0 = past + qt * tq
    kmax = jnp.minimum((lax.shift_right_logical(q0 + tq - 1, CHUNK_SHIFT) + 1) * CHUNK, s_real)
    nblk = (kmax + kb - 1) // kb
    qchunk = lax.shift_right_logical(q0 + lax.broadcasted_iota(I32, (1, tq), 1), CHUNK_SHIFT)
    kf = float(topk)
    chunks = lambda ref, i: ref[0, :, i * LANES:(i + 1) * LANES]
    qi_heads = [q for i in range(IDX_HEADS // 2) for q in _head_halves(chunks(qi_ref, i))]
    qa_heads = [q for i in range(PAIRS) for q in _head_halves(chunks(qa_ref, i))]
    w_rows = w_ref[0].T

    def fold_rows(x, op):
        part = op(x.reshape(kb // COUNT_ROWS, COUNT_ROWS, tq), axis=0)
        return op(part, axis=0, keepdims=True)

    def score_block(j, masked):
        r0 = pl.multiple_of(j * kb, kb)
        kib = ki_ref[0, pl.ds(r0, kb), :]
        acc = jnp.zeros((kb, tq), F32)
        for h in range(IDX_HEADS):
            rel = jnp.maximum(_dot_nt(kib, qi_heads[h]), 0.0)
            acc = acc + w_rows[h:h + 1, :] * rel
        s = acc + 0.0
        if masked:
            kpos = r0 + lax.broadcasted_iota(I32, (kb, tq), 0)
            adm = (lax.shift_right_logical(kpos, CHUNK_SHIFT) <= qchunk) & (kpos < s_real)
            s = jnp.where(adm, s, -jnp.inf)
        s_scr[pl.ds(r0, kb), :] = s

    n_open = jnp.minimum((lax.shift_right_logical(q0, CHUNK_SHIFT) + 1) * CHUNK, s_real) // kb

    def open_step(j, c):
        score_block(j, False)
        return c

    def masked_step(j, c):
        score_block(j, True)
        return c

    lax.fori_loop(0, n_open, open_step, 0)
    lax.fori_loop(n_open, nblk, masked_step, 0)

    def count(pred):
        def body(j, acc):
            r0 = pl.multiple_of(j * kb, kb)
            for u in range(kb // COUNT_ROWS):
                slab = s_scr[pl.ds(pl.multiple_of(r0 + u * COUNT_ROWS, COUNT_ROWS), COUNT_ROWS), :]
                acc = acc + jnp.where(pred(slab), 1.0, 0.0)
            return acc
        acc = lax.fori_loop(0, nblk, body, jnp.zeros((COUNT_ROWS, tq), F32))
        return jnp.sum(acc, axis=0, keepdims=True)

    def key_to_score(key):
        return pltpu.bitcast(jnp.where(key < 0, key ^ 0x7FFFFFFF, key), F32)

    c0 = count(lambda blk: blk >= 0.0)
    t0 = jnp.where(c0 >= kf, 0, INT_MIN).astype(I32)

    def value_bit(i, t):
        cand = t | jnp.left_shift(jnp.int32(1), 30 - i)
        cand_f = key_to_score(cand)
        return jnp.where(count(lambda blk: blk >= cand_f) >= kf, cand, t)

    t_key = lax.fori_loop(0, 31, value_bit, t0)
    t = key_to_score(jnp.maximum(t_key, KEY_NEG_INF))

    need = kf - count(lambda blk: blk > t)
    tri = tri_ref[...]

    def select_block(j, seen):
        r0 = pl.multiple_of(j * kb, kb)
        blk = s_scr[pl.ds(r0, kb), :]
        tie = jnp.where((blk == t) & (blk > -jnp.inf), 1.0, 0.0)
        rank = _dot(tri, tie.astype(BF16)) + seen
        chosen = (blk > t) | ((tie > 0.0) & (rank <= need))
        bias_scr[pl.ds(r0, kb), :] = jnp.where(chosen, 0.0, NEG_BIG)
        return seen + fold_rows(tie, jnp.sum)

    lax.fori_loop(0, nblk, select_block, jnp.zeros((1, tq), F32))

    m_scr[...] = jnp.full(m_scr.shape, NEG_BIG, F32)
    l_scr[...] = jnp.zeros(l_scr.shape, F32)
    acc_scr[...] = jnp.zeros(acc_scr.shape, F32)

    def attend_block(j, carry):
        r0 = pl.multiple_of(j * kb, kb)
        bias = bias_scr[pl.ds(r0, kb), :]
        heads = range(A_HEADS)
        m_old = m_scr[...]
        l_old = l_scr[...]
        k_pair = [ka_ref[0, pl.ds(r0, kb), i * LANES:(i + 1) * LANES] for i in range(PAIRS)]
        v_pair = [va_ref[0, pl.ds(r0, kb), i * LANES:(i + 1) * LANES] for i in range(PAIRS)]
        ss = [_dot_nt(k_pair[h // 2], qa_heads[h]) + bias for h in heads]
        m_new = [jnp.maximum(m_old[h:h + 1, :], fold_rows(ss[h], jnp.max)) for h in heads]
        ps = [jnp.exp(ss[h] - m_new[h]) for h in heads]
        pvs = [_dot_tn(v_pair[h // 2], ps[h].astype(BF16)) for h in heads]
        alphas = [jnp.exp(m_old[h:h + 1, :] - m_new[h]) for h in heads]
        for i in range(PAIRS):
            e, o = 2 * i, 2 * i + 1
            acc_scr[i] = (jnp.where(even_rows, alphas[e], alphas[o]) * acc_scr[i]
                          + jnp.where(even_rows, pvs[e], pvs[o]))
        for h in heads:
            l_scr[h:h + 1, :] = alphas[h] * l_old[h:h + 1, :] + fold_rows(ps[h], jnp.sum)
            m_scr[h:h + 1, :] = m_new[h]
        return carry

    even_rows = lax.broadcasted_iota(I32, (LANES, tq), 0) < HEAD_DIM
    lax.fori_loop(0, nblk, attend_block, 0)

    l_fin = l_scr[...]
    for i in range(PAIRS):
        denom = jnp.where(even_rows, l_fin[2 * i:2 * i + 1, :], l_fin[2 * i + 1:2 * i + 2, :])
        o_ref[0, :, i * LANES:(i + 1) * LANES] = (acc_scr[i] / denom).T.astype(o_ref.dtype)


def _dsa(qi, wi, qa, ki2, ka, va, *, tq, kb, s_real, past, topk):
    bsz, t_len, _ = qa.shape
    s_pad = ki2.shape[1]
    kern = functools.partial(_dsa_kernel, tq=tq, kb=kb, s_real=s_real, past=past, topk=topk)
    qmap = lambda b, q: (b, q, 0)
    kmap = lambda b, q: (b, 0, 0)
    return pl.pallas_call(
        kern,
        grid=(bsz, t_len // tq),
        in_specs=[
            pl.BlockSpec((1, tq, IDX_HEADS * IDX_DIM), qmap),
            pl.BlockSpec((1, tq, LANES), qmap),
            pl.BlockSpec((1, tq, A_WIDTH), qmap),
            pl.BlockSpec((1, s_pad, LANES), kmap),
            pl.BlockSpec((1, s_pad, A_WIDTH), kmap),
            pl.BlockSpec((1, s_pad, A_WIDTH), kmap),
            pl.BlockSpec((kb, kb), lambda b, q: (0, 0)),
        ],
        out_specs=pl.BlockSpec((1, tq, A_WIDTH), qmap),
        out_shape=jax.ShapeDtypeStruct((bsz, t_len, A_WIDTH), BF16),
        scratch_shapes=[
            pltpu.VMEM((s_pad, tq), F32),
            pltpu.VMEM((s_pad, tq), F32),
            pltpu.VMEM((A_HEADS, tq), F32),
            pltpu.VMEM((A_HEADS, tq), F32),
            pltpu.VMEM((PAIRS, LANES, tq), F32),
        ],
        compiler_params=pltpu.CompilerParams(dimension_semantics=("arbitrary", "arbitrary"),
                                             vmem_limit_bytes=VMEM_LIMIT),
        name="dsa_attention",
    )(qi, wi, qa, ki2, ka, va, _lower_tri(kb))


def _stick_kernel(q_ref, k_ref, v_ref, tri_ref, o_ref, carry_scr, acc_scr, *, tq, kb, past):
    qt = pl.program_id(1)
    q0 = past + qt * tq
    nblk = (q0 + tq - 1 + kb - 1) // kb
    n_full = q0 // kb
    qpos = q0 + lax.broadcasted_iota(I32, (tq, kb), 0)
    carry_scr[...] = jnp.zeros(carry_scr.shape, F32)
    acc_scr[...] = jnp.zeros(acc_scr.shape, F32)
    tri2 = tri_ref[...]
    heads = range(B_HEADS)
    q_heads = [q for i in range(PAIRS) for q in _head_halves(q_ref[0, :, i * LANES:(i + 1) * LANES])]
    even_lanes = lax.broadcasted_iota(I32, (tq, LANES), 1) < HEAD_DIM

    def block(j, masked):
        r0 = pl.multiple_of(j * kb, kb)
        causal = ((r0 + lax.broadcasted_iota(I32, (tq, kb), 1)) < qpos) if masked else None
        carries = [carry_scr[h] for h in heads]
        k_pair = [k_ref[0, pl.ds(r0, kb), i * LANES:(i + 1) * LANES] for i in range(PAIRS)]
        v_pair = [v_ref[0, pl.ds(r0, kb), i * LANES:(i + 1) * LANES] for i in range(PAIRS)]
        zs = [_dot_nt(q_heads[h], k_pair[h // 2]) for h in heads]
        sps = []
        for z in zs:
            sp = jnp.maximum(z, 0.0) + jnp.log(1.0 + jnp.exp(-jnp.abs(z)))
            sps.append(jnp.where(causal, sp, 0.0) if masked else sp)
        sufs = []
        for h in heads:
            hi, lo = _split_bf16(sps[h])
            hilo = jnp.concatenate([hi, lo], axis=1)
            sufs.append(_dot(hilo, tri2) + jnp.tile(carries[h], (1, kb // LANES)))
        pvs = []
        for h in heads:
            a = jnp.exp(zs[h] - sufs[h])
            if masked:
                a = jnp.where(causal, a, 0.0)
            pvs.append(_dot(a.astype(BF16), v_pair[h // 2]))
        for i in range(PAIRS):
            acc_scr[i] = acc_scr[i] + jnp.where(even_lanes, pvs[2 * i], pvs[2 * i + 1])
        for h in heads:
            carry_scr[h] = carries[h] + jnp.sum(sps[h], axis=-1, keepdims=True)

    def masked_step(jj, c):
        block(nblk - 1 - jj, True)
        return c

    def full_step(jj, c):
        block(n_full - 1 - jj, False)
        return c

    lax.fori_loop(0, nblk - n_full, masked_step, 0)
    lax.fori_loop(0, n_full, full_step, 0)
    for i in range(PAIRS):
        o_ref[0, :, i * LANES:(i + 1) * LANES] = acc_scr[i].astype(o_ref.dtype)


def _stick(q, k, v, *, tq, kb, past):
    bsz, t_len, _ = q.shape
    s_pad = k.shape[1]
    kern = functools.partial(_stick_kernel, tq=tq, kb=kb, past=past)
    kmap = lambda b, q: (b, 0, 0)
    return pl.pallas_call(
        kern,
        grid=(bsz, t_len // tq),
        in_specs=[
            pl.BlockSpec((1, tq, B_WIDTH), lambda b, q: (b, q, 0)),
            pl.BlockSpec((1, s_pad, B_WIDTH), kmap),
            pl.BlockSpec((1, s_pad, B_WIDTH), kmap),
            pl.BlockSpec((2 * kb, kb), lambda b, q: (0, 0)),
        ],
        out_specs=pl.BlockSpec((1, tq, B_WIDTH), lambda b, q: (b, q, 0)),
        out_shape=jax.ShapeDtypeStruct((bsz, t_len, B_WIDTH), BF16),
        scratch_shapes=[
            pltpu.VMEM((B_HEADS, tq, LANES), F32),
            pltpu.VMEM((PAIRS, tq, LANES), F32),
        ],
        compiler_params=pltpu.CompilerParams(dimension_semantics=("arbitrary", "arbitrary"),
                                             vmem_limit_bytes=VMEM_LIMIT),
        name="stick_attention",
    )(q, k, v, jnp.tile(_lower_tri(kb), (2, 1)))


def _merge_kernel(x_ref, oa_ref, ob_ref, gates_ref, mod_ref, wua_ref, wub_ref, wout_ref, n2g_ref, wr_ref, br_ref,
                  y_ref, h2_ref, comb_ref, *, nb):
    g1 = mod_ref[:, 2, :]
    sh2 = mod_ref[:, 3, :]
    sc2 = mod_ref[:, 4, :]
    gates = gates_ref[...].astype(F32)
    merged = gates[:, :D_MODEL] * _dot(oa_ref[...], wua_ref[...]) + gates[:, D_MODEL:] * _dot(ob_ref[...], wub_ref[...])
    y = x_ref[...] + _rows_times(_dot(merged.astype(BF16), wout_ref[...]), g1, nb)
    y_ref[...] = y
    h2 = _rows_plus(_rows_times(_rms_rows(y) * n2g_ref[...], 1.0 + sc2, nb), sh2, nb)
    h2_ref[...] = h2.astype(BF16)

    logits = _dot3(h2, wr_ref[...]) + br_ref[...]
    lane = lax.broadcasted_iota(I32, logits.shape, 1).astype(F32)
    far = float(ROUTER_LANES)
    is_g = lane < float(N_GROUPS)
    gl = jnp.where(is_g, logits, -jnp.inf)
    gmax = jnp.max(gl, axis=-1, keepdims=True)
    gsel = jnp.min(jnp.where(is_g & (gl == gmax), lane, far), axis=-1, keepdims=True)
    p_group = 1.0 / jnp.sum(jnp.where(is_g, jnp.exp(gl - gmax), 0.0), axis=-1, keepdims=True)
    e_lo = float(EXPERT_LANE0) + gsel * float(EXPERTS_PER_GROUP)
    in_grp = (lane >= e_lo) & (lane < e_lo + float(EXPERTS_PER_GROUP))
    el = jnp.where(in_grp, logits, -jnp.inf)
    v1 = jnp.max(el, axis=-1, keepdims=True)
    i1 = jnp.min(jnp.where(in_grp & (el == v1), lane, far), axis=-1, keepdims=True)
    el2 = jnp.where(lane == i1, -jnp.inf, el)
    v2 = jnp.max(el2, axis=-1, keepdims=True)
    i2 = jnp.min(jnp.where(in_grp & (lane != i1) & (el2 == v2), lane, far), axis=-1, keepdims=True)
    e2 = jnp.exp(v2 - v1)
    w1 = p_group / (1.0 + e2)
    w2 = p_group * e2 / (1.0 + e2)
    comb_ref[...] = jnp.where(lane == i1, w1, 0.0) + jnp.where(lane == i2, w2, 0.0)


def _merge(x2, oa, ob, gates, mod, wua, wub, wout, n2g, wr, br, *, t_len, tm):
    n = x2.shape[0]
    nb = max(1, tm // t_len)
    tiles_per_batch = max(1, t_len // tm)
    row = lambda i: (i, 0)
    const2 = lambda i: (0, 0)
    return pl.pallas_call(
        functools.partial(_merge_kernel, nb=nb),
        grid=(n // tm,),
        in_specs=[
            pl.BlockSpec((tm, D_MODEL), row),
            pl.BlockSpec((tm, A_WIDTH), row),
            pl.BlockSpec((tm, B_WIDTH), row),
            pl.BlockSpec((tm, 2 * D_MODEL), row),
            pl.BlockSpec((nb, 6, D_MODEL), lambda i: (i // tiles_per_batch, 0, 0)),
            pl.BlockSpec(wua.shape, const2),
            pl.BlockSpec(wub.shape, const2),
            pl.BlockSpec(wout.shape, const2),
            pl.BlockSpec((1, D_MODEL), const2),
            pl.BlockSpec(wr.shape, const2),
            pl.BlockSpec((1, ROUTER_LANES), const2),
        ],
        out_specs=[pl.BlockSpec((tm, D_MODEL), row), pl.BlockSpec((tm, D_MODEL), row),
                   pl.BlockSpec((tm, ROUTER_LANES), row)],
        out_shape=[jax.ShapeDtypeStruct((n, D_MODEL), F32), jax.ShapeDtypeStruct((n, D_MODEL), BF16),
                   jax.ShapeDtypeStruct((n, ROUTER_LANES), F32)],
        compiler_params=pltpu.CompilerParams(dimension_semantics=("arbitrary",), vmem_limit_bytes=VMEM_LIMIT),
        name="merge_router",
    )(x2, oa, ob, gates, mod, wua, wub, wout, n2g, wr, br)


def _moe_kernel(y_ref, h2_ref, comb_ref, mod_ref, wg_ref, wu_ref, wd_ref, o_ref, acc_scr, *, nb):
    e = pl.program_id(1)

    @pl.when(e == 0)
    def _():
        acc_scr[...] = jnp.zeros(acc_scr.shape, F32)

    h2 = h2_ref[...]
    act = _silu(_dot(h2, wg_ref[0].astype(BF16))) * _dot(h2, wu_ref[0].astype(BF16))
    contrib = _dot(act.astype(BF16), wd_ref[0].astype(BF16))
    comb = comb_ref[...]
    lane = lax.broadcasted_iota(I32, comb.shape, 1)
    w_e = jnp.sum(jnp.where(lane == e + EXPERT_LANE0, comb, 0.0), axis=-1, keepdims=True)
    acc_scr[...] += w_e * contrib

    @pl.when(e == pl.num_programs(1) - 1)
    def _():
        o_ref[...] = y_ref[...] + _rows_times(acc_scr[...], mod_ref[:, 5, :], nb)


def _moe(y, h2, comb, mod, w_gate, w_up, w_down, *, t_len, tm):
    n = y.shape[0]
    nb = max(1, tm // t_len)
    tiles_per_batch = max(1, t_len // tm)
    row = lambda i, e: (i, 0)
    return pl.pallas_call(
        functools.partial(_moe_kernel, nb=nb),
        grid=(n // tm, N_EXPERTS),
        in_specs=[
            pl.BlockSpec((tm, D_MODEL), row),
            pl.BlockSpec((tm, D_MODEL), row),
            pl.BlockSpec((tm, ROUTER_LANES), row),
            pl.BlockSpec((nb, 6, D_MODEL), lambda i, e: (i // tiles_per_batch, 0, 0)),
            pl.BlockSpec((1, D_MODEL, D_EXPERT), lambda i, e: (e, 0, 0)),
            pl.BlockSpec((1, D_MODEL, D_EXPERT), lambda i, e: (e, 0, 0)),
            pl.BlockSpec((1, D_EXPERT, D_MODEL), lambda i, e: (e, 0, 0)),
        ],
        out_specs=pl.BlockSpec((tm, D_MODEL), row),
        out_shape=jax.ShapeDtypeStruct((n, D_MODEL), F32),
        scratch_shapes=[pltpu.VMEM((tm, D_MODEL), F32)],
        compiler_params=pltpu.CompilerParams(dimension_semantics=("arbitrary", "arbitrary"),
                                             vmem_limit_bytes=VMEM_LIMIT),
        name="moe_dense",
    )(y, h2, comb, mod, w_gate, w_up, w_down)


def _rope_tables(pos):
    half = ROT_DIM // 2
    freqs = ROPE_THETA ** (-jnp.arange(0, ROT_DIM, 2, dtype=F32) / ROT_DIM)
    ang = pos.astype(F32)[:, None] * freqs[None, :]
    cos, sin = jnp.cos(ang), jnp.sin(ang)
    dh = jnp.arange(LANES) % HEAD_DIM
    fi = dh % half
    cos_t = jnp.where(dh < ROT_DIM, cos[:, fi], 1.0)
    sin_lo = jnp.where(dh < half, -sin[:, fi], 0.0)
    sin_hi = jnp.where((dh >= half) & (dh < ROT_DIM), sin[:, fi], 0.0)
    return jnp.stack([cos_t, sin_lo, sin_hi]).astype(F32)


def _round_up(a, b):
    return (a + b - 1) // b * b


def _layer(x, mod, past_ak, past_av, past_aki, past_bk, past_bv, w, *, cfg):
    bsz, t_len, _ = x.shape
    past = past_ak.shape[1]
    s_real = past + t_len
    n = bsz * t_len
    x2 = x.reshape(n, D_MODEL)
    pos = past + jnp.arange(t_len, dtype=jnp.int32)

    tm = cfg["tm"]
    tab = jnp.tile(_rope_tables(pos), (1, max(1, tm // t_len), 1))
    pr = _proj(x2, mod, w["n1g"], w["wa"], w["wv"], w["wi"], w["wg"], w["bd"], w["gqk"], tab, t_len=t_len, tm=tm)

    new_ak = pr["ka"].reshape(bsz, t_len, A_HEADS, HEAD_DIM)
    new_av = pr["va"].reshape(bsz, t_len, A_HEADS, HEAD_DIM)
    new_aki = pr["ki2"][:, :IDX_DIM].reshape(bsz, t_len, IDX_DIM)
    new_bk = pr["kb"].reshape(bsz, t_len, B_HEADS, HEAD_DIM)
    new_bv = pr["vb"].reshape(bsz, t_len, B_HEADS, HEAD_DIM)

    def all_keys(past_x, new_bf16, s_pad):
        width = new_bf16.shape[-1]
        parts = [new_bf16.reshape(bsz, t_len, width)]
        if past:
            parts.insert(0, past_x.astype(BF16))
        if s_pad > s_real:
            parts.append(jnp.zeros((bsz, s_pad - s_real, width), BF16))
        return parts[0] if len(parts) == 1 else jnp.concatenate(parts, axis=1)

    tq, kb = cfg["dsa_tq"], cfg["dsa_kb"]
    s_pad = _round_up(s_real, kb)
    topk = min(TOPK_MAX, s_real // 4)
    past_ki2 = jnp.concatenate([past_aki, past_aki], axis=-1) if past else None
    oa = _dsa(pr["qi"].reshape(bsz, t_len, -1), pr["wi"].reshape(bsz, t_len, LANES), pr["qa"].reshape(bsz, t_len, -1),
              all_keys(past_ki2, pr["ki2b"], s_pad),
              all_keys(past_ak.reshape(bsz, past, A_WIDTH), pr["kab"], s_pad),
              all_keys(past_av.reshape(bsz, past, A_WIDTH), pr["vab"], s_pad),
              tq=tq, kb=kb, s_real=s_real, past=past, topk=topk)

    tqb, kbb = cfg["sb_tq"], cfg["sb_kb"]
    s_pad_b = _round_up(s_real, kbb)
    ob = _stick(pr["qb"].reshape(bsz, t_len, -1),
                all_keys(past_bk.reshape(bsz, past, B_WIDTH), pr["kbb"], s_pad_b),
                all_keys(past_bv.reshape(bsz, past, B_WIDTH), pr["vbb"], s_pad_b),
                tq=tqb, kb=kbb, past=past)

    y, h2, comb = _merge(x2, oa.reshape(n, A_WIDTH), ob.reshape(n, B_WIDTH), pr["gates"], mod,
                         w["wua"], w["wub"], w["wout"], w["n2g"], w["wr"], w["br"], t_len=t_len, tm=cfg["tm_merge"])
    out = _moe(y, h2, comb, mod, w["w_gate"], w["w_up"], w["w_down"], t_len=t_len, tm=cfg["tm_moe"])
    return out.reshape(bsz, t_len, D_MODEL), (new_ak, new_av, new_aki, new_bk, new_bv)


def _prep_weights(w_in, norm1_g, qnorm_g, knorm_g, w_up_a, w_up_b, w_out, norm2_g, w_rg, b_rg, w_re, b_re,
                  w_e_gate, w_e_up, w_e_down):
    o = 0
    seg = {}
    for name, width in (("qa", A_WIDTH), ("ka", A_WIDTH), ("va", A_WIDTH), ("qi", IDX_HEADS * IDX_DIM),
                        ("ki", IDX_DIM), ("wi", IDX_HEADS), ("qb", B_WIDTH), ("kb", B_WIDTH), ("vb", B_WIDTH),
                        ("gates", 2 * D_MODEL)):
        seg[name] = w_in[:, o:o + width]
        o += width
    pad_i = jnp.zeros((D_MODEL, LANES - IDX_HEADS), w_in.dtype)
    hd = jnp.arange(2 * A_WIDTH) // HEAD_DIM
    n_r = N_GROUPS + N_EXPERTS
    return dict(
        n1g=norm1_g.reshape(1, D_MODEL),
        wa=jnp.concatenate([seg["qa"], seg["ka"]], axis=1).astype(BF16),
        wv=jnp.concatenate([seg["va"], seg["qb"], seg["kb"], seg["vb"]], axis=1).astype(BF16),
        wi=jnp.concatenate([seg["qi"], seg["ki"], seg["ki"], seg["wi"], pad_i], axis=1).astype(BF16),
        wg=seg["gates"].astype(BF16),
        bd=((hd[:, None] == hd[None, :]).astype(F32) / HEAD_DIM).astype(BF16),
        gqk=jnp.concatenate([jnp.tile(qnorm_g, A_HEADS), jnp.tile(knorm_g, A_HEADS)]).reshape(1, 2 * A_WIDTH),
        wua=w_up_a.astype(BF16), wub=w_up_b.astype(BF16), wout=w_out.astype(BF16),
        n2g=norm2_g.reshape(1, D_MODEL),
        wr=jnp.pad(jnp.concatenate([w_rg, w_re], axis=1), ((0, 0), (0, ROUTER_LANES - n_r))),
        br=jnp.pad(jnp.concatenate([b_rg, b_re]), (0, ROUTER_LANES - n_r)).reshape(1, ROUTER_LANES),
        w_gate=w_e_gate, w_up=w_e_up, w_down=w_e_down,
    )


def _group_cfg(t_len):
    if t_len >= 512:
        return dict(tm=256, tm_merge=256, tm_moe=1024, dsa_tq=256, dsa_kb=512, sb_tq=256, sb_kb=256)
    return dict(tm=512, tm_merge=512, tm_moe=512, dsa_tq=t_len, dsa_kb=384, sb_tq=t_len, sb_kb=256)


def kernel(x_prompt, x_sample, cache_a_k, cache_a_v, cache_a_kidx, cache_b_k, cache_b_v, c_prompt, c_sample,
           w_ada, b_ada, norm1_g, w_in, qnorm_g, knorm_g, w_up_a, w_up_b, w_out, norm2_g,
           w_rg, b_rg, w_re, b_re, w_e_gate, w_e_up, w_e_down):
    depth = w_ada.shape[0]
    bp, bs = x_prompt.shape[0], x_sample.shape[0]
    dt = x_prompt.dtype
    rows = _round_up(bp + bs, 8)
    empty_a = jnp.zeros((bp, 0, A_HEADS, HEAD_DIM), dt)
    empty_i = jnp.zeros((bp, 0, IDX_DIM), dt)
    empty_b = jnp.zeros((bp, 0, B_HEADS, HEAD_DIM), dt)
    y_p, y_s = x_prompt, x_sample
    rows_p, rows_s = [], []
    c_all = jnp.concatenate([c_prompt, c_sample, jnp.zeros((rows - bp - bs, D_MODEL), dt)], axis=0)
    for l in range(depth):
        w = _prep_weights(w_in[l], norm1_g[l], qnorm_g[l], knorm_g[l], w_up_a[l], w_up_b[l], w_out[l], norm2_g[l],
                          w_rg[l], b_rg[l], w_re[l], b_re[l], w_e_gate[l], w_e_up[l], w_e_down[l])
        mod = _ada(c_all, w_ada[l], b_ada[l].reshape(1, -1))
        mod_p = mod[:bp].reshape(bp, 6, D_MODEL)
        mod_s = mod[bp:bp + bs].reshape(bs, 6, D_MODEL)
        y_p, new_p = _layer(y_p, mod_p, empty_a, empty_a, empty_i, empty_b, empty_b, w,
                            cfg=_group_cfg(y_p.shape[1]))
        y_s, new_s = _layer(y_s, mod_s, cache_a_k[l], cache_a_v[l], cache_a_kidx[l], cache_b_k[l], cache_b_v[l], w,
                            cfg=_group_cfg(y_s.shape[1]))
        rows_p.append(new_p)
        rows_s.append(new_s)
    stack = lambda rws, i: jnp.stack([r[i] for r in rws])
    return (y_p, y_s) + tuple(stack(rows_p, i) for i in range(5)) + tuple(stack(rows_s, i) for i in range(5))
```

```python
import functools

import jax
import jax.numpy as jnp
from jax import lax
from jax.experimental import pallas as pl
from jax.experimental.pallas import tpu as pltpu

D_MODEL = 1024
CHUNK = 64
CHUNK_SHIFT = 6
HEAD_DIM = 64
A_HEADS = 8
B_HEADS = 8
IDX_HEADS = 4
IDX_DIM = 64
TOPK_MAX = 256
ROPE_THETA = 500000.0
ROT_DIM = HEAD_DIM // 4
N_GROUPS = 4
EXPERTS_PER_GROUP = 8
N_EXPERTS = N_GROUPS * EXPERTS_PER_GROUP
D_EXPERT = 256
RMS_EPS = 1e-6
A_WIDTH = A_HEADS * HEAD_DIM
B_WIDTH = B_HEADS * HEAD_DIM

LANES = 128
CACHE_OUTS = ("ka", "va", "kb", "vb")
IDX_SEG = 512
PAIRS = 4
COUNT_ROWS = 64
ROUTER_LANES = 128
EXPERT_LANE0 = N_GROUPS
EXPERTS_PER_STEP = 2
VMEM_LIMIT = 56 * 1024 * 1024

F32 = jnp.float32
BF16 = jnp.bfloat16
I32 = jnp.int32
NEG_BIG = -1e30
INT_MIN = -(2 ** 31)
KEY_NEG_INF = (0xFF800000 - (1 << 32)) ^ 0x7FFFFFFF


def _dot(a, b):
    return jnp.dot(a, b, preferred_element_type=F32)


def _dot_nt(a, b):
    return lax.dot_general(a, b, (((1,), (1,)), ((), ())), preferred_element_type=F32)


def _dot_tn(a, b):
    return lax.dot_general(a, b, (((0,), (0,)), ((), ())), preferred_element_type=F32)


def _split_bf16(a):
    hi = a.astype(BF16)
    lo = (a - hi.astype(F32)).astype(BF16)
    return hi, lo


def _dot3(a, b):
    ah, al = _split_bf16(a)
    bh, bl = _split_bf16(b)
    return _dot(ah, bh) + (_dot(ah, bl) + _dot(al, bh))


def _silu(x):
    return x * (1.0 / (1.0 + jnp.exp(-x)))


def _sigmoid(x):
    return 1.0 / (1.0 + jnp.exp(-x))


def _rows_times(x, vec, nb):
    if nb == 1:
        return x * vec
    tm, d = x.shape
    return (x.reshape(nb, tm // nb, d) * vec[:, None, :]).reshape(tm, d)


def _rows_plus(x, vec, nb):
    if nb == 1:
        return x + vec
    tm, d = x.shape
    return (x.reshape(nb, tm // nb, d) + vec[:, None, :]).reshape(tm, d)


def _rms_rows(x):
    return x * lax.rsqrt(jnp.mean(x * x, axis=-1, keepdims=True) + RMS_EPS)


def _ada_kernel(c_ref, w_ref, b_ref, o_ref):
    o_ref[...] = _dot3(_silu(c_ref[...]), w_ref[...]) + b_ref[...]


def _ada(c_all, w_ada, b_ada):
    rows = c_all.shape[0]
    n = w_ada.shape[1]
    tn = 1536
    return pl.pallas_call(
        _ada_kernel,
        grid=(n // tn,),
        in_specs=[pl.BlockSpec((rows, D_MODEL), lambda j: (0, 0)),
                  pl.BlockSpec((D_MODEL, tn), lambda j: (0, j)),
                  pl.BlockSpec((1, tn), lambda j: (0, j))],
        out_specs=pl.BlockSpec((rows, tn), lambda j: (0, j)),
        out_shape=jax.ShapeDtypeStruct((rows, n), F32),
        compiler_params=pltpu.CompilerParams(dimension_semantics=("arbitrary",), vmem_limit_bytes=VMEM_LIMIT),
        name="ada_mod",
    )(c_all, w_ada, b_ada)


def _rope_lanes(a, tab_ref, reps):
    width = a.shape[-1]
    cos = jnp.tile(tab_ref[0], (1, reps)) if reps > 1 else tab_ref[0]
    sin_lo = jnp.tile(tab_ref[1], (1, reps)) if reps > 1 else tab_ref[1]
    sin_hi = jnp.tile(tab_ref[2], (1, reps)) if reps > 1 else tab_ref[2]
    half = ROT_DIM // 2
    return a * cos + pltpu.roll(a, width - half, 1) * sin_lo + pltpu.roll(a, half, 1) * sin_hi


def _proj_kernel(x_ref, mod_ref, n1g_ref, wa_ref, wv_ref, wi_ref, wg_ref, bd_ref, gqk_ref, tab_ref,
                 qa_ref, ka_ref, kab_ref, va_ref, vab_ref, qb_ref, kb_ref, kbb_ref, vb_ref, vbb_ref,
                 qi_ref, ki2_ref, ki2b_ref, wi_out_ref, gates_ref, *, nb):
    x = x_ref[...]
    sh1 = mod_ref[:, 0, :]
    sc1 = mod_ref[:, 1, :]
    h = _rows_plus(_rows_times(_rms_rows(x) * n1g_ref[...], 1.0 + sc1, nb), sh1, nb)
    hb = h.astype(BF16)

    a = _dot(hb, wa_ref[...])
    ms = _dot((a * a).astype(BF16), bd_ref[...])
    a = a * lax.rsqrt(ms + RMS_EPS) * gqk_ref[...]
    a = _rope_lanes(a, tab_ref, a.shape[-1] // LANES)
    qa_ref[...] = (a[:, :A_WIDTH] * (HEAD_DIM ** -0.5)).astype(BF16)
    ka = a[:, A_WIDTH:]
    ka_ref[...] = ka.reshape(ka_ref.shape)
    kab_ref[...] = ka.astype(BF16)

    v = _dot(hb, wv_ref[...])
    va = v[:, :A_WIDTH]
    va_ref[...] = va.reshape(va_ref.shape)
    vab_ref[...] = va.astype(BF16)
    qb_ref[...] = (v[:, A_WIDTH:A_WIDTH + B_WIDTH] * (HEAD_DIM ** -0.5)).astype(BF16)
    kb = v[:, A_WIDTH + B_WIDTH:A_WIDTH + 2 * B_WIDTH]
    kb_ref[...] = kb.reshape(kb_ref.shape)
    kbb_ref[...] = kb.astype(BF16)
    vb = v[:, A_WIDTH + 2 * B_WIDTH:]
    vb_ref[...] = vb.reshape(vb_ref.shape)
    vbb_ref[...] = vb.astype(BF16)

    i = _dot(hb, wi_ref[...])
    nr = IDX_SEG - LANES
    ir = _rope_lanes(i[:, :nr], tab_ref, nr // LANES)
    nq = IDX_HEADS * IDX_DIM
    qi_ref[...] = (ir[:, :nq] * (IDX_DIM ** -0.5)).astype(BF16)
    ki2 = ir[:, nq:]
    ki2_ref[...] = ki2
    ki2b_ref[...] = ki2.astype(BF16)
    wi_out_ref[...] = i[:, nr:]

    gates_ref[...] = _sigmoid(_dot(hb, wg_ref[...])).astype(BF16)


def _proj(x2, mod, n1g, wa, wv, wi, wg, bd, gqk, tab, *, t_len, tm):
    n = x2.shape[0]
    nb = max(1, tm // t_len)
    tiles_per_batch = max(1, t_len // tm)
    tab_tiles = tab.shape[1] // tm
    row = lambda i: (i, 0)
    const2 = lambda i: (0, 0)
    tabmap = lambda i: (0, i % tab_tiles, 0)
    in_specs = [
        pl.BlockSpec((tm, D_MODEL), row),
        pl.BlockSpec((nb, 6, D_MODEL), lambda i: (i // tiles_per_batch, 0, 0)),
        pl.BlockSpec((1, D_MODEL), const2),
        pl.BlockSpec(wa.shape, const2),
        pl.BlockSpec(wv.shape, const2),
        pl.BlockSpec(wi.shape, const2),
        pl.BlockSpec(wg.shape, const2),
        pl.BlockSpec(bd.shape, const2),
        pl.BlockSpec((1, 2 * A_WIDTH), const2),
        pl.BlockSpec((3, tm, LANES), tabmap),
    ]
    outs = [
        ("qa", A_WIDTH, BF16), ("ka", A_WIDTH, F32), ("kab", A_WIDTH, BF16),
        ("va", A_WIDTH, F32), ("vab", A_WIDTH, BF16),
        ("qb", B_WIDTH, BF16), ("kb", B_WIDTH, F32), ("kbb", B_WIDTH, BF16),
        ("vb", B_WIDTH, F32), ("vbb", B_WIDTH, BF16),
        ("qi", IDX_HEADS * IDX_DIM, BF16), ("ki2", LANES, F32), ("ki2b", LANES, BF16), ("wi", LANES, F32),
        ("gates", 2 * D_MODEL, BF16),
    ]
    res = pl.pallas_call(
        functools.partial(_proj_kernel, nb=nb),
        grid=(n // tm,),
        in_specs=in_specs,
        out_specs=[pl.BlockSpec((tm, w // HEAD_DIM, HEAD_DIM), lambda i: (i, 0, 0)) if name in CACHE_OUTS
                   else pl.BlockSpec((tm, w), row) for name, w, _ in outs],
        out_shape=[jax.ShapeDtypeStruct((n, w // HEAD_DIM, HEAD_DIM) if name in CACHE_OUTS else (n, w), dt)
                   for name, w, dt in outs],
        compiler_params=pltpu.CompilerParams(dimension_semantics=("arbitrary",), vmem_limit_bytes=VMEM_LIMIT),
        name="in_proj",
    )(x2, mod, n1g, wa, wv, wi, wg, bd, gqk, tab)
    return dict(zip([o[0] for o in outs], res))


def _lower_tri(n):
    idx = jnp.arange(n)
    return (idx[:, None] >= idx[None, :]).astype(BF16)


def _head_halves(x):
    lane = lax.broadcasted_iota(I32, x.shape, 1)
    zero = jnp.zeros_like(x)
    return jnp.where(lane < HEAD_DIM, x, zero), jnp.where(lane >= HEAD_DIM, x, zero)


def _dsa_kernel(qi_ref, w_ref, qa_ref, ki_ref, ka_ref, va_ref, tri_ref, o_ref,
                s_scr, bias_scr, m_scr, l_scr, acc_scr, *, tq, kb, s_real, past, topk):
    qt = pl.program_id(1)
    q0 = past + qt * tq
    kmax = jnp.minimum((lax.shift_right_logical(q0 + tq - 1, CHUNK_SHIFT) + 1) * CHUNK, s_real)
    nblk = (kmax + kb - 1) // kb
    qchunk = lax.shift_right_logical(q0 + lax.broadcasted_iota(I32, (1, tq), 1), CHUNK_SHIFT)
    kf = float(topk)
    chunks = lambda ref, i: ref[0, :, i * LANES:(i + 1) * LANES]
    qi_heads = [q for i in range(IDX_HEADS // 2) for q in _head_halves(chunks(qi_ref, i))]
    qa_heads = [q for i in range(PAIRS) for q in _head_halves(chunks(qa_ref, i))]
    w_rows = w_ref[0].T

    def fold_rows(x, op):
        part = op(x.reshape(kb // COUNT_ROWS, COUNT_ROWS, tq), axis=0)
        return op(part, axis=0, keepdims=True)

    def score_block(j, masked):
        r0 = pl.multiple_of(j * kb, kb)
        kib = ki_ref[0, pl.ds(r0, kb), :]
        acc = jnp.zeros((kb, tq), F32)
        for h in range(IDX_HEADS):
            rel = jnp.maximum(_dot_nt(kib, qi_heads[h]), 0.0)
            acc = acc + w_rows[h:h + 1, :] * rel
        s = acc + 0.0
        if masked:
            kpos = r0 + lax.broadcasted_iota(I32, (kb, tq), 0)
            adm = (lax.shift_right_logical(kpos, CHUNK_SHIFT) <= qchunk) & (kpos < s_real)
            s = jnp.where(adm, s, -jnp.inf)
        s_scr[pl.ds(r0, kb), :] = s

    n_open = jnp.minimum((lax.shift_right_logical(q0, CHUNK_SHIFT) + 1) * CHUNK, s_real) // kb

    def open_step(j, c):
        score_block(j, False)
        return c

    def masked_step(j, c):
        score_block(j, True)
        return c

    lax.fori_loop(0, n_open, open_step, 0)
    lax.fori_loop(n_open, nblk, masked_step, 0)

    def count(pred):
        def body(j, acc):
            r0 = pl.multiple_of(j * kb, kb)
            for u in range(kb // COUNT_ROWS):
                slab = s_scr[pl.ds(pl.multiple_of(r0 + u * COUNT_ROWS, COUNT_ROWS), COUNT_ROWS), :]
                acc = acc + jnp.where(pred(slab), 1.0, 0.0)
            return acc
        acc = lax.fori_loop(0, nblk, body, jnp.zeros((COUNT_ROWS, tq), F32))
        return jnp.sum(acc, axis=0, keepdims=True)

    def key_to_score(key):
        return pltpu.bitcast(jnp.where(key < 0, key ^ 0x7FFFFFFF, key), F32)

    c0 = count(lambda blk: blk >= 0.0)
    t0 = jnp.where(c0 >= kf, 0, INT_MIN).astype(I32)

    def value_bit(i, t):
        cand = t | jnp.left_shift(jnp.int32(1), 30 - i)
        cand_f = key_to_score(cand)
        return jnp.where(count(lambda blk: blk >= cand_f) >= kf, cand, t)

    t_key = lax.fori_loop(0, 31, value_bit, t0)
    t = key_to_score(jnp.maximum(t_key, KEY_NEG_INF))

    need = kf - count(lambda blk: blk > t)
    tri = tri_ref[...]

    def select_block(j, seen):
        r0 = pl.multiple_of(j * kb, kb)
        blk = s_scr[pl.ds(r0, kb), :]
        tie = jnp.where((blk == t) & (blk > -jnp.inf), 1.0, 0.0)
        rank = _dot(tri, tie.astype(BF16)) + seen
        chosen = (blk > t) | ((tie > 0.0) & (rank <= need))
        bias_scr[pl.ds(r0, kb), :] = jnp.where(chosen, 0.0, NEG_BIG)
        return seen + fold_rows(tie, jnp.sum)

    lax.fori_loop(0, nblk, select_block, jnp.zeros((1, tq), F32))

    m_scr[...] = jnp.full(m_scr.shape, NEG_BIG, F32)
    l_scr[...] = jnp.zeros(l_scr.shape, F32)
    acc_scr[...] = jnp.zeros(acc_scr.shape, F32)

    def attend_block(j, carry):
        r0 = pl.multiple_of(j * kb, kb)
        bias = bias_scr[pl.ds(r0, kb), :]
        heads = range(A_HEADS)
        m_old = m_scr[...]
        l_old = l_scr[...]
        k_pair = [ka_ref[0, pl.ds(r0, kb), i * LANES:(i + 1) * LANES] for i in range(PAIRS)]
        v_pair = [va_ref[0, pl.ds(r0, kb), i * LANES:(i + 1) * LANES] for i in range(PAIRS)]
        ss = [_dot_nt(k_pair[h // 2], qa_heads[h]) + bias for h in heads]
        m_new = [jnp.maximum(m_old[h:h + 1, :], fold_rows(ss[h], jnp.max)) for h in heads]
        ps = [jnp.exp(ss[h] - m_new[h]) for h in heads]
        pvs = [_dot_tn(v_pair[h // 2], ps[h].astype(BF16)) for h in heads]
        alphas = [jnp.exp(m_old[h:h + 1, :] - m_new[h]) for h in heads]
        for i in range(PAIRS):
            e, o = 2 * i, 2 * i + 1
            acc_scr[i] = (jnp.where(even_rows, alphas[e], alphas[o]) * acc_scr[i]
                          + jnp.where(even_rows, pvs[e], pvs[o]))
        for h in heads:
            l_scr[h:h + 1, :] = alphas[h] * l_old[h:h + 1, :] + fold_rows(ps[h], jnp.sum)
            m_scr[h:h + 1, :] = m_new[h]
        return carry

    even_rows = lax.broadcasted_iota(I32, (LANES, tq), 0) < HEAD_DIM
    lax.fori_loop(0, nblk, attend_block, 0)

    l_fin = l_scr[...]
    for i in range(PAIRS):
        denom = jnp.where(even_rows, l_fin[2 * i:2 * i + 1, :], l_fin[2 * i + 1:2 * i + 2, :])
        o_ref[0, :, i * LANES:(i + 1) * LANES] = (acc_scr[i] / denom).T.astype(o_ref.dtype)


def _dsa(qi, wi, qa, ki2, ka, va, *, tq, kb, s_real, past, topk):
    bsz, t_len, _ = qa.shape
    s_pad = ki2.shape[1]
    kern = functools.partial(_dsa_kernel, tq=tq, kb=kb, s_real=s_real, past=past, topk=topk)
    qmap = lambda b, q: (b, q, 0)
    kmap = lambda b, q: (b, 0, 0)
    return pl.pallas_call(
        kern,
        grid=(bsz, t_len // tq),
        in_specs=[
            pl.BlockSpec((1, tq, IDX_HEADS * IDX_DIM), qmap),
            pl.BlockSpec((1, tq, LANES), qmap),
            pl.BlockSpec((1, tq, A_WIDTH), qmap),
            pl.BlockSpec((1, s_pad, LANES), kmap),
            pl.BlockSpec((1, s_pad, A_WIDTH), kmap),
            pl.BlockSpec((1, s_pad, A_WIDTH), kmap),
            pl.BlockSpec((kb, kb), lambda b, q: (0, 0)),
        ],
        out_specs=pl.BlockSpec((1, tq, A_WIDTH), qmap),
        out_shape=jax.ShapeDtypeStruct((bsz, t_len, A_WIDTH), BF16),
        scratch_shapes=[
            pltpu.VMEM((s_pad, tq), F32),
            pltpu.VMEM((s_pad, tq), F32),
            pltpu.VMEM((A_HEADS, tq), F32),
            pltpu.VMEM((A_HEADS, tq), F32),
            pltpu.VMEM((PAIRS, LANES, tq), F32),
        ],
        compiler_params=pltpu.CompilerParams(dimension_semantics=("arbitrary", "arbitrary"),
                                             vmem_limit_bytes=VMEM_LIMIT),
        name="dsa_attention",
    )(qi, wi, qa, ki2, ka, va, _lower_tri(kb))


def _stick_kernel(q_ref, k_ref, v_ref, tri_ref, o_ref, carry_scr, acc_scr, *, tq, kb, past):
    qt = pl.program_id(1)
    q0 = past + qt * tq
    nblk = (q0 + tq - 1 + kb - 1) // kb
    n_full = q0 // kb
    qpos = q0 + lax.broadcasted_iota(I32, (tq, kb), 0)
    carry_scr[...] = jnp.zeros(carry_scr.shape, F32)
    acc_scr[...] = jnp.zeros(acc_scr.shape, F32)
    tri2 = tri_ref[...]
    heads = range(B_HEADS)
    q_heads = [q for i in range(PAIRS) for q in _head_halves(q_ref[0, :, i * LANES:(i + 1) * LANES])]
    even_lanes = lax.broadcasted_iota(I32, (tq, LANES), 1) < HEAD_DIM

    def block(j, masked):
        r0 = pl.multiple_of(j * kb, kb)
        causal = ((r0 + lax.broadcasted_iota(I32, (tq, kb), 1)) < qpos) if masked else None
        carries = [carry_scr[h] for h in heads]
        k_pair = [k_ref[0, pl.ds(r0, kb), i * LANES:(i + 1) * LANES] for i in range(PAIRS)]
        v_pair = [v_ref[0, pl.ds(r0, kb), i * LANES:(i + 1) * LANES] for i in range(PAIRS)]
        zs = [_dot_nt(q_heads[h], k_pair[h // 2]) for h in heads]
        sps = []
        for z in zs:
            sp = jnp.maximum(z, 0.0) + jnp.log(1.0 + jnp.exp(-jnp.abs(z)))
            sps.append(jnp.where(causal, sp, 0.0) if masked else sp)
        sufs = []
        for h in heads:
            hi, lo = _split_bf16(sps[h])
            hilo = jnp.concatenate([hi, lo], axis=1)
            sufs.append(_dot(hilo, tri2) + jnp.tile(carries[h], (1, kb // LANES)))
        pvs = []
        for h in heads:
            a = jnp.exp(zs[h] - sufs[h])
            if masked:
                a = jnp.where(causal, a, 0.0)
            pvs.append(_dot(a.astype(BF16), v_pair[h // 2]))
        for i in range(PAIRS):
            acc_scr[i] = acc_scr[i] + jnp.where(even_lanes, pvs[2 * i], pvs[2 * i + 1])
        for h in heads:
            carry_scr[h] = carries[h] + jnp.sum(sps[h], axis=-1, keepdims=True)

    def masked_step(jj, c):
        block(nblk - 1 - jj, True)
        return c

    def full_step(jj, c):
        block(n_full - 1 - jj, False)
        return c

    lax.fori_loop(0, nblk - n_full, masked_step, 0)
    lax.fori_loop(0, n_full, full_step, 0)
    for i in range(PAIRS):
        o_ref[0, :, i * LANES:(i + 1) * LANES] = acc_scr[i].astype(o_ref.dtype)


def _stick(q, k, v, *, tq, kb, past):
    bsz, t_len, _ = q.shape
    s_pad = k.shape[1]
    kern = functools.partial(_stick_kernel, tq=tq, kb=kb, past=past)
    kmap = lambda b, q: (b, 0, 0)
    return pl.pallas_call(
        kern,
        grid=(bsz, t_len // tq),
        in_specs=[
            pl.BlockSpec((1, tq, B_WIDTH), lambda b, q: (b, q, 0)),
            pl.BlockSpec((1, s_pad, B_WIDTH), kmap),
            pl.BlockSpec((1, s_pad, B_WIDTH), kmap),
            pl.BlockSpec((2 * kb, kb), lambda b, q: (0, 0)),
        ],
        out_specs=pl.BlockSpec((1, tq, B_WIDTH), lambda b, q: (b, q, 0)),
        out_shape=jax.ShapeDtypeStruct((bsz, t_len, B_WIDTH), BF16),
        scratch_shapes=[
            pltpu.VMEM((B_HEADS, tq, LANES), F32),
            pltpu.VMEM((PAIRS, tq, LANES), F32),
        ],
        compiler_params=pltpu.CompilerParams(dimension_semantics=("arbitrary", "arbitrary"),
                                             vmem_limit_bytes=VMEM_LIMIT),
        name="stick_attention",
    )(q, k, v, jnp.tile(_lower_tri(kb), (2, 1)))


def _merge_kernel(x_ref, oa_ref, ob_ref, gates_ref, mod_ref, wua_ref, wub_ref, wout_ref, n2g_ref, wr_ref, br_ref,
                  y_ref, h2_ref, comb_ref, *, nb):
    g1 = mod_ref[:, 2, :]
    sh2 = mod_ref[:, 3, :]
    sc2 = mod_ref[:, 4, :]
    gates = gates_ref[...].astype(F32)
    merged = gates[:, :D_MODEL] * _dot(oa_ref[...], wua_ref[...]) + gates[:, D_MODEL:] * _dot(ob_ref[...], wub_ref[...])
    y = x_ref[...] + _rows_times(_dot(merged.astype(BF16), wout_ref[...]), g1, nb)
    y_ref[...] = y
    h2 = _rows_plus(_rows_times(_rms_rows(y) * n2g_ref[...], 1.0 + sc2, nb), sh2, nb)
    h2_ref[...] = h2.astype(BF16)

    logits = _dot3(h2, wr_ref[...]) + br_ref[...]
    lane = lax.broadcasted_iota(I32, logits.shape, 1).astype(F32)
    far = float(ROUTER_LANES)
    is_g = lane < float(N_GROUPS)
    gl = jnp.where(is_g, logits, -jnp.inf)
    gmax = jnp.max(gl, axis=-1, keepdims=True)
    gsel = jnp.min(jnp.where(is_g & (gl == gmax), lane, far), axis=-1, keepdims=True)
    p_group = 1.0 / jnp.sum(jnp.where(is_g, jnp.exp(gl - gmax), 0.0), axis=-1, keepdims=True)
    e_lo = float(EXPERT_LANE0) + gsel * float(EXPERTS_PER_GROUP)
    in_grp = (lane >= e_lo) & (lane < e_lo + float(EXPERTS_PER_GROUP))
    el = jnp.where(in_grp, logits, -jnp.inf)
    v1 = jnp.max(el, axis=-1, keepdims=True)
    i1 = jnp.min(jnp.where(in_grp & (el == v1), lane, far), axis=-1, keepdims=True)
    el2 = jnp.where(lane == i1, -jnp.inf, el)
    v2 = jnp.max(el2, axis=-1, keepdims=True)
    i2 = jnp.min(jnp.where(in_grp & (lane != i1) & (el2 == v2), lane, far), axis=-1, keepdims=True)
    e2 = jnp.exp(v2 - v1)
    w1 = p_group / (1.0 + e2)
    w2 = p_group * e2 / (1.0 + e2)
    comb_ref[...] = jnp.where(lane == i1, w1, 0.0) + jnp.where(lane == i2, w2, 0.0)


def _merge(x2, oa, ob, gates, mod, wua, wub, wout, n2g, wr, br, *, t_len, tm):
    n = x2.shape[0]
    nb = max(1, tm // t_len)
    tiles_per_batch = max(1, t_len // tm)
    row = lambda i: (i, 0)
    const2 = lambda i: (0, 0)
    return pl.pallas_call(
        functools.partial(_merge_kernel, nb=nb),
        grid=(n // tm,),
        in_specs=[
            pl.BlockSpec((tm, D_MODEL), row),
            pl.BlockSpec((tm, A_WIDTH), row),
            pl.BlockSpec((tm, B_WIDTH), row),
            pl.BlockSpec((tm, 2 * D_MODEL), row),
            pl.BlockSpec((nb, 6, D_MODEL), lambda i: (i // tiles_per_batch, 0, 0)),
            pl.BlockSpec(wua.shape, const2),
            pl.BlockSpec(wub.shape, const2),
            pl.BlockSpec(wout.shape, const2),
            pl.BlockSpec((1, D_MODEL), const2),
            pl.BlockSpec(wr.shape, const2),
            pl.BlockSpec((1, ROUTER_LANES), const2),
        ],
        out_specs=[pl.BlockSpec((tm, D_MODEL), row), pl.BlockSpec((tm, D_MODEL), row),
                   pl.BlockSpec((tm, ROUTER_LANES), row)],
        out_shape=[jax.ShapeDtypeStruct((n, D_MODEL), F32), jax.ShapeDtypeStruct((n, D_MODEL), BF16),
                   jax.ShapeDtypeStruct((n, ROUTER_LANES), F32)],
        compiler_params=pltpu.CompilerParams(dimension_semantics=("arbitrary",), vmem_limit_bytes=VMEM_LIMIT),
        name="merge_router",
    )(x2, oa, ob, gates, mod, wua, wub, wout, n2g, wr, br)


def _moe_kernel(y_ref, h2_ref, comb_ref, mod_ref, wg_ref, wu_ref, wd_ref, o_ref, acc_scr, *, nb):
    e = pl.program_id(1)

    @pl.when(e == 0)
    def _():
        acc_scr[...] = jnp.zeros(acc_scr.shape, F32)

    h2 = h2_ref[...]
    comb = comb_ref[...]
    lane = lax.broadcasted_iota(I32, comb.shape, 1)
    experts = range(wg_ref.shape[0])
    acts = [_silu(_dot(h2, wg_ref[k].astype(BF16))) * _dot(h2, wu_ref[k].astype(BF16)) for k in experts]
    contribs = [_dot(acts[k].astype(BF16), wd_ref[k].astype(BF16)) for k in experts]
    total = acc_scr[...]
    for k in experts:
        col = e * len(experts) + k + EXPERT_LANE0
        w_e = jnp.sum(jnp.where(lane == col, comb, 0.0), axis=-1, keepdims=True)
        total = total + w_e * contribs[k]
    acc_scr[...] = total

    @pl.when(e == pl.num_programs(1) - 1)
    def _():
        o_ref[...] = y_ref[...] + _rows_times(acc_scr[...], mod_ref[:, 5, :], nb)


def _moe(y, h2, comb, mod, w_gate, w_up, w_down, *, t_len, tm):
    n = y.shape[0]
    nb = max(1, tm // t_len)
    tiles_per_batch = max(1, t_len // tm)
    row = lambda i, e: (i, 0)
    return pl.pallas_call(
        functools.partial(_moe_kernel, nb=nb),
        grid=(n // tm, N_EXPERTS // EXPERTS_PER_STEP),
        in_specs=[
            pl.BlockSpec((tm, D_MODEL), row),
            pl.BlockSpec((tm, D_MODEL), row),
            pl.BlockSpec((tm, ROUTER_LANES), row),
            pl.BlockSpec((nb, 6, D_MODEL), lambda i, e: (i // tiles_per_batch, 0, 0)),
            pl.BlockSpec((EXPERTS_PER_STEP, D_MODEL, D_EXPERT), lambda i, e: (e, 0, 0)),
            pl.BlockSpec((EXPERTS_PER_STEP, D_MODEL, D_EXPERT), lambda i, e: (e, 0, 0)),
            pl.BlockSpec((EXPERTS_PER_STEP, D_EXPERT, D_MODEL), lambda i, e: (e, 0, 0)),
        ],
        out_specs=pl.BlockSpec((tm, D_MODEL), row),
        out_shape=jax.ShapeDtypeStruct((n, D_MODEL), F32),
        scratch_shapes=[pltpu.VMEM((tm, D_MODEL), F32)],
        compiler_params=pltpu.CompilerParams(dimension_semantics=("arbitrary", "arbitrary"),
                                             vmem_limit_bytes=VMEM_LIMIT),
        name="moe_dense",
    )(y, h2, comb, mod, w_gate, w_up, w_down)


def _rope_tables(pos):
    half = ROT_DIM // 2
    freqs = ROPE_THETA ** (-jnp.arange(0, ROT_DIM, 2, dtype=F32) / ROT_DIM)
    ang = pos.astype(F32)[:, None] * freqs[None, :]
    cos, sin = jnp.cos(ang), jnp.sin(ang)
    dh = jnp.arange(LANES) % HEAD_DIM
    fi = dh % half
    cos_t = jnp.where(dh < ROT_DIM, cos[:, fi], 1.0)
    sin_lo = jnp.where(dh < half, -sin[:, fi], 0.0)
    sin_hi = jnp.where((dh >= half) & (dh < ROT_DIM), sin[:, fi], 0.0)
    return jnp.stack([cos_t, sin_lo, sin_hi]).astype(F32)


def _round_up(a, b):
    return (a + b - 1) // b * b


def _layer(x, mod, past_ak, past_av, past_aki, past_bk, past_bv, w, *, cfg):
    bsz, t_len, _ = x.shape
    past = past_ak.shape[1]
    s_real = past + t_len
    n = bsz * t_len
    x2 = x.reshape(n, D_MODEL)
    pos = past + jnp.arange(t_len, dtype=jnp.int32)

    tm = cfg["tm"]
    tab = jnp.tile(_rope_tables(pos), (1, max(1, tm // t_len), 1))
    pr = _proj(x2, mod, w["n1g"], w["wa"], w["wv"], w["wi"], w["wg"], w["bd"], w["gqk"], tab, t_len=t_len, tm=tm)

    new_ak = pr["ka"].reshape(bsz, t_len, A_HEADS, HEAD_DIM)
    new_av = pr["va"].reshape(bsz, t_len, A_HEADS, HEAD_DIM)
    new_aki = pr["ki2"][:, :IDX_DIM].reshape(bsz, t_len, IDX_DIM)
    new_bk = pr["kb"].reshape(bsz, t_len, B_HEADS, HEAD_DIM)
    new_bv = pr["vb"].reshape(bsz, t_len, B_HEADS, HEAD_DIM)

    def all_keys(past_x, new_bf16, s_pad):
        width = new_bf16.shape[-1]
        parts = [new_bf16.reshape(bsz, t_len, width)]
        if past:
            parts.insert(0, past_x.astype(BF16))
        if s_pad > s_real:
            parts.append(jnp.zeros((bsz, s_pad - s_real, width), BF16))
        return parts[0] if len(parts) == 1 else jnp.concatenate(parts, axis=1)

    tq, kb = cfg["dsa_tq"], cfg["dsa_kb"]
    s_pad = _round_up(s_real, kb)
    topk = min(TOPK_MAX, s_real // 4)
    past_ki2 = jnp.concatenate([past_aki, past_aki], axis=-1) if past else None
    oa = _dsa(pr["qi"].reshape(bsz, t_len, -1), pr["wi"].reshape(bsz, t_len, LANES), pr["qa"].reshape(bsz, t_len, -1),
              all_keys(past_ki2, pr["ki2b"], s_pad),
              all_keys(past_ak.reshape(bsz, past, A_WIDTH), pr["kab"], s_pad),
              all_keys(past_av.reshape(bsz, past, A_WIDTH), pr["vab"], s_pad),
              tq=tq, kb=kb, s_real=s_real, past=past, topk=topk)

    tqb, kbb = cfg["sb_tq"], cfg["sb_kb"]
    s_pad_b = _round_up(s_real, kbb)
    ob = _stick(pr["qb"].reshape(bsz, t_len, -1),
                all_keys(past_bk.reshape(bsz, past, B_WIDTH), pr["kbb"], s_pad_b),
                all_keys(past_bv.reshape(bsz, past, B_WIDTH), pr["vbb"], s_pad_b),
                tq=tqb, kb=kbb, past=past)

    y, h2, comb = _merge(x2, oa.reshape(n, A_WIDTH), ob.reshape(n, B_WIDTH), pr["gates"], mod,
                         w["wua"], w["wub"], w["wout"], w["n2g"], w["wr"], w["br"], t_len=t_len, tm=cfg["tm_merge"])
    out = _moe(y, h2, comb, mod, w["w_gate"], w["w_up"], w["w_down"], t_len=t_len, tm=cfg["tm_moe"])
    return out.reshape(bsz, t_len, D_MODEL), (new_ak, new_av, new_aki, new_bk, new_bv)


def _prep_weights(w_in, norm1_g, qnorm_g, knorm_g, w_up_a, w_up_b, w_out, norm2_g, w_rg, b_rg, w_re, b_re,
                  w_e_gate, w_e_up, w_e_down):
    o = 0
    seg = {}
    for name, width in (("qa", A_WIDTH), ("ka", A_WIDTH), ("va", A_WIDTH), ("qi", IDX_HEADS * IDX_DIM),
                        ("ki", IDX_DIM), ("wi", IDX_HEADS), ("qb", B_WIDTH), ("kb", B_WIDTH), ("vb", B_WIDTH),
                        ("gates", 2 * D_MODEL)):
        seg[name] = w_in[:, o:o + width]
        o += width
    pad_i = jnp.zeros((D_MODEL, LANES - IDX_HEADS), w_in.dtype)
    hd = jnp.arange(2 * A_WIDTH) // HEAD_DIM
    n_r = N_GROUPS + N_EXPERTS
    return dict(
        n1g=norm1_g.reshape(1, D_MODEL),
        wa=jnp.concatenate([seg["qa"], seg["ka"]], axis=1).astype(BF16),
        wv=jnp.concatenate([seg["va"], seg["qb"], seg["kb"], seg["vb"]], axis=1).astype(BF16),
        wi=jnp.concatenate([seg["qi"], seg["ki"], seg["ki"], seg["wi"], pad_i], axis=1).astype(BF16),
        wg=seg["gates"].astype(BF16),
        bd=((hd[:, None] == hd[None, :]).astype(F32) / HEAD_DIM).astype(BF16),
        gqk=jnp.concatenate([jnp.tile(qnorm_g, A_HEADS), jnp.tile(knorm_g, A_HEADS)]).reshape(1, 2 * A_WIDTH),
        wua=w_up_a.astype(BF16), wub=w_up_b.astype(BF16), wout=w_out.astype(BF16),
        n2g=norm2_g.reshape(1, D_MODEL),
        wr=jnp.pad(jnp.concatenate([w_rg, w_re], axis=1), ((0, 0), (0, ROUTER_LANES - n_r))),
        br=jnp.pad(jnp.concatenate([b_rg, b_re]), (0, ROUTER_LANES - n_r)).reshape(1, ROUTER_LANES),
        w_gate=w_e_gate, w_up=w_e_up, w_down=w_e_down,
    )


def _group_cfg(t_len):
    if t_len >= 512:
        return dict(tm=256, tm_merge=256, tm_moe=1024, dsa_tq=256, dsa_kb=512, sb_tq=256, sb_kb=256)
    return dict(tm=512, tm_merge=512, tm_moe=512, dsa_tq=t_len, dsa_kb=384, sb_tq=t_len, sb_kb=256)


def kernel(x_prompt, x_sample, cache_a_k, cache_a_v, cache_a_kidx, cache_b_k, cache_b_v, c_prompt, c_sample,
           w_ada, b_ada, norm1_g, w_in, qnorm_g, knorm_g, w_up_a, w_up_b, w_out, norm2_g,
           w_rg, b_rg, w_re, b_re, w_e_gate, w_e_up, w_e_down):
    depth = w_ada.shape[0]
    bp, bs = x_prompt.shape[0], x_sample.shape[0]
    dt = x_prompt.dtype
    rows = _round_up(bp + bs, 8)
    empty_a = jnp.zeros((bp, 0, A_HEADS, HEAD_DIM), dt)
    empty_i = jnp.zeros((bp, 0, IDX_DIM), dt)
    empty_b = jnp.zeros((bp, 0, B_HEADS, HEAD_DIM), dt)
    y_p, y_s = x_prompt, x_sample
    rows_p, rows_s = [], []
    c_all = jnp.concatenate([c_prompt, c_sample, jnp.zeros((rows - bp - bs, D_MODEL), dt)], axis=0)
    for l in range(depth):
        w = _prep_weights(w_in[l], norm1_g[l], qnorm_g[l], knorm_g[l], w_up_a[l], w_up_b[l], w_out[l], norm2_g[l],
                          w_rg[l], b_rg[l], w_re[l], b_re[l], w_e_gate[l], w_e_up[l], w_e_down[l])
        mod = _ada(c_all, w_ada[l], b_ada[l].reshape(1, -1))
        mod_p = mod[:bp].reshape(bp, 6, D_MODEL)
        mod_s = mod[bp:bp + bs].reshape(bs, 6, D_MODEL)
        y_p, new_p = _layer(y_p, mod_p, empty_a, empty_a, empty_i, empty_b, empty_b, w,
                            cfg=_group_cfg(y_p.shape[1]))
        y_s, new_s = _layer(y_s, mod_s, cache_a_k[l], cache_a_v[l], cache_a_kidx[l], cache_b_k[l], cache_b_v[l], w,
                            cfg=_group_cfg(y_s.shape[1]))
        rows_p.append(new_p)
        rows_s.append(new_s)
    stack = lambda rws, i: jnp.stack([r[i] for r in rws])
    return (y_p, y_s) + tuple(stack(rows_p, i) for i in range(5)) + tuple(stack(rows_s, i) for i in range(5))
```

```python
import functools

import jax
import jax.numpy as jnp
from jax import lax
from jax.experimental import pallas as pl
from jax.experimental.pallas import tpu as pltpu

D_MODEL = 1024
CHUNK = 64
CHUNK_SHIFT = 6
HEAD_DIM = 64
A_HEADS = 8
B_HEADS = 8
IDX_HEADS = 4
IDX_DIM = 64
TOPK_MAX = 256
ROPE_THETA = 500000.0
ROT_DIM = HEAD_DIM // 4
N_GROUPS = 4
EXPERTS_PER_GROUP = 8
N_EXPERTS = N_GROUPS * EXPERTS_PER_GROUP
D_EXPERT = 256
RMS_EPS = 1e-6
A_WIDTH = A_HEADS * HEAD_DIM
B_WIDTH = B_HEADS * HEAD_DIM

LANES = 128
CACHE_OUTS = ("ka", "va", "kb", "vb")
IDX_SEG = 512
PAIRS = 4
COUNT_ROWS = 64
ROUTER_LANES = 128
EXPERT_LANE0 = N_GROUPS
EXPERTS_PER_STEP = 2
VMEM_LIMIT = 56 * 1024 * 1024

F32 = jnp.float32
BF16 = jnp.bfloat16
I32 = jnp.int32
NEG_BIG = -1e30
INT_MIN = -(2 ** 31)
KEY_NEG_INF = (0xFF800000 - (1 << 32)) ^ 0x7FFFFFFF


def _dot(a, b):
    return jnp.dot(a, b, preferred_element_type=F32)


def _dot_nt(a, b):
    return lax.dot_general(a, b, (((1,), (1,)), ((), ())), preferred_element_type=F32)


def _dot_tn(a, b):
    return lax.dot_general(a, b, (((0,), (0,)), ((), ())), preferred_element_type=F32)


def _split_bf16(a):
    hi = a.astype(BF16)
    lo = (a - hi.astype(F32)).astype(BF16)
    return hi, lo


def _dot3(a, b):
    ah, al = _split_bf16(a)
    bh, bl = _split_bf16(b)
    return _dot(ah, bh) + (_dot(ah, bl) + _dot(al, bh))


def _silu(x):
    return x * (1.0 / (1.0 + jnp.exp(-x)))


def _sigmoid(x):
    return 1.0 / (1.0 + jnp.exp(-x))


def _rows_times(x, vec, nb):
    if nb == 1:
        return x * vec
    tm, d = x.shape
    return (x.reshape(nb, tm // nb, d) * vec[:, None, :]).reshape(tm, d)


def _rows_plus(x, vec, nb):
    if nb == 1:
        return x + vec
    tm, d = x.shape
    return (x.reshape(nb, tm // nb, d) + vec[:, None, :]).reshape(tm, d)


def _rms_rows(x):
    return x * lax.rsqrt(jnp.mean(x * x, axis=-1, keepdims=True) + RMS_EPS)


def _ada_kernel(c_ref, w_ref, b_ref, o_ref):
    o_ref[...] = _dot3(_silu(c_ref[...]), w_ref[...]) + b_ref[...]


def _ada(c_all, w_ada, b_ada):
    rows = c_all.shape[0]
    n = w_ada.shape[1]
    tn = 1536
    return pl.pallas_call(
        _ada_kernel,
        grid=(n // tn,),
        in_specs=[pl.BlockSpec((rows, D_MODEL), lambda j: (0, 0)),
                  pl.BlockSpec((D_MODEL, tn), lambda j: (0, j)),
                  pl.BlockSpec((1, tn), lambda j: (0, j))],
        out_specs=pl.BlockSpec((rows, tn), lambda j: (0, j)),
        out_shape=jax.ShapeDtypeStruct((rows, n), F32),
        compiler_params=pltpu.CompilerParams(dimension_semantics=("arbitrary",), vmem_limit_bytes=VMEM_LIMIT),
        name="ada_mod",
    )(c_all, w_ada, b_ada)


def _rope_lanes(a, tab_ref, reps):
    width = a.shape[-1]
    cos = jnp.tile(tab_ref[0], (1, reps)) if reps > 1 else tab_ref[0]
    sin_lo = jnp.tile(tab_ref[1], (1, reps)) if reps > 1 else tab_ref[1]
    sin_hi = jnp.tile(tab_ref[2], (1, reps)) if reps > 1 else tab_ref[2]
    half = ROT_DIM // 2
    return a * cos + pltpu.roll(a, width - half, 1) * sin_lo + pltpu.roll(a, half, 1) * sin_hi


def _proj_kernel(x_ref, mod_ref, n1g_ref, wa_ref, wv_ref, wi_ref, wg_ref, bd_ref, gqk_ref, tab_ref,
                 qa_ref, ka_ref, kab_ref, va_ref, vab_ref, qb_ref, kb_ref, kbb_ref, vb_ref, vbb_ref,
                 qi_ref, ki2_ref, ki2b_ref, wi_out_ref, gates_ref, *, nb):
    x = x_ref[...]
    sh1 = mod_ref[:, 0, :]
    sc1 = mod_ref[:, 1, :]
    h = _rows_plus(_rows_times(_rms_rows(x) * n1g_ref[...], 1.0 + sc1, nb), sh1, nb)
    hb = h.astype(BF16)

    a = _dot(hb, wa_ref[...])
    ms = _dot((a * a).astype(BF16), bd_ref[...])
    a = a * lax.rsqrt(ms + RMS_EPS) * gqk_ref[...]
    a = _rope_lanes(a, tab_ref, a.shape[-1] // LANES)
    qa_ref[...] = (a[:, :A_WIDTH] * (HEAD_DIM ** -0.5)).astype(BF16)
    ka = a[:, A_WIDTH:]
    ka_ref[...] = ka.reshape(ka_ref.shape)
    kab_ref[...] = ka.astype(BF16)

    v = _dot(hb, wv_ref[...])
    va = v[:, :A_WIDTH]
    va_ref[...] = va.reshape(va_ref.shape)
    vab_ref[...] = va.astype(BF16)
    qb_ref[...] = (v[:, A_WIDTH:A_WIDTH + B_WIDTH] * (HEAD_DIM ** -0.5)).astype(BF16)
    kb = v[:, A_WIDTH + B_WIDTH:A_WIDTH + 2 * B_WIDTH]
    kb_ref[...] = kb.reshape(kb_ref.shape)
    kbb_ref[...] = kb.astype(BF16)
    vb = v[:, A_WIDTH + 2 * B_WIDTH:]
    vb_ref[...] = vb.reshape(vb_ref.shape)
    vbb_ref[...] = vb.astype(BF16)

    i = _dot(hb, wi_ref[...])
    nr = IDX_SEG - LANES
    ir = _rope_lanes(i[:, :nr], tab_ref, nr // LANES)
    nq = IDX_HEADS * IDX_DIM
    qi_ref[...] = (ir[:, :nq] * (IDX_DIM ** -0.5)).astype(BF16)
    ki2 = ir[:, nq:]
    ki2_ref[...] = ki2
    ki2b_ref[...] = ki2.astype(BF16)
    wi_out_ref[...] = i[:, nr:]

    gates_ref[...] = _sigmoid(_dot(hb, wg_ref[...])).astype(BF16)


def _proj(x2, mod, n1g, wa, wv, wi, wg, bd, gqk, tab, *, t_len, tm):
    n = x2.shape[0]
    nb = max(1, tm // t_len)
    tiles_per_batch = max(1, t_len // tm)
    tab_tiles = tab.shape[1] // tm
    row = lambda i: (i, 0)
    const2 = lambda i: (0, 0)
    tabmap = lambda i: (0, i % tab_tiles, 0)
    in_specs = [
        pl.BlockSpec((tm, D_MODEL), row),
        pl.BlockSpec((nb, 6, D_MODEL), lambda i: (i // tiles_per_batch, 0, 0)),
        pl.BlockSpec((1, D_MODEL), const2),
        pl.BlockSpec(wa.shape, const2),
        pl.BlockSpec(wv.shape, const2),
        pl.BlockSpec(wi.shape, const2),
        pl.BlockSpec(wg.shape, const2),
        pl.BlockSpec(bd.shape, const2),
        pl.BlockSpec((1, 2 * A_WIDTH), const2),
        pl.BlockSpec((3, tm, LANES), tabmap),
    ]
    outs = [
        ("qa", A_WIDTH, BF16), ("ka", A_WIDTH, F32), ("kab", A_WIDTH, BF16),
        ("va", A_WIDTH, F32), ("vab", A_WIDTH, BF16),
        ("qb", B_WIDTH, BF16), ("kb", B_WIDTH, F32), ("kbb", B_WIDTH, BF16),
        ("vb", B_WIDTH, F32), ("vbb", B_WIDTH, BF16),
        ("qi", IDX_HEADS * IDX_DIM, BF16), ("ki2", LANES, F32), ("ki2b", LANES, BF16), ("wi", LANES, F32),
        ("gates", 2 * D_MODEL, BF16),
    ]
    res = pl.pallas_call(
        functools.partial(_proj_kernel, nb=nb),
        grid=(n // tm,),
        in_specs=in_specs,
        out_specs=[pl.BlockSpec((tm, w // HEAD_DIM, HEAD_DIM), lambda i: (i, 0, 0)) if name in CACHE_OUTS
                   else pl.BlockSpec((tm, w), row) for name, w, _ in outs],
        out_shape=[jax.ShapeDtypeStruct((n, w // HEAD_DIM, HEAD_DIM) if name in CACHE_OUTS else (n, w), dt)
                   for name, w, dt in outs],
        compiler_params=pltpu.CompilerParams(dimension_semantics=("arbitrary",), vmem_limit_bytes=VMEM_LIMIT),
        name="in_proj",
    )(x2, mod, n1g, wa, wv, wi, wg, bd, gqk, tab)
    return dict(zip([o[0] for o in outs], res))


def _lower_tri(n):
    idx = jnp.arange(n)
    return (idx[:, None] >= idx[None, :]).astype(BF16)


def _head_halves(x):
    lane = lax.broadcasted_iota(I32, x.shape, 1)
    zero = jnp.zeros_like(x)
    return jnp.where(lane < HEAD_DIM, x, zero), jnp.where(lane >= HEAD_DIM, x, zero)


def _dsa_kernel(qi_ref, w_ref, qa_ref, ki_ref, ka_ref, va_ref, tri_ref, o_ref,
                s_scr, bias_scr, m_scr, l_scr, acc_scr, *, tq, kb, s_real, past, topk):
    qt = pl.program_id(1)
    q0 = past + qt * tq
    kmax = jnp.minimum((lax.shift_right_logical(q0 + tq - 1, CHUNK_SHIFT) + 1) * CHUNK, s_real)
    nblk = (kmax + kb - 1) // kb
    qchunk = lax.shift_right_logical(q0 + lax.broadcasted_iota(I32, (1, tq), 1), CHUNK_SHIFT)
    kf = float(topk)
    chunks = lambda ref, i: ref[0, :, i * LANES:(i + 1) * LANES]
    qi_heads = [q for i in range(IDX_HEADS // 2) for q in _head_halves(chunks(qi_ref, i))]
    qa_heads = [q for i in range(PAIRS) for q in _head_halves(chunks(qa_ref, i))]
    w_rows = w_ref[0].T

    def fold_rows(x, op):
        part = op(x.reshape(kb // COUNT_ROWS, COUNT_ROWS, tq), axis=0)
        return op(part, axis=0, keepdims=True)

    def score_block(j, masked):
        r0 = pl.multiple_of(j * kb, kb)
        kib = ki_ref[0, pl.ds(r0, kb), :]
        acc = jnp.zeros((kb, tq), F32)
        for h in range(IDX_HEADS):
            rel = jnp.maximum(_dot_nt(kib, qi_heads[h]), 0.0)
            acc = acc + w_rows[h:h + 1, :] * rel
        s = acc + 0.0
        if masked:
            kpos = r0 + lax.broadcasted_iota(I32, (kb, tq), 0)
            adm = (lax.shift_right_logical(kpos, CHUNK_SHIFT) <= qchunk) & (kpos < s_real)
            s = jnp.where(adm, s, -jnp.inf)
        s_scr[pl.ds(r0, kb), :] = s

    n_open = jnp.minimum((lax.shift_right_logical(q0, CHUNK_SHIFT) + 1) * CHUNK, s_real) // kb

    def open_step(j, c):
        score_block(j, False)
        return c

    def masked_step(j, c):
        score_block(j, True)
        return c

    lax.fori_loop(0, n_open, open_step, 0)
    lax.fori_loop(n_open, nblk, masked_step, 0)

    def count(pred):
        def body(j, acc):
            r0 = pl.multiple_of(j * kb, kb)
            for u in range(kb // COUNT_ROWS):
                slab = s_scr[pl.ds(pl.multiple_of(r0 + u * COUNT_ROWS, COUNT_ROWS), COUNT_ROWS), :]
                acc = acc + jnp.where(pred(slab), 1.0, 0.0)
            return acc
        acc = lax.fori_loop(0, nblk, body, jnp.zeros((COUNT_ROWS, tq), F32))
        return jnp.sum(acc, axis=0, keepdims=True)

    def key_to_score(key):
        return pltpu.bitcast(jnp.where(key < 0, key ^ 0x7FFFFFFF, key), F32)

    c0 = count(lambda blk: blk >= 0.0)
    t0 = jnp.where(c0 >= kf, 0, INT_MIN).astype(I32)

    def value_bit(i, t):
        cand = t | jnp.left_shift(jnp.int32(1), 30 - i)
        cand_f = key_to_score(cand)
        return jnp.where(count(lambda blk: blk >= cand_f) >= kf, cand, t)

    t_key = lax.fori_loop(0, 31, value_bit, t0)
    t = key_to_score(jnp.maximum(t_key, KEY_NEG_INF))

    need = kf - count(lambda blk: blk > t)
    tri = tri_ref[...]

    def select_block(j, seen):
        r0 = pl.multiple_of(j * kb, kb)
        blk = s_scr[pl.ds(r0, kb), :]
        tie = jnp.where((blk == t) & (blk > -jnp.inf), 1.0, 0.0)
        rank = _dot(tri, tie.astype(BF16)) + seen
        chosen = (blk > t) | ((tie > 0.0) & (rank <= need))
        bias_scr[pl.ds(r0, kb), :] = jnp.where(chosen, 0.0, NEG_BIG)
        return seen + fold_rows(tie, jnp.sum)

    lax.fori_loop(0, nblk, select_block, jnp.zeros((1, tq), F32))

    m_scr[...] = jnp.full(m_scr.shape, NEG_BIG, F32)
    l_scr[...] = jnp.zeros(l_scr.shape, F32)
    acc_scr[...] = jnp.zeros(acc_scr.shape, F32)

    def attend_block(j, carry):
        r0 = pl.multiple_of(j * kb, kb)
        bias = bias_scr[pl.ds(r0, kb), :]
        heads = range(A_HEADS)
        m_old = m_scr[...]
        l_old = l_scr[...]
        k_pair = [ka_ref[0, pl.ds(r0, kb), i * LANES:(i + 1) * LANES] for i in range(PAIRS)]
        v_pair = [va_ref[0, pl.ds(r0, kb), i * LANES:(i + 1) * LANES] for i in range(PAIRS)]
        ss = [_dot_nt(k_pair[h // 2], qa_heads[h]) + bias for h in heads]
        m_new = [jnp.maximum(m_old[h:h + 1, :], fold_rows(ss[h], jnp.max)) for h in heads]
        ps = [jnp.exp(ss[h] - m_new[h]) for h in heads]
        pvs = [_dot_tn(v_pair[h // 2], ps[h].astype(BF16)) for h in heads]
        alphas = [jnp.exp(m_old[h:h + 1, :] - m_new[h]) for h in heads]
        for i in range(PAIRS):
            e, o = 2 * i, 2 * i + 1
            acc_scr[i] = (jnp.where(even_rows, alphas[e], alphas[o]) * acc_scr[i]
                          + jnp.where(even_rows, pvs[e], pvs[o]))
        for h in heads:
            l_scr[h:h + 1, :] = alphas[h] * l_old[h:h + 1, :] + fold_rows(ps[h], jnp.sum)
            m_scr[h:h + 1, :] = m_new[h]
        return carry

    even_rows = lax.broadcasted_iota(I32, (LANES, tq), 0) < HEAD_DIM
    lax.fori_loop(0, nblk, attend_block, 0)

    l_fin = l_scr[...]
    for i in range(PAIRS):
        denom = jnp.where(even_rows, l_fin[2 * i:2 * i + 1, :], l_fin[2 * i + 1:2 * i + 2, :])
        o_ref[0, :, i * LANES:(i + 1) * LANES] = (acc_scr[i] / denom).T.astype(o_ref.dtype)


def _dsa(qi, wi, qa, ki2, ka, va, *, tq, kb, s_real, past, topk):
    bsz, t_len, _ = qa.shape
    s_pad = ki2.shape[1]
    kern = functools.partial(_dsa_kernel, tq=tq, kb=kb, s_real=s_real, past=past, topk=topk)
    qmap = lambda b, q: (b, q, 0)
    kmap = lambda b, q: (b, 0, 0)
    return pl.pallas_call(
        kern,
        grid=(bsz, t_len // tq),
        in_specs=[
            pl.BlockSpec((1, tq, IDX_HEADS * IDX_DIM), qmap),
            pl.BlockSpec((1, tq, LANES), qmap),
            pl.BlockSpec((1, tq, A_WIDTH), qmap),
            pl.BlockSpec((1, s_pad, LANES), kmap),
            pl.BlockSpec((1, s_pad, A_WIDTH), kmap),
            pl.BlockSpec((1, s_pad, A_WIDTH), kmap),
            pl.BlockSpec((kb, kb), lambda b, q: (0, 0)),
        ],
        out_specs=pl.BlockSpec((1, tq, A_WIDTH), qmap),
        out_shape=jax.ShapeDtypeStruct((bsz, t_len, A_WIDTH), BF16),
        scratch_shapes=[
            pltpu.VMEM((s_pad, tq), F32),
            pltpu.VMEM((s_pad, tq), F32),
            pltpu.VMEM((A_HEADS, tq), F32),
            pltpu.VMEM((A_HEADS, tq), F32),
            pltpu.VMEM((PAIRS, LANES, tq), F32),
        ],
        compiler_params=pltpu.CompilerParams(dimension_semantics=("arbitrary", "arbitrary"),
                                             vmem_limit_bytes=VMEM_LIMIT),
        name="dsa_attention",
    )(qi, wi, qa, ki2, ka, va, _lower_tri(kb))


def _stick_kernel(q_ref, k_ref, v_ref, tri_ref, o_ref, carry_scr, acc_scr, *, tq, kb, past):
    qt = pl.program_id(1)
    q0 = past + qt * tq
    nblk = (q0 + tq - 1 + kb - 1) // kb
    n_full = q0 // kb
    qpos = q0 + lax.broadcasted_iota(I32, (tq, kb), 0)
    carry_scr[...] = jnp.zeros(carry_scr.shape, F32)
    acc_scr[...] = jnp.zeros(acc_scr.shape, F32)
    tri2 = tri_ref[...]
    heads = range(B_HEADS)
    q_heads = [q for i in range(PAIRS) for q in _head_halves(q_ref[0, :, i * LANES:(i + 1) * LANES])]
    even_lanes = lax.broadcasted_iota(I32, (tq, LANES), 1) < HEAD_DIM

    def block(j, masked):
        r0 = pl.multiple_of(j * kb, kb)
        causal = ((r0 + lax.broadcasted_iota(I32, (tq, kb), 1)) < qpos) if masked else None
        carries = [carry_scr[h] for h in heads]
        k_pair = [k_ref[0, pl.ds(r0, kb), i * LANES:(i + 1) * LANES] for i in range(PAIRS)]
        v_pair = [v_ref[0, pl.ds(r0, kb), i * LANES:(i + 1) * LANES] for i in range(PAIRS)]
        zs = [_dot_nt(q_heads[h], k_pair[h // 2]) for h in heads]
        sps = []
        for z in zs:
            sp = jnp.maximum(z, 0.0) + jnp.log(1.0 + jnp.exp(-jnp.abs(z)))
            sps.append(jnp.where(causal, sp, 0.0) if masked else sp)
        sufs = []
        for h in heads:
            hi, lo = _split_bf16(sps[h])
            hilo = jnp.concatenate([hi, lo], axis=1)
            sufs.append(_dot(hilo, tri2) + jnp.tile(carries[h], (1, kb // LANES)))
        pvs = []
        for h in heads:
            a = jnp.exp(zs[h] - sufs[h])
            if masked:
                a = jnp.where(causal, a, 0.0)
            pvs.append(_dot(a.astype(BF16), v_pair[h // 2]))
        for i in range(PAIRS):
            acc_scr[i] = acc_scr[i] + jnp.where(even_lanes, pvs[2 * i], pvs[2 * i + 1])
        for h in heads:
            carry_scr[h] = carries[h] + jnp.sum(sps[h], axis=-1, keepdims=True)

    def masked_step(jj, c):
        block(nblk - 1 - jj, True)
        return c

    def full_step(jj, c):
        block(n_full - 1 - jj, False)
        return c

    lax.fori_loop(0, nblk - n_full, masked_step, 0)
    lax.fori_loop(0, n_full, full_step, 0)
    for i in range(PAIRS):
        o_ref[0, :, i * LANES:(i + 1) * LANES] = acc_scr[i].astype(o_ref.dtype)


def _stick(q, k, v, *, tq, kb, past):
    bsz, t_len, _ = q.shape
    s_pad = k.shape[1]
    kern = functools.partial(_stick_kernel, tq=tq, kb=kb, past=past)
    kmap = lambda b, q: (b, 0, 0)
    return pl.pallas_call(
        kern,
        grid=(bsz, t_len // tq),
        in_specs=[
            pl.BlockSpec((1, tq, B_WIDTH), lambda b, q: (b, q, 0)),
            pl.BlockSpec((1, s_pad, B_WIDTH), kmap),
            pl.BlockSpec((1, s_pad, B_WIDTH), kmap),
            pl.BlockSpec((2 * kb, kb), lambda b, q: (0, 0)),
        ],
        out_specs=pl.BlockSpec((1, tq, B_WIDTH), lambda b, q: (b, q, 0)),
        out_shape=jax.ShapeDtypeStruct((bsz, t_len, B_WIDTH), BF16),
        scratch_shapes=[
            pltpu.VMEM((B_HEADS, tq, LANES), F32),
            pltpu.VMEM((PAIRS, tq, LANES), F32),
        ],
        compiler_params=pltpu.CompilerParams(dimension_semantics=("arbitrary", "arbitrary"),
                                             vmem_limit_bytes=VMEM_LIMIT),
        name="stick_attention",
    )(q, k, v, jnp.tile(_lower_tri(kb), (2, 1)))


def _merge_kernel(x_ref, oa_ref, ob_ref, gates_ref, mod_ref, wua_ref, wub_ref, wout_ref, n2g_ref, wr_ref, br_ref,
                  y_ref, h2_ref, comb_ref, *, nb):
    g1 = mod_ref[:, 2, :]
    sh2 = mod_ref[:, 3, :]
    sc2 = mod_ref[:, 4, :]
    gates = gates_ref[...].astype(F32)
    merged = gates[:, :D_MODEL] * _dot(oa_ref[...], wua_ref[...]) + gates[:, D_MODEL:] * _dot(ob_ref[...], wub_ref[...])
    y = x_ref[...] + _rows_times(_dot(merged.astype(BF16), wout_ref[...]), g1, nb)
    y_ref[...] = y
    h2 = _rows_plus(_rows_times(_rms_rows(y) * n2g_ref[...], 1.0 + sc2, nb), sh2, nb)
    h2_ref[...] = h2.astype(BF16)

    logits = _dot3(h2, wr_ref[...]) + br_ref[...]
    lane = lax.broadcasted_iota(I32, logits.shape, 1).astype(F32)
    far = float(ROUTER_LANES)
    is_g = lane < float(N_GROUPS)
    gl = jnp.where(is_g, logits, -jnp.inf)
    gmax = jnp.max(gl, axis=-1, keepdims=True)
    gsel = jnp.min(jnp.where(is_g & (gl == gmax), lane, far), axis=-1, keepdims=True)
    p_group = 1.0 / jnp.sum(jnp.where(is_g, jnp.exp(gl - gmax), 0.0), axis=-1, keepdims=True)
    e_lo = float(EXPERT_LANE0) + gsel * float(EXPERTS_PER_GROUP)
    in_grp = (lane >= e_lo) & (lane < e_lo + float(EXPERTS_PER_GROUP))
    el = jnp.where(in_grp, logits, -jnp.inf)
    v1 = jnp.max(el, axis=-1, keepdims=True)
    i1 = jnp.min(jnp.where(in_grp & (el == v1), lane, far), axis=-1, keepdims=True)
    el2 = jnp.where(lane == i1, -jnp.inf, el)
    v2 = jnp.max(el2, axis=-1, keepdims=True)
    i2 = jnp.min(jnp.where(in_grp & (lane != i1) & (el2 == v2), lane, far), axis=-1, keepdims=True)
    e2 = jnp.exp(v2 - v1)
    w1 = p_group / (1.0 + e2)
    w2 = p_group * e2 / (1.0 + e2)
    comb_ref[...] = jnp.where(lane == i1, w1, 0.0) + jnp.where(lane == i2, w2, 0.0)


def _merge(x2, oa, ob, gates, mod, wua, wub, wout, n2g, wr, br, *, t_len, tm):
    n = x2.shape[0]
    nb = max(1, tm // t_len)
    tiles_per_batch = max(1, t_len // tm)
    row = lambda i: (i, 0)
    const2 = lambda i: (0, 0)
    return pl.pallas_call(
        functools.partial(_merge_kernel, nb=nb),
        grid=(n // tm,),
        in_specs=[
            pl.BlockSpec((tm, D_MODEL), row),
            pl.BlockSpec((tm, A_WIDTH), row),
            pl.BlockSpec((tm, B_WIDTH), row),
            pl.BlockSpec((tm, 2 * D_MODEL), row),
            pl.BlockSpec((nb, 6, D_MODEL), lambda i: (i // tiles_per_batch, 0, 0)),
            pl.BlockSpec(wua.shape, const2),
            pl.BlockSpec(wub.shape, const2),
            pl.BlockSpec(wout.shape, const2),
            pl.BlockSpec((1, D_MODEL), const2),
            pl.BlockSpec(wr.shape, const2),
            pl.BlockSpec((1, ROUTER_LANES), const2),
        ],
        out_specs=[pl.BlockSpec((tm, D_MODEL), row), pl.BlockSpec((tm, D_MODEL), row),
                   pl.BlockSpec((tm, ROUTER_LANES), row)],
        out_shape=[jax.ShapeDtypeStruct((n, D_MODEL), F32), jax.ShapeDtypeStruct((n, D_MODEL), BF16),
                   jax.ShapeDtypeStruct((n, ROUTER_LANES), F32)],
        compiler_params=pltpu.CompilerParams(dimension_semantics=("arbitrary",), vmem_limit_bytes=VMEM_LIMIT),
        name="merge_router",
    )(x2, oa, ob, gates, mod, wua, wub, wout, n2g, wr, br)


def _moe_kernel(y_ref, h2_ref, comb_ref, mod_ref, wg_ref, wu_ref, wd_ref, o_ref, acc_scr, *, nb):
    e = pl.program_id(1)

    @pl.when(e == 0)
    def _():
        acc_scr[...] = jnp.zeros(acc_scr.shape, F32)

    h2 = h2_ref[...]
    comb = comb_ref[...]
    lane = lax.broadcasted_iota(I32, comb.shape, 1)
    experts = range(wg_ref.shape[0])
    acts = [_silu(_dot(h2, wg_ref[k].astype(BF16))) * _dot(h2, wu_ref[k].astype(BF16)) for k in experts]
    contribs = [_dot(acts[k].astype(BF16), wd_ref[k].astype(BF16)) for k in experts]
    total = acc_scr[...]
    for k in experts:
        col = e * len(experts) + k + EXPERT_LANE0
        w_e = jnp.sum(jnp.where(lane == col, comb, 0.0), axis=-1, keepdims=True)
        total = total + w_e * contribs[k]
    acc_scr[...] = total

    @pl.when(e == pl.num_programs(1) - 1)
    def _():
        o_ref[...] = y_ref[...] + _rows_times(acc_scr[...], mod_ref[:, 5, :], nb)


def _moe(y, h2, comb, mod, w_gate, w_up, w_down, *, t_len, tm):
    n = y.shape[0]
    nb = max(1, tm // t_len)
    tiles_per_batch = max(1, t_len // tm)
    row = lambda i, e: (i, 0)
    return pl.pallas_call(
        functools.partial(_moe_kernel, nb=nb),
        grid=(n // tm, N_EXPERTS // EXPERTS_PER_STEP),
        in_specs=[
            pl.BlockSpec((tm, D_MODEL), row),
            pl.BlockSpec((tm, D_MODEL), row),
            pl.BlockSpec((tm, ROUTER_LANES), row),
            pl.BlockSpec((nb, 6, D_MODEL), lambda i, e: (i // tiles_per_batch, 0, 0)),
            pl.BlockSpec((EXPERTS_PER_STEP, D_MODEL, D_EXPERT), lambda i, e: (e, 0, 0)),
            pl.BlockSpec((EXPERTS_PER_STEP, D_MODEL, D_EXPERT), lambda i, e: (e, 0, 0)),
            pl.BlockSpec((EXPERTS_PER_STEP, D_EXPERT, D_MODEL), lambda i, e: (e, 0, 0)),
        ],
        out_specs=pl.BlockSpec((tm, D_MODEL), row),
        out_shape=jax.ShapeDtypeStruct((n, D_MODEL), F32),
        scratch_shapes=[pltpu.VMEM((tm, D_MODEL), F32)],
        compiler_params=pltpu.CompilerParams(dimension_semantics=("arbitrary", "arbitrary"),
                                             vmem_limit_bytes=VMEM_LIMIT),
        name="moe_dense",
    )(y, h2, comb, mod, w_gate, w_up, w_down)


def _rope_tables(pos):
    half = ROT_DIM // 2
    freqs = ROPE_THETA ** (-jnp.arange(0, ROT_DIM, 2, dtype=F32) / ROT_DIM)
    ang = pos.astype(F32)[:, None] * freqs[None, :]
    cos, sin = jnp.cos(ang), jnp.sin(ang)
    dh = jnp.arange(LANES) % HEAD_DIM
    fi = dh % half
    cos_t = jnp.where(dh < ROT_DIM, cos[:, fi], 1.0)
    sin_lo = jnp.where(dh < half, -sin[:, fi], 0.0)
    sin_hi = jnp.where((dh >= half) & (dh < ROT_DIM), sin[:, fi], 0.0)
    return jnp.stack([cos_t, sin_lo, sin_hi]).astype(F32)


def _round_up(a, b):
    return (a + b - 1) // b * b


def _layer(x, mod, past_ak, past_av, past_aki, past_bk, past_bv, w, *, cfg):
    bsz, t_len, _ = x.shape
    past = past_ak.shape[1]
    s_real = past + t_len
    n = bsz * t_len
    x2 = x.reshape(n, D_MODEL)
    pos = past + jnp.arange(t_len, dtype=jnp.int32)

    tm = cfg["tm"]
    tab = jnp.tile(_rope_tables(pos), (1, max(1, tm // t_len), 1))
    pr = _proj(x2, mod, w["n1g"], w["wa"], w["wv"], w["wi"], w["wg"], w["bd"], w["gqk"], tab, t_len=t_len, tm=tm)

    new_ak = pr["ka"].reshape(bsz, t_len, A_HEADS, HEAD_DIM)
    new_av = pr["va"].reshape(bsz, t_len, A_HEADS, HEAD_DIM)
    new_aki = pr["ki2"][:, :IDX_DIM].reshape(bsz, t_len, IDX_DIM)
    new_bk = pr["kb"].reshape(bsz, t_len, B_HEADS, HEAD_DIM)
    new_bv = pr["vb"].reshape(bsz, t_len, B_HEADS, HEAD_DIM)

    def all_keys(past_x, new_bf16, s_pad):
        width = new_bf16.shape[-1]
        parts = [new_bf16.reshape(bsz, t_len, width)]
        if past:
            parts.insert(0, past_x.astype(BF16))
        if s_pad > s_real:
            parts.append(jnp.zeros((bsz, s_pad - s_real, width), BF16))
        return parts[0] if len(parts) == 1 else jnp.concatenate(parts, axis=1)

    tq, kb = cfg["dsa_tq"], cfg["dsa_kb"]
    s_pad = _round_up(s_real, kb)
    topk = min(TOPK_MAX, s_real // 4)
    past_ki2 = jnp.concatenate([past_aki, past_aki], axis=-1) if past else None
    oa = _dsa(pr["qi"].reshape(bsz, t_len, -1), pr["wi"].reshape(bsz, t_len, LANES), pr["qa"].reshape(bsz, t_len, -1),
              all_keys(past_ki2, pr["ki2b"], s_pad),
              all_keys(past_ak.reshape(bsz, past, A_WIDTH), pr["kab"], s_pad),
              all_keys(past_av.reshape(bsz, past, A_WIDTH), pr["vab"], s_pad),
              tq=tq, kb=kb, s_real=s_real, past=past, topk=topk)

    tqb, kbb = cfg["sb_tq"], cfg["sb_kb"]
    s_pad_b = _round_up(s_real, kbb)
    ob = _stick(pr["qb"].reshape(bsz, t_len, -1),
                all_keys(past_bk.reshape(bsz, past, B_WIDTH), pr["kbb"], s_pad_b),
                all_keys(past_bv.reshape(bsz, past, B_WIDTH), pr["vbb"], s_pad_b),
                tq=tqb, kb=kbb, past=past)

    y, h2, comb = _merge(x2, oa.reshape(n, A_WIDTH), ob.reshape(n, B_WIDTH), pr["gates"], mod,
                         w["wua"], w["wub"], w["wout"], w["n2g"], w["wr"], w["br"], t_len=t_len, tm=cfg["tm_merge"])
    out = _moe(y, h2, comb, mod, w["w_gate"], w["w_up"], w["w_down"], t_len=t_len, tm=cfg["tm_moe"])
    return out.reshape(bsz, t_len, D_MODEL), (new_ak, new_av, new_aki, new_bk, new_bv)


def _prep_weights(w_in, norm1_g, qnorm_g, knorm_g, w_up_a, w_up_b, w_out, norm2_g, w_rg, b_rg, w_re, b_re,
                  w_e_gate, w_e_up, w_e_down):
    o = 0
    seg = {}
    for name, width in (("qa", A_WIDTH), ("ka", A_WIDTH), ("va", A_WIDTH), ("qi", IDX_HEADS * IDX_DIM),
                        ("ki", IDX_DIM), ("wi", IDX_HEADS), ("qb", B_WIDTH), ("kb", B_WIDTH), ("vb", B_WIDTH),
                        ("gates", 2 * D_MODEL)):
        seg[name] = w_in[:, o:o + width]
        o += width
    pad_i = jnp.zeros((D_MODEL, LANES - IDX_HEADS), w_in.dtype)
    hd = jnp.arange(2 * A_WIDTH) // HEAD_DIM
    n_r = N_GROUPS + N_EXPERTS
    return dict(
        n1g=norm1_g.reshape(1, D_MODEL),
        wa=jnp.concatenate([seg["qa"], seg["ka"]], axis=1).astype(BF16),
        wv=jnp.concatenate([seg["va"], seg["qb"], seg["kb"], seg["vb"]], axis=1).astype(BF16),
        wi=jnp.concatenate([seg["qi"], seg["ki"], seg["ki"], seg["wi"], pad_i], axis=1).astype(BF16),
        wg=seg["gates"].astype(BF16),
        bd=((hd[:, None] == hd[None, :]).astype(F32) / HEAD_DIM).astype(BF16),
        gqk=jnp.concatenate([jnp.tile(qnorm_g, A_HEADS), jnp.tile(knorm_g, A_HEADS)]).reshape(1, 2 * A_WIDTH),
        wua=w_up_a.astype(BF16), wub=w_up_b.astype(BF16), wout=w_out.astype(BF16),
        n2g=norm2_g.reshape(1, D_MODEL),
        wr=jnp.pad(jnp.concatenate([w_rg, w_re], axis=1), ((0, 0), (0, ROUTER_LANES - n_r))),
        br=jnp.pad(jnp.concatenate([b_rg, b_re]), (0, ROUTER_LANES - n_r)).reshape(1, ROUTER_LANES),
        w_gate=w_e_gate, w_up=w_e_up, w_down=w_e_down,
    )


def _group_cfg(t_len):
    if t_len >= 512:
        return dict(tm=256, tm_merge=512, tm_moe=1024, dsa_tq=256, dsa_kb=512, sb_tq=256, sb_kb=256)
    return dict(tm=512, tm_merge=512, tm_moe=512, dsa_tq=t_len, dsa_kb=384, sb_tq=t_len, sb_kb=256)


def kernel(x_prompt, x_sample, cache_a_k, cache_a_v, cache_a_kidx, cache_b_k, cache_b_v, c_prompt, c_sample,
           w_ada, b_ada, norm1_g, w_in, qnorm_g, knorm_g, w_up_a, w_up_b, w_out, norm2_g,
           w_rg, b_rg, w_re, b_re, w_e_gate, w_e_up, w_e_down):
    depth = w_ada.shape[0]
    bp, bs = x_prompt.shape[0], x_sample.shape[0]
    dt = x_prompt.dtype
    rows = _round_up(bp + bs, 8)
    empty_a = jnp.zeros((bp, 0, A_HEADS, HEAD_DIM), dt)
    empty_i = jnp.zeros((bp, 0, IDX_DIM), dt)
    empty_b = jnp.zeros((bp, 0, B_HEADS, HEAD_DIM), dt)
    y_p, y_s = x_prompt, x_sample
    rows_p, rows_s = [], []
    c_all = jnp.concatenate([c_prompt, c_sample, jnp.zeros((rows - bp - bs, D_MODEL), dt)], axis=0)
    for l in range(depth):
        w = _prep_weights(w_in[l], norm1_g[l], qnorm_g[l], knorm_g[l], w_up_a[l], w_up_b[l], w_out[l], norm2_g[l],
                          w_rg[l], b_rg[l], w_re[l], b_re[l], w_e_gate[l], w_e_up[l], w_e_down[l])
        mod = _ada(c_all, w_ada[l], b_ada[l].reshape(1, -1))
        mod_p = mod[:bp].reshape(bp, 6, D_MODEL)
        mod_s = mod[bp:bp + bs].reshape(bs, 6, D_MODEL)
        y_p, new_p = _layer(y_p, mod_p, empty_a, empty_a, empty_i, empty_b, empty_b, w,
                            cfg=_group_cfg(y_p.shape[1]))
        y_s, new_s = _layer(y_s, mod_s, cache_a_k[l], cache_a_v[l], cache_a_kidx[l], cache_b_k[l], cache_b_v[l], w,
                            cfg=_group_cfg(y_s.shape[1]))
        rows_p.append(new_p)
        rows_s.append(new_s)
    stack = lambda rws, i: jnp.stack([r[i] for r in rws])
    return (y_p, y_s) + tuple(stack(rows_p, i) for i in range(5)) + tuple(stack(rows_s, i) for i in range(5))
```
